```python
import math
import jax, jax.numpy as jnp
from jax import lax
import numpy as np


D_MODEL = 2048
BATCH = 8
SEQ = 2048
DEPTH = 2

MEM_LEN = 256
D_MIX = D_MODEL
M_HEADS = 4
M_DV = D_MIX // 2 // M_HEADS
M_DQK = M_DV // 2
CONV_K = 4
CHUNK = 64
DA_HEADS = 8
DA_DV = D_MIX // 2 // DA_HEADS
DA_D = DA_DV // 2
Q_BLOCK = 128
NUM_BUCKETS = 32
MAX_EXACT = NUM_BUCKETS // 2
MAX_DISTANCE = 128
X_HEADS = 4
X_DH = D_MODEL // X_HEADS
D_FF = ((8 * D_MODEL // 3 + 255) // 256) * 256
EPS = 1e-6

M_QK = 2 * M_HEADS * M_DQK
M_V = M_HEADS * M_DV
M_O = M_HEADS * M_DV
M_G = 2 * M_HEADS
DA_QK = 2 * DA_HEADS * 2 * DA_D
DA_V = DA_HEADS * DA_DV
N_IN = M_QK + M_V + M_O + M_G + DA_QK + DA_V
SPLITS = [int(s) for s in np.cumsum([M_QK, M_V, M_O, M_G, DA_QK])]

kernel_name = 'hymba_mlstm_diffattn_macaron'


def rmsnorm(x, g):
    xf = x.astype(jnp.float32)
    y = xf * lax.rsqrt(jnp.mean(xf * xf, axis=-1, keepdims=True) + EPS)
    return (y * g.astype(jnp.float32)).astype(x.dtype)


def swiglu(x, w_gate, w_up, w_down):
    return (jax.nn.silu(x @ w_gate) * (x @ w_up)) @ w_down


def causal_conv(x, w, b):
    c = x.shape[-1]
    y = lax.conv_general_dilated(x, w[:, None, :].astype(x.dtype), window_strides=(1,),
                                 padding=[(CONV_K - 1, 0)],
                                 dimension_numbers=('NWC', 'WIO', 'NWC'),
                                 feature_group_count=c)
    return y + b.astype(x.dtype)


def mlstm(q, k, v, i_pre, f_pre):
    B, S, H, dqk = q.shape
    dv = v.shape[-1]
    nc = S // CHUNK

    def chunk(a):
        return a.astype(jnp.float32).reshape(B, nc, CHUNK, H, -1).transpose(1, 0, 3, 2, 4)

    def chunk_gate(a):
        return a.astype(jnp.float32).reshape(B, nc, CHUNK, H).transpose(1, 0, 3, 2)

    qc = chunk(q)
    kc = chunk(k) * (dqk ** -0.5)
    vc = chunk(v)
    ic = chunk_gate(i_pre)
    lfc = jax.nn.log_sigmoid(chunk_gate(f_pre))
    tri = jnp.tril(jnp.ones((CHUNK, CHUNK), dtype=bool))

    def step(carry, inp):
        C, n, m = carry
        qb, kb, vb, ib, lfb = inp
        b = jnp.cumsum(lfb, axis=-1)
        g = b[..., -1]
        D = jnp.where(tri, b[..., :, None] - b[..., None, :] + ib[..., None, :], -jnp.inf)
        inter = b + m[..., None]
        m_t = jnp.maximum(inter, jnp.max(D, axis=-1))
        w_inter = jnp.exp(inter - m_t)
        P = jnp.exp(D - m_t[..., None]) * jnp.einsum('bhld,bhsd->bhls', qb, kb)
        num = w_inter[..., None] * jnp.einsum('bhld,bhde->bhle', qb, C) + jnp.einsum('bhls,bhse->bhle', P, vb)
        den = w_inter * jnp.einsum('bhld,bhd->bhl', qb, n) + jnp.sum(P, axis=-1)
        h = num / jnp.maximum(jnp.abs(den), jnp.exp(-m_t))[..., None]
        s_w = g[..., None] - b + ib
        m_new = jnp.maximum(g + m, jnp.max(s_w, axis=-1))
        decay = jnp.exp(g + m - m_new)
        w_s = jnp.exp(s_w - m_new[..., None])
        C_new = decay[..., None, None] * C + jnp.einsum('bhs,bhsd,bhse->bhde', w_s, kb, vb)
        n_new = decay[..., None] * n + jnp.einsum('bhs,bhsd->bhd', w_s, kb)
        return (C_new, n_new, m_new), h

    init = (jnp.zeros((B, H, dqk, dv), jnp.float32),
            jnp.zeros((B, H, dqk), jnp.float32),
            jnp.zeros((B, H), jnp.float32))
    _, h = lax.scan(step, init, (qc, kc, vc, ic, lfc))
    return h.transpose(1, 0, 3, 2, 4).reshape(B, S, H, dv)


def t5_bucket(rel):
    n = jnp.maximum(rel, 0)
    nf = jnp.maximum(n, 1).astype(jnp.float32)
    large = MAX_EXACT + (jnp.log(nf / MAX_EXACT) / math.log(MAX_DISTANCE / MAX_EXACT)
                         * (NUM_BUCKETS - MAX_EXACT)).astype(jnp.int32)
    large = jnp.minimum(large, NUM_BUCKETS - 1)
    return jnp.where(n < MAX_EXACT, n, large)


def diff_attention(q, k, v, rel_bias, lam, lam_init, subln_g):
    B, H, _, S, d = q.shape
    scale = d ** -0.5
    outs = []
    for blk in range(S // Q_BLOCK):
        q0 = blk * Q_BLOCK
        q1 = q0 + Q_BLOCK
        qb = q[:, :, :, q0:q1]
        kb = k[:, :, :, :q1]
        vb = v[:, :, :q1]
        rel = (q0 + jnp.arange(Q_BLOCK))[:, None] - jnp.arange(q1)[None, :]
        bias = rel_bias[t5_bucket(rel)].transpose(2, 0, 1).astype(jnp.float32)
        logits = jnp.einsum('bhmqd,bhmkd->bhmqk', qb, kb).astype(jnp.float32) * scale + bias[None, :, None]
        logits = jnp.where(rel >= 0, logits, -jnp.inf)
        a = jax.nn.softmax(logits, axis=-1)
        a = a[:, :, 0] - lam * a[:, :, 1]
        outs.append(jnp.einsum('bhqk,bhkd->bhqd', a.astype(vb.dtype), vb))
    o = jnp.concatenate(outs, axis=2)
    o = rmsnorm(o, subln_g) * (1.0 - lam_init)
    return o.transpose(0, 2, 1, 3).reshape(B, S, H * DV_OF(o))


def DV_OF(o):
    return o.shape[-1]


def cross_attention(u, m, wq, wk, wv, wo):
    B, S, _ = u.shape
    M = m.shape[1]
    q = (u @ wq).reshape(B, S, X_HEADS, X_DH)
    k = (m @ wk).reshape(B, M, X_HEADS, X_DH)
    v = (m @ wv).reshape(B, M, X_HEADS, X_DH)
    logits = jnp.einsum('bshd,bmhd->bhsm', q, k).astype(jnp.float32) * (X_DH ** -0.5)
    a = jax.nn.softmax(logits, axis=-1)
    o = jnp.einsum('bhsm,bmhd->bshd', a.astype(v.dtype), v).reshape(B, S, D_MODEL)
    return o @ wo


def setup_inputs(seed: int = 0) -> dict:
    key = jax.random.key(seed)
    ks = iter(jax.random.split(key, 40))

    def nrm(shape, scale):
        return jax.random.normal(next(ks), shape, jnp.float32) * scale

    def gain(width=D_MODEL):
        return 1.0 + nrm((DEPTH, width), 0.02)

    return {
        'x': nrm((BATCH, SEQ, D_MODEL), 1.0),
        'mem': nrm((BATCH, MEM_LEN, D_MODEL), 1.0),
        'rel_bias': nrm((NUM_BUCKETS, DA_HEADS), 0.5),
        'ffn1_norm_pre': gain(),
        'ffn1_norm_post': gain(),
        'ffn1_w_gate': nrm((DEPTH, D_MODEL, D_FF), D_MODEL ** -0.5),
        'ffn1_w_up': nrm((DEPTH, D_MODEL, D_FF), D_MODEL ** -0.5),
        'ffn1_w_down': nrm((DEPTH, D_FF, D_MODEL), D_FF ** -0.5),
        'mix_norm_pre': gain(),
        'mix_norm_post': gain(),
        'w_in': nrm((DEPTH, D_MODEL, N_IN), D_MODEL ** -0.5),
        'conv_w': nrm((DEPTH, CONV_K, M_QK), CONV_K ** -0.5),
        'conv_b': nrm((DEPTH, M_QK), 0.01),
        'b_igate': nrm((DEPTH, M_HEADS), 0.1),
        'b_fgate': jnp.linspace(3.0, 6.0, M_HEADS, dtype=jnp.float32)[None, :] + nrm((DEPTH, M_HEADS), 0.1),
        'mlstm_norm': gain(M_V),
        'diff_lambda': nrm((DEPTH, 4, DA_D), 0.1),
        'diff_subln': gain(DA_DV),
        'w_out': nrm((DEPTH, D_MIX, D_MODEL), D_MIX ** -0.5),
        'xattn_norm_pre': gain(),
        'xattn_norm_post': gain(),
        'mem_norm': gain(),
        'xattn_wq': nrm((DEPTH, D_MODEL, D_MODEL), D_MODEL ** -0.5),
        'xattn_wk': nrm((DEPTH, D_MODEL, D_MODEL), D_MODEL ** -0.5),
        'xattn_wv': nrm((DEPTH, D_MODEL, D_MODEL), D_MODEL ** -0.5),
        'xattn_wo': nrm((DEPTH, D_MODEL, D_MODEL), D_MODEL ** -0.5),
        'ffn2_norm_pre': gain(),
        'ffn2_norm_post': gain(),
        'ffn2_w_gate': nrm((DEPTH, D_MODEL, D_FF), D_MODEL ** -0.5),
        'ffn2_w_up': nrm((DEPTH, D_MODEL, D_FF), D_MODEL ** -0.5),
        'ffn2_w_down': nrm((DEPTH, D_FF, D_MODEL), D_FF ** -0.5),
    }


def reference(x, mem, rel_bias,
              ffn1_norm_pre, ffn1_norm_post, ffn1_w_gate, ffn1_w_up, ffn1_w_down,
              mix_norm_pre, mix_norm_post, w_in, conv_w, conv_b, b_igate, b_fgate,
              mlstm_norm, diff_lambda, diff_subln, w_out,
              xattn_norm_pre, xattn_norm_post, mem_norm, xattn_wq, xattn_wk, xattn_wv, xattn_wo,
              ffn2_norm_pre, ffn2_norm_post, ffn2_w_gate, ffn2_w_up, ffn2_w_down):
    B, S, _ = x.shape
    for l in range(DEPTH):
        lam_init = 0.8 - 0.6 * math.exp(-0.3 * l)
        f = swiglu(rmsnorm(x, ffn1_norm_pre[l]), ffn1_w_gate[l], ffn1_w_up[l], ffn1_w_down[l])
        x = x + 0.5 * rmsnorm(f, ffn1_norm_post[l])
        u = rmsnorm(x, mix_norm_pre[l])
        z = u @ w_in[l]
        m_qk, m_v, m_o, m_g, d_qk, d_v = jnp.split(z, SPLITS, axis=-1)
        m_qk = jax.nn.silu(causal_conv(m_qk, conv_w[l], conv_b[l]))
        mq, mk = jnp.split(m_qk, 2, axis=-1)
        mq = mq.reshape(B, S, M_HEADS, M_DQK)
        mk = mk.reshape(B, S, M_HEADS, M_DQK)
        mv = m_v.reshape(B, S, M_HEADS, M_DV)
        i_pre = m_g[..., :M_HEADS] + b_igate[l]
        f_pre = m_g[..., M_HEADS:] + b_fgate[l]
        hm = mlstm(mq, mk, mv, i_pre, f_pre).astype(u.dtype)
        hm = rmsnorm(hm, mlstm_norm[l].reshape(M_HEADS, M_DV)).reshape(B, S, M_V)
        y_m = jax.nn.sigmoid(m_o) * hm
        d_qk = d_qk.reshape(B, S, 2, DA_HEADS, 2, DA_D)
        dq = d_qk[:, :, 0].transpose(0, 2, 3, 1, 4)
        dk = d_qk[:, :, 1].transpose(0, 2, 3, 1, 4)
        dv = d_v.reshape(B, S, DA_HEADS, DA_DV).transpose(0, 2, 1, 3)
        lam_p = diff_lambda[l].astype(jnp.float32)
        lam = jnp.exp(jnp.sum(lam_p[0] * lam_p[1])) - jnp.exp(jnp.sum(lam_p[2] * lam_p[3])) + lam_init
        y_d = diff_attention(dq, dk, dv, rel_bias, lam, lam_init, diff_subln[l])
        y = jnp.concatenate([y_m, y_d], axis=-1) @ w_out[l]
        x = x + rmsnorm(y, mix_norm_post[l])
        c = cross_attention(rmsnorm(x, xattn_norm_pre[l]), rmsnorm(mem, mem_norm[l]),
                            xattn_wq[l], xattn_wk[l], xattn_wv[l], xattn_wo[l])
        x = x + rmsnorm(c, xattn_norm_post[l])
        f = swiglu(rmsnorm(x, ffn2_norm_pre[l]), ffn2_w_gate[l], ffn2_w_up[l], ffn2_w_down[l])
        x = x + 0.5 * rmsnorm(f, ffn2_norm_post[l])
    return x
```

```python
import functools
import math

import numpy as np
import jax
import jax.numpy as jnp
from jax import lax
from jax.experimental import pallas as pl
from jax.experimental.pallas import tpu as pltpu

F32 = jnp.float32
BF16 = jnp.bfloat16
EPS = 1e-6

M_HEADS = 4
M_DQK = 128
M_DV = 256
CONV_K = 4
DA_HEADS = 8
DA_D = 64
DA_DV = 128
X_HEADS = 4
NUM_BUCKETS = 32
MAX_EXACT = NUM_BUCKETS // 2
MAX_DISTANCE = 128
DEPTH = 2

LANES = 128
V7X_VMEM_BYTES = 64 * 2 ** 20
MLSTM_CHUNK = 256
DA_BLOCK = 256


def _params(semantics, vmem_bytes):
    assert vmem_bytes <= V7X_VMEM_BYTES
    return pltpu.CompilerParams(dimension_semantics=semantics, vmem_limit_bytes=int(vmem_bytes))


def _rmsnorm(xf, g):
    return xf * lax.rsqrt(jnp.mean(xf * xf, axis=-1, keepdims=True) + EPS) * g


def _log_sigmoid(x):
    return jnp.minimum(x, 0.0) - jnp.log1p(jnp.exp(-jnp.abs(x)))


def _dot(a, b):
    return jnp.dot(a, b, preferred_element_type=F32)


def _dot_nt(a, b):
    return lax.dot_general(a, b, (((1,), (1,)), ((), ())), preferred_element_type=F32)


def _dot_tn(a, b):
    return lax.dot_general(a, b, (((0,), (0,)), ((), ())), preferred_element_type=F32)


def _ffn_body(x_ref, gpre_ref, gpost_ref, wg_ref, wu_ref, wd_ref, o_ref, xn_ref):
    j = pl.program_id(1)

    @pl.when(j == 0)
    def _():
        xn_ref[...] = _rmsnorm(x_ref[...], gpre_ref[...]).astype(BF16)

    xn = xn_ref[...]
    g = _dot(xn, wg_ref[...])
    u = _dot(xn, wu_ref[...])
    h = (g * jax.nn.sigmoid(g) * u).astype(BF16)
    p = _dot(h, wd_ref[...])

    @pl.when(j == 0)
    def _():
        o_ref[...] = p

    @pl.when(j > 0)
    def _():
        o_ref[...] += p

    @pl.when(j == pl.num_programs(1) - 1)
    def _():
        o_ref[...] = x_ref[...] + 0.5 * _rmsnorm(o_ref[...], gpost_ref[...])


def _ffn(x, g_pre, g_post, w_gate, w_up, w_down, *, bm=512, bf=512):
    m, d = x.shape
    f = w_gate.shape[1]
    bm, bf = min(bm, m), min(bf, f)
    vmem = 4 * bm * d * 4 + bm * d * 2 + 6 * d * bf * 2 + 4 * bm * bf * 4 + (4 << 20)
    return pl.pallas_call(
        _ffn_body,
        out_shape=jax.ShapeDtypeStruct((m, d), F32),
        grid=(m // bm, f // bf),
        in_specs=[
            pl.BlockSpec((bm, d), lambda i, j: (i, 0)),
            pl.BlockSpec((1, d), lambda i, j: (0, 0)),
            pl.BlockSpec((1, d), lambda i, j: (0, 0)),
            pl.BlockSpec((d, bf), lambda i, j: (0, j)),
            pl.BlockSpec((d, bf), lambda i, j: (0, j)),
            pl.BlockSpec((bf, d), lambda i, j: (j, 0)),
        ],
        out_specs=pl.BlockSpec((bm, d), lambda i, j: (i, 0)),
        scratch_shapes=[pltpu.VMEM((bm, d), BF16)],
        compiler_params=_params(("parallel", "arbitrary"), vmem),
        name="ffn",
    )(x, g_pre, g_post, w_gate, w_up, w_down)


def _normproj_body(x_ref, g_ref, w_ref, *rest, with_gate):
    if with_gate:
        wgate_ref, o_ref, gate_ref, xn_ref = rest
    else:
        o_ref, xn_ref = rest
    j = pl.program_id(1)

    @pl.when(j == 0)
    def _():
        xn = _rmsnorm(x_ref[...], g_ref[...]).astype(BF16)
        xn_ref[...] = xn
        if with_gate:
            gate_ref[...] = _dot(xn, wgate_ref[...])

    o_ref[...] = _dot(xn_ref[...], w_ref[...]).astype(o_ref.dtype)


def _normproj(x, g, w, w_gate=None, *, bm=1024, bn=512):
    m, d = x.shape
    n = w.shape[1]
    bm, bn = min(bm, m), min(bn, n)
    with_gate = w_gate is not None
    in_specs = [
        pl.BlockSpec((bm, d), lambda i, j: (i, 0)),
        pl.BlockSpec((1, d), lambda i, j: (0, 0)),
        pl.BlockSpec((d, bn), lambda i, j: (0, j)),
    ]
    out_shape = [jax.ShapeDtypeStruct((m, n), BF16)]
    out_specs = [pl.BlockSpec((bm, bn), lambda i, j: (i, j))]
    args = [x, g, w]
    if with_gate:
        in_specs.append(pl.BlockSpec((d, LANES), lambda i, j: (0, 0)))
        out_shape.append(jax.ShapeDtypeStruct((m, LANES), F32))
        out_specs.append(pl.BlockSpec((bm, LANES), lambda i, j: (i, 0)))
        args.append(w_gate)
    vmem = 2 * bm * d * 4 + bm * d * 2 + 2 * d * bn * 2 + 2 * bm * bn * 2 + bm * bn * 4 + (6 << 20)
    out = pl.pallas_call(
        functools.partial(_normproj_body, with_gate=with_gate),
        out_shape=out_shape,
        grid=(m // bm, n // bn),
        in_specs=in_specs,
        out_specs=out_specs,
        scratch_shapes=[pltpu.VMEM((bm, d), BF16)],
        compiler_params=_params(("parallel", "arbitrary"), vmem),
        name="normproj_gate" if with_gate else "normproj",
    )(*args)
    return out if with_gate else out[0]


def _outproj_body(*refs, n_parts):
    a_refs = refs[:n_parts]
    w_refs = refs[n_parts:2 * n_parts]
    x_ref, g_ref, o_ref = refs[2 * n_parts:]
    y = _dot(a_refs[0][...], w_refs[0][...])
    for a_ref, w_ref in zip(a_refs[1:], w_refs[1:]):
        y += _dot(a_ref[...], w_ref[...])
    o_ref[...] = x_ref[...] + _rmsnorm(y, g_ref[...])


def _outproj(x, g, parts, *, bm=512):
    m, d = x.shape
    bm = min(bm, m)
    acts = [a for a, _ in parts]
    ws = [w for _, w in parts]
    in_specs = [pl.BlockSpec((bm, a.shape[1]), lambda i: (i, 0)) for a in acts]
    in_specs += [pl.BlockSpec(w.shape, lambda i: (0, 0)) for w in ws]
    in_specs += [pl.BlockSpec((bm, d), lambda i: (i, 0)), pl.BlockSpec((1, d), lambda i: (0, 0))]
    k_total = sum(w.shape[0] for w in ws)
    vmem = 2 * k_total * d * 2 + 2 * bm * k_total * 2 + 5 * bm * d * 4 + (4 << 20)
    return pl.pallas_call(
        functools.partial(_outproj_body, n_parts=len(parts)),
        out_shape=jax.ShapeDtypeStruct((m, d), F32),
        grid=(m // bm,),
        in_specs=in_specs,
        out_specs=pl.BlockSpec((bm, d), lambda i: (i, 0)),
        compiler_params=_params(("parallel",), vmem),
        name="outproj",
    )(*acts, *ws, x, g)


def _conv_silu(raw, w, b):
    rows = lax.broadcasted_iota(jnp.int32, raw.shape, 0)
    acc = raw * w[CONV_K - 1:CONV_K, :] + b
    for shift in range(1, CONV_K):
        shifted = jnp.where(rows >= shift, pltpu.roll(raw, shift, 0), 0.0)
        acc += shifted * w[CONV_K - 1 - shift:CONV_K - shift, :]
    return acc * jax.nn.sigmoid(acc)


def _mlstm_body(q_ref, k_ref, v_ref, og_ref, gates_ref, gbias_ref, cwq_ref, cbq_ref, cwk_ref, cbk_ref,
                norm_ref, o_ref, qs_ref, ks_ref, gs_ref, *, chunk):
    s = q_ref.shape[0]
    head = pl.program_id(1)
    qs_ref[...] = _conv_silu(q_ref[...].astype(F32), cwq_ref[...], cbq_ref[...]).astype(BF16)
    ks_ref[...] = _conv_silu(k_ref[...].astype(F32), cwk_ref[...], cbk_ref[...]) * (M_DQK ** -0.5)
    gs_ref[...] = gates_ref[...] + gbias_ref[...]

    r_i = lax.broadcasted_iota(jnp.int32, (chunk, chunk), 0)
    c_i = lax.broadcasted_iota(jnp.int32, (chunk, chunk), 1)
    causal = r_i >= c_i
    tri = causal.astype(F32)
    lane_is_head = lax.broadcasted_iota(jnp.int32, (chunk, LANES), 1) == head
    sub_is_head = lax.broadcasted_iota(jnp.int32, (LANES, chunk), 0) == head
    gain = norm_ref[...]

    def step(c, carry):
        cmat, nvec, mrun = carry
        rows = pl.ds(pl.multiple_of(c * chunk, chunk), chunk)
        gc = gs_ref[rows, :]
        bcum = jnp.dot(tri, _log_sigmoid(gc), preferred_element_type=F32, precision=lax.Precision.HIGHEST)
        bsh = pltpu.roll(bcum, LANES - M_HEADS, 1)
        b_col = jnp.sum(jnp.where(lane_is_head, bsh, 0.0), axis=-1, keepdims=True)
        i_col = jnp.sum(jnp.where(lane_is_head, gc, 0.0), axis=-1, keepdims=True)
        r_row = jnp.sum(jnp.where(sub_is_head, (gc - bsh).T, 0.0), axis=0, keepdims=True)
        g_end = b_col[chunk - 1:chunk, :]

        qb = qs_ref[rows, :]
        kf = ks_ref[rows, :]
        vb = v_ref[rows, :]
        dmat = jnp.where(causal, b_col + r_row, -jnp.inf)
        inter = b_col + mrun
        m_t = jnp.maximum(inter, jnp.max(dmat, axis=-1, keepdims=True))
        w_inter = jnp.exp(inter - m_t)
        p = jnp.exp(dmat - m_t) * _dot_nt(qb, kf.astype(BF16))
        num = w_inter * _dot(qb, cmat.astype(BF16)) + _dot(p.astype(BF16), vb)
        den = (w_inter * jnp.sum(qb.astype(F32) * nvec, axis=-1, keepdims=True)
               + jnp.sum(p, axis=-1, keepdims=True))
        hc = num / jnp.maximum(jnp.abs(den), jnp.exp(-m_t))

        s_w = g_end - b_col + i_col
        m_new = jnp.maximum(g_end + mrun, jnp.max(s_w, axis=0, keepdims=True))
        decay = jnp.exp(g_end + mrun - m_new)
        kw = kf * jnp.exp(s_w - m_new)
        cmat = decay * cmat + _dot_tn(kw.astype(BF16), vb)
        nvec = decay * nvec + jnp.sum(kw, axis=0, keepdims=True)

        y = _rmsnorm(hc, gain) * jax.nn.sigmoid(og_ref[rows, :].astype(F32))
        o_ref[rows, :] = y.astype(o_ref.dtype)
        return cmat, nvec, m_new

    init = (jnp.zeros((M_DQK, M_DV), F32), jnp.zeros((1, M_DQK), F32), jnp.zeros((1, 1), F32))
    lax.fori_loop(0, s // chunk, step, init)


def _mlstm(z, gates, gate_bias, conv_w, conv_b, norm, *, batch, seq):
    chunk = min(MLSTM_CHUNK, seq)
    qk_blocks = M_HEADS
    v_block0 = 2 * M_HEADS * M_DQK // M_DV
    og_block0 = v_block0 + M_HEADS
    vmem = seq * (2 * 2 * M_DQK * 2 + 3 * 2 * M_DV * 2 + 3 * LANES * 4 + M_DQK * 6) + (24 << 20)
    return pl.pallas_call(
        functools.partial(_mlstm_body, chunk=chunk),
        out_shape=jax.ShapeDtypeStruct((batch * seq, M_HEADS * M_DV), BF16),
        grid=(batch, M_HEADS),
        in_specs=[
            pl.BlockSpec((seq, M_DQK), lambda b, h: (b, h)),
            pl.BlockSpec((seq, M_DQK), lambda b, h: (b, qk_blocks + h)),
            pl.BlockSpec((seq, M_DV), lambda b, h: (b, v_block0 + h)),
            pl.BlockSpec((seq, M_DV), lambda b, h: (b, og_block0 + h)),
            pl.BlockSpec((seq, LANES), lambda b, h: (b, 0)),
            pl.BlockSpec((1, LANES), lambda b, h: (0, 0)),
            pl.BlockSpec((CONV_K, M_DQK), lambda b, h: (0, h)),
            pl.BlockSpec((1, M_DQK), lambda b, h: (0, h)),
            pl.BlockSpec((CONV_K, M_DQK), lambda b, h: (0, qk_blocks + h)),
            pl.BlockSpec((1, M_DQK), lambda b, h: (0, qk_blocks + h)),
            pl.BlockSpec((1, M_DV), lambda b, h: (0, h)),
        ],
        out_specs=pl.BlockSpec((seq, M_DV), lambda b, h: (b, h)),
        scratch_shapes=[pltpu.VMEM((seq, M_DQK), BF16), pltpu.VMEM((seq, M_DQK), F32),
                        pltpu.VMEM((seq, LANES), F32)],
        compiler_params=_params(("parallel", "parallel"), vmem),
        name="mlstm",
    )(z, z, z, z, gates, gate_bias, conv_w, conv_b, conv_w, conv_b, norm)


def _t5_bucket_np(n):
    nf = np.maximum(n, 1).astype(np.float64)
    val = np.log(nf / MAX_EXACT) / math.log(MAX_DISTANCE / MAX_EXACT) * (NUM_BUCKETS - MAX_EXACT)
    frac = val[(n > MAX_EXACT) & (n < MAX_DISTANCE)]
    assert np.all(np.abs(frac - np.round(frac)) > 1e-3)
    large = np.minimum(MAX_EXACT + val.astype(np.int64), NUM_BUCKETS - 1)
    return np.where(n < MAX_EXACT, n, large)


def _bucket_tile(blk):
    rel = np.arange(blk)[:, None] + blk - np.arange(2 * blk)[None, :]
    return np.where(rel >= 0, _t5_bucket_np(np.maximum(rel, 0)), -1).astype(np.int32)


def _bias_body(rb_ref, bucket_ref, t_ref):
    h = pl.program_id(0)
    bucket = bucket_ref[...]
    acc = jnp.where(bucket < 0, -jnp.inf, 0.0).astype(F32)
    for b in range(NUM_BUCKETS):
        acc = jnp.where(bucket == b, rb_ref[h * NUM_BUCKETS + b], acc)
    t_ref[0] = acc


def _bias_tiles(rel_bias, blk):
    bucket = jnp.asarray(_bucket_tile(blk))
    return pl.pallas_call(
        _bias_body,
        out_shape=jax.ShapeDtypeStruct((DA_HEADS, blk, 2 * blk), F32),
        grid=(DA_HEADS,),
        in_specs=[pl.BlockSpec(memory_space=pltpu.SMEM),
                  pl.BlockSpec((blk, 2 * blk), lambda h: (0, 0))],
        out_specs=pl.BlockSpec((1, blk, 2 * blk), lambda h: (h, 0, 0)),
        compiler_params=_params(("arbitrary",), 16 << 20),
        name="t5_bias_tiles",
    )(rel_bias.T.reshape(-1), bucket)


def _diffattn_body(rb_ref, lam_ref, q_ref, k_ref, v_ref, t_ref, g_ref, o_ref, l1_ref, l2_ref,
                   *, blk, lam_init):
    head = pl.program_id(1)
    qi = pl.program_id(2)
    q = q_ref[...]
    lane = lax.broadcasted_iota(jnp.int32, q.shape, 1)
    q1 = jnp.where(lane < DA_D, q, jnp.zeros_like(q))
    q2 = jnp.where(lane >= DA_D, q, jnp.zeros_like(q))
    scale = DA_D ** -0.5
    far_bias = rb_ref[head * NUM_BUCKETS + NUM_BUCKETS - 1]
    assert blk >= MAX_DISTANCE

    def logits(kc, bias):
        kb = k_ref[pl.ds(pl.multiple_of(kc * blk, blk), blk), :]
        s1 = _dot_nt(q1, kb) * scale + bias
        s2 = _dot_nt(q2, kb) * scale + bias
        l1_ref[kc] = s1
        l2_ref[kc] = s2
        return jnp.max(s1, axis=-1, keepdims=True), jnp.max(s2, axis=-1, keepdims=True)

    def pass1(kc, carry):
        m1, m2 = carry
        bias = jnp.where(kc == qi - 1, t_ref[0, :, :blk], far_bias)
        n1, n2 = logits(kc, bias)
        return jnp.maximum(m1, n1), jnp.maximum(m2, n2)

    neg = jnp.full((blk, 1), -jnp.inf, F32)
    m1, m2 = lax.fori_loop(0, qi, pass1, (neg, neg))
    n1, n2 = logits(qi, t_ref[0, :, blk:])
    m1, m2 = jnp.maximum(m1, n1), jnp.maximum(m2, n2)

    def pass2(kc, carry):
        s1, s2, a1, a2 = carry
        vb = v_ref[pl.ds(pl.multiple_of(kc * blk, blk), blk), :]
        e1 = jnp.exp(l1_ref[kc] - m1)
        e2 = jnp.exp(l2_ref[kc] - m2)
        return (s1 + jnp.sum(e1, axis=-1, keepdims=True), s2 + jnp.sum(e2, axis=-1, keepdims=True),
                a1 + _dot(e1.astype(BF16), vb), a2 + _dot(e2.astype(BF16), vb))

    zs = jnp.zeros((blk, 1), F32)
    za = jnp.zeros((blk, DA_DV), F32)
    s1, s2, a1, a2 = lax.fori_loop(0, qi + 1, pass2, (zs, zs, za, za))

    lp = lam_ref[...]
    lam = (jnp.exp(jnp.sum(lp[0:1] * lp[1:2], axis=-1, keepdims=True))
           - jnp.exp(jnp.sum(lp[2:3] * lp[3:4], axis=-1, keepdims=True)) + lam_init)
    o = a1 / s1 - lam * (a2 / s2)
    o_ref[...] = (_rmsnorm(o, g_ref[...]) * (1.0 - lam_init)).astype(o_ref.dtype)


def _diffattn(z, tiles, rel_bias, lam_p, subln, *, batch, seq, lam_init, col0):
    blk = tiles.shape[1]
    nq = seq // blk
    q0 = col0 // LANES
    k0 = q0 + DA_HEADS
    v0 = k0 + DA_HEADS
    vmem = 2 * seq * blk * 4 + 4 * seq * LANES * 2 + 2 * blk * 2 * blk * 4 + (16 << 20)
    return pl.pallas_call(
        functools.partial(_diffattn_body, blk=blk, lam_init=lam_init),
        out_shape=jax.ShapeDtypeStruct((batch * seq, DA_HEADS * DA_DV), BF16),
        grid=(batch, DA_HEADS, nq),
        in_specs=[
            pl.BlockSpec(memory_space=pltpu.SMEM),
            pl.BlockSpec((4, DA_D), lambda b, h, i: (0, 0)),
            pl.BlockSpec((blk, LANES), lambda b, h, i: (b * nq + i, q0 + h)),
            pl.BlockSpec((seq, LANES), lambda b, h, i: (b, k0 + h)),
            pl.BlockSpec((seq, LANES), lambda b, h, i: (b, v0 + h)),
            pl.BlockSpec((1, blk, 2 * blk), lambda b, h, i: (h, 0, 0)),
            pl.BlockSpec((1, DA_DV), lambda b, h, i: (0, 0)),
        ],
        out_specs=pl.BlockSpec((blk, DA_DV), lambda b, h, i: (b * nq + i, h)),
        scratch_shapes=[pltpu.VMEM((nq, blk, blk), F32), pltpu.VMEM((nq, blk, blk), F32)],
        compiler_params=_params(("parallel", "parallel", "arbitrary"), vmem),
        name="diffattn",
    )(rel_bias.T.reshape(-1), lam_p, z, z, z, tiles, subln)


def _xattn_body(q_ref, k_ref, v_ref, o_ref):
    dh = q_ref.shape[1]
    s = _dot_nt(q_ref[...], k_ref[...]) * (dh ** -0.5)
    e = jnp.exp(s - jnp.max(s, axis=-1, keepdims=True))
    a = e / jnp.sum(e, axis=-1, keepdims=True)
    o_ref[...] = _dot(a.astype(BF16), v_ref[...]).astype(o_ref.dtype)


def _xattn(q, kv, *, batch, seq, mem_len, bq=512):
    d = q.shape[1]
    dh = d // X_HEADS
    bq = min(bq, seq)
    nq = seq // bq
    vmem = 4 * bq * dh * 2 + 4 * mem_len * dh * 2 + 4 * bq * mem_len * 4 + (8 << 20)
    return pl.pallas_call(
        _xattn_body,
        out_shape=jax.ShapeDtypeStruct((batch * seq, d), BF16),
        grid=(batch, X_HEADS, nq),
        in_specs=[
            pl.BlockSpec((bq, dh), lambda b, h, i: (b * nq + i, h)),
            pl.BlockSpec((mem_len, dh), lambda b, h, i: (b, h)),
            pl.BlockSpec((mem_len, dh), lambda b, h, i: (b, X_HEADS + h)),
        ],
        out_specs=pl.BlockSpec((bq, dh), lambda b, h, i: (b * nq + i, h)),
        compiler_params=_params(("parallel", "parallel", "parallel"), vmem),
        name="xattn",
    )(q, kv, kv)


def kernel(x, mem, rel_bias, ffn1_norm_pre, ffn1_norm_post, ffn1_w_gate, ffn1_w_up, ffn1_w_down, mix_norm_pre, mix_norm_post, w_in, conv_w, conv_b, b_igate, b_fgate, mlstm_norm, diff_lambda, diff_subln, w_out, xattn_norm_pre, xattn_norm_post, mem_norm, xattn_wq, xattn_wk, xattn_wv, xattn_wo, ffn2_norm_pre, ffn2_norm_post, ffn2_w_gate, ffn2_w_up, ffn2_w_down):
    batch, seq, d = x.shape
    mem_len = mem.shape[1]
    depth = w_in.shape[0]
    m_qk = 2 * M_HEADS * M_DQK
    m_v = M_HEADS * M_DV
    n_gate = 2 * M_HEADS
    gate0 = m_qk + 2 * m_v
    xf = x.reshape(batch * seq, d)
    memf = mem.reshape(batch * mem_len, d)
    row = lambda v: v.reshape(1, -1).astype(F32)
    bf = lambda w: w.astype(BF16)

    tiles = _bias_tiles(rel_bias, min(DA_BLOCK, seq))

    for l in range(depth):
        lam_init = 0.8 - 0.6 * math.exp(-0.3 * l)
        xf = _ffn(xf, row(ffn1_norm_pre[l]), row(ffn1_norm_post[l]),
                  bf(ffn1_w_gate[l]), bf(ffn1_w_up[l]), bf(ffn1_w_down[l]))

        w_l = w_in[l]
        w_main = bf(jnp.concatenate([w_l[:, :gate0], w_l[:, gate0 + n_gate:]], axis=1))
        w_gate = bf(jnp.pad(w_l[:, gate0:gate0 + n_gate], ((0, 0), (0, LANES - n_gate))))
        z, gates = _normproj(xf, row(mix_norm_pre[l]), w_main, w_gate)
        gate_bias = jnp.pad(jnp.concatenate([b_igate[l], b_fgate[l]]), (0, LANES - n_gate)).reshape(1, LANES)
        y_m = _mlstm(z, gates, gate_bias.astype(F32), conv_w[l].astype(F32), row(conv_b[l]),
                     row(mlstm_norm[l]), batch=batch, seq=seq)
        y_d = _diffattn(z, tiles, rel_bias, diff_lambda[l].astype(F32), row(diff_subln[l]),
                        batch=batch, seq=seq, lam_init=lam_init, col0=gate0)
        w_o = bf(w_out[l])
        xf = _outproj(xf, row(mix_norm_post[l]), [(y_m, w_o[:m_v]), (y_d, w_o[m_v:])])

        q = _normproj(xf, row(xattn_norm_pre[l]), bf(xattn_wq[l]))
        kv = _normproj(memf, row(mem_norm[l]), bf(jnp.concatenate([xattn_wk[l], xattn_wv[l]], axis=1)))
        c = _xattn(q, kv, batch=batch, seq=seq, mem_len=mem_len)
        xf = _outproj(xf, row(xattn_norm_post[l]), [(c, bf(xattn_wo[l]))])

        xf = _ffn(xf, row(ffn2_norm_pre[l]), row(ffn2_norm_post[l]),
                  bf(ffn2_w_gate[l]), bf(ffn2_w_up[l]), bf(ffn2_w_down[l]))
    return xf.reshape(batch, seq, d)
```

```python
import functools
import math

import numpy as np
import jax
import jax.numpy as jnp
from jax import lax
from jax.experimental import pallas as pl
from jax.experimental.pallas import tpu as pltpu

F32 = jnp.float32
BF16 = jnp.bfloat16
EPS = 1e-6
LOG2E = math.log2(math.e)

M_HEADS = 4
M_DQK = 128
M_DV = 256
CONV_K = 4
DA_HEADS = 8
DA_D = 64
DA_DV = 128
X_HEADS = 4
NUM_BUCKETS = 32
MAX_EXACT = NUM_BUCKETS // 2
MAX_DISTANCE = 128
DEPTH = 2

LANES = 128
V7X_VMEM_BYTES = 64 * 2 ** 20
MLSTM_CHUNK = 256
DA_BLOCK = 256


def _params(semantics, vmem_bytes):
    assert vmem_bytes <= V7X_VMEM_BYTES
    return pltpu.CompilerParams(dimension_semantics=semantics, vmem_limit_bytes=int(vmem_bytes))


def _rmsnorm(xf, g):
    return xf * lax.rsqrt(jnp.mean(xf * xf, axis=-1, keepdims=True) + EPS) * g


def _log_sigmoid(x):
    return jnp.minimum(x, 0.0) - jnp.log1p(jnp.exp(-jnp.abs(x)))


def _dot(a, b):
    return jnp.dot(a, b, preferred_element_type=F32)


def _dot_nt(a, b):
    return lax.dot_general(a, b, (((1,), (1,)), ((), ())), preferred_element_type=F32)


def _dot_tn(a, b):
    return lax.dot_general(a, b, (((0,), (0,)), ((), ())), preferred_element_type=F32)


def _ffn_body(x_ref, gpre_ref, gpost_ref, wg_ref, wu_ref, wd_ref, o_ref, xn_ref):
    j = pl.program_id(1)

    @pl.when(j == 0)
    def _():
        xn_ref[...] = _rmsnorm(x_ref[...], gpre_ref[...]).astype(BF16)

    xn = xn_ref[...]
    g = _dot(xn, wg_ref[...])
    u = _dot(xn, wu_ref[...])
    h = (g * jax.nn.sigmoid(g) * u).astype(BF16)
    p = _dot(h, wd_ref[...])

    @pl.when(j == 0)
    def _():
        o_ref[...] = p

    @pl.when(j > 0)
    def _():
        o_ref[...] += p

    @pl.when(j == pl.num_programs(1) - 1)
    def _():
        o_ref[...] = x_ref[...] + 0.5 * _rmsnorm(o_ref[...], gpost_ref[...])


def _ffn(x, g_pre, g_post, w_gate, w_up, w_down, *, bm=512, bf=512):
    m, d = x.shape
    f = w_gate.shape[1]
    bm, bf = min(bm, m), min(bf, f)
    vmem = 4 * bm * d * 4 + bm * d * 2 + 6 * d * bf * 2 + 4 * bm * bf * 4 + (4 << 20)
    return pl.pallas_call(
        _ffn_body,
        out_shape=jax.ShapeDtypeStruct((m, d), F32),
        grid=(m // bm, f // bf),
        in_specs=[
            pl.BlockSpec((bm, d), lambda i, j: (i, 0)),
            pl.BlockSpec((1, d), lambda i, j: (0, 0)),
            pl.BlockSpec((1, d), lambda i, j: (0, 0)),
            pl.BlockSpec((d, bf), lambda i, j: (0, j)),
            pl.BlockSpec((d, bf), lambda i, j: (0, j)),
            pl.BlockSpec((bf, d), lambda i, j: (j, 0)),
        ],
        out_specs=pl.BlockSpec((bm, d), lambda i, j: (i, 0)),
        scratch_shapes=[pltpu.VMEM((bm, d), BF16)],
        compiler_params=_params(("parallel", "arbitrary"), vmem),
        name="ffn",
    )(x, g_pre, g_post, w_gate, w_up, w_down)


def _normproj_body(x_ref, g_ref, w_ref, *rest, with_gate):
    if with_gate:
        wgate_ref, o_ref, gate_ref, xn_ref = rest
    else:
        o_ref, xn_ref = rest
    j = pl.program_id(1)

    @pl.when(j == 0)
    def _():
        xn = _rmsnorm(x_ref[...], g_ref[...]).astype(BF16)
        xn_ref[...] = xn
        if with_gate:
            gate_ref[...] = _dot(xn, wgate_ref[...])

    o_ref[...] = _dot(xn_ref[...], w_ref[...]).astype(o_ref.dtype)


def _normproj(x, g, w, w_gate=None, *, bm=1024, bn=512):
    m, d = x.shape
    n = w.shape[1]
    bm, bn = min(bm, m), min(bn, n)
    with_gate = w_gate is not None
    in_specs = [
        pl.BlockSpec((bm, d), lambda i, j: (i, 0)),
        pl.BlockSpec((1, d), lambda i, j: (0, 0)),
        pl.BlockSpec((d, bn), lambda i, j: (0, j)),
    ]
    out_shape = [jax.ShapeDtypeStruct((m, n), BF16)]
    out_specs = [pl.BlockSpec((bm, bn), lambda i, j: (i, j))]
    args = [x, g, w]
    if with_gate:
        in_specs.append(pl.BlockSpec((d, LANES), lambda i, j: (0, 0)))
        out_shape.append(jax.ShapeDtypeStruct((m, LANES), F32))
        out_specs.append(pl.BlockSpec((bm, LANES), lambda i, j: (i, 0)))
        args.append(w_gate)
    vmem = 2 * bm * d * 4 + bm * d * 2 + 2 * d * bn * 2 + 2 * bm * bn * 2 + bm * bn * 4 + (6 << 20)
    out = pl.pallas_call(
        functools.partial(_normproj_body, with_gate=with_gate),
        out_shape=out_shape,
        grid=(m // bm, n // bn),
        in_specs=in_specs,
        out_specs=out_specs,
        scratch_shapes=[pltpu.VMEM((bm, d), BF16)],
        compiler_params=_params(("parallel", "arbitrary"), vmem),
        name="normproj_gate" if with_gate else "normproj",
    )(*args)
    return out if with_gate else out[0]


def _outproj_body(*refs, n_parts):
    a_refs = refs[:n_parts]
    w_refs = refs[n_parts:2 * n_parts]
    x_ref, g_ref, o_ref = refs[2 * n_parts:]
    y = _dot(a_refs[0][...], w_refs[0][...])
    for a_ref, w_ref in zip(a_refs[1:], w_refs[1:]):
        y += _dot(a_ref[...], w_ref[...])
    o_ref[...] = x_ref[...] + _rmsnorm(y, g_ref[...])


def _outproj(x, g, parts, *, bm=512):
    m, d = x.shape
    bm = min(bm, m)
    acts = [a for a, _ in parts]
    ws = [w for _, w in parts]
    in_specs = [pl.BlockSpec((bm, a.shape[1]), lambda i: (i, 0)) for a in acts]
    in_specs += [pl.BlockSpec(w.shape, lambda i: (0, 0)) for w in ws]
    in_specs += [pl.BlockSpec((bm, d), lambda i: (i, 0)), pl.BlockSpec((1, d), lambda i: (0, 0))]
    k_total = sum(w.shape[0] for w in ws)
    vmem = 2 * k_total * d * 2 + 2 * bm * k_total * 2 + 5 * bm * d * 4 + (4 << 20)
    return pl.pallas_call(
        functools.partial(_outproj_body, n_parts=len(parts)),
        out_shape=jax.ShapeDtypeStruct((m, d), F32),
        grid=(m // bm,),
        in_specs=in_specs,
        out_specs=pl.BlockSpec((bm, d), lambda i: (i, 0)),
        compiler_params=_params(("parallel",), vmem),
        name="outproj",
    )(*acts, *ws, x, g)


def _conv_silu(raw, w, b):
    rows = lax.broadcasted_iota(jnp.int32, raw.shape, 0)
    acc = raw * w[CONV_K - 1:CONV_K, :] + b
    for shift in range(1, CONV_K):
        shifted = jnp.where(rows >= shift, pltpu.roll(raw, shift, 0), 0.0)
        acc += shifted * w[CONV_K - 1 - shift:CONV_K - shift, :]
    return acc * jax.nn.sigmoid(acc)


def _mlstm_body(q_ref, k_ref, v_ref, og_ref, gates_ref, gbias_ref, cwq_ref, cbq_ref, cwk_ref, cbk_ref,
                norm_ref, o_ref, qs_ref, ks_ref, gs_ref, *, chunk):
    s = q_ref.shape[0]
    head = pl.program_id(1)
    qs_ref[...] = _conv_silu(q_ref[...].astype(F32), cwq_ref[...], cbq_ref[...]).astype(BF16)
    ks_ref[...] = _conv_silu(k_ref[...].astype(F32), cwk_ref[...], cbk_ref[...]) * (M_DQK ** -0.5)
    gs_ref[...] = gates_ref[...] + gbias_ref[...]

    r_i = lax.broadcasted_iota(jnp.int32, (chunk, chunk), 0)
    c_i = lax.broadcasted_iota(jnp.int32, (chunk, chunk), 1)
    causal = r_i >= c_i
    tri = causal.astype(F32)
    lane_is_head = lax.broadcasted_iota(jnp.int32, (chunk, LANES), 1) == head
    sub_is_head = lax.broadcasted_iota(jnp.int32, (LANES, chunk), 0) == head
    gain = norm_ref[...]

    def step(c, carry):
        cmat, nvec, mrun = carry
        rows = pl.ds(pl.multiple_of(c * chunk, chunk), chunk)
        gc = gs_ref[rows, :]
        bcum = jnp.dot(tri, _log_sigmoid(gc), preferred_element_type=F32, precision=lax.Precision.HIGHEST)
        bsh = pltpu.roll(bcum, LANES - M_HEADS, 1)
        b_col = jnp.sum(jnp.where(lane_is_head, bsh, 0.0), axis=-1, keepdims=True)
        i_col = jnp.sum(jnp.where(lane_is_head, gc, 0.0), axis=-1, keepdims=True)
        r_row = jnp.sum(jnp.where(sub_is_head, (gc - bsh).T, 0.0), axis=0, keepdims=True)
        g_end = b_col[chunk - 1:chunk, :]

        qb = qs_ref[rows, :]
        kf = ks_ref[rows, :]
        vb = v_ref[rows, :]
        dmat = jnp.where(causal, b_col + r_row, -jnp.inf)
        inter = b_col + mrun
        m_t = jnp.maximum(inter, jnp.max(dmat, axis=-1, keepdims=True))
        w_inter = jnp.exp(inter - m_t)
        p = jnp.exp(dmat - m_t) * _dot_nt(qb, kf.astype(BF16))
        num = w_inter * _dot(qb, cmat.astype(BF16)) + _dot(p.astype(BF16), vb)
        den = (w_inter * jnp.sum(qb.astype(F32) * nvec, axis=-1, keepdims=True)
               + jnp.sum(p, axis=-1, keepdims=True))
        hc = num / jnp.maximum(jnp.abs(den), jnp.exp(-m_t))

        s_w = g_end - b_col + i_col
        m_new = jnp.maximum(g_end + mrun, jnp.max(s_w, axis=0, keepdims=True))
        decay = jnp.exp(g_end + mrun - m_new)
        kw = kf * jnp.exp(s_w - m_new)
        cmat = decay * cmat + _dot_tn(kw.astype(BF16), vb)
        nvec = decay * nvec + jnp.sum(kw, axis=0, keepdims=True)

        y = _rmsnorm(hc, gain) * jax.nn.sigmoid(og_ref[rows, :].astype(F32))
        o_ref[rows, :] = y.astype(o_ref.dtype)
        return cmat, nvec, m_new

    init = (jnp.zeros((M_DQK, M_DV), F32), jnp.zeros((1, M_DQK), F32), jnp.zeros((1, 1), F32))
    lax.fori_loop(0, s // chunk, step, init)


def _mlstm(z, gates, gate_bias, conv_w, conv_b, norm, *, batch, seq):
    chunk = min(MLSTM_CHUNK, seq)
    qk_blocks = M_HEADS
    v_block0 = 2 * M_HEADS * M_DQK // M_DV
    og_block0 = v_block0 + M_HEADS
    vmem = seq * (2 * 2 * M_DQK * 2 + 3 * 2 * M_DV * 2 + 3 * LANES * 4 + M_DQK * 6) + (24 << 20)
    return pl.pallas_call(
        functools.partial(_mlstm_body, chunk=chunk),
        out_shape=jax.ShapeDtypeStruct((batch * seq, M_HEADS * M_DV), BF16),
        grid=(batch, M_HEADS),
        in_specs=[
            pl.BlockSpec((seq, M_DQK), lambda b, h: (b, h)),
            pl.BlockSpec((seq, M_DQK), lambda b, h: (b, qk_blocks + h)),
            pl.BlockSpec((seq, M_DV), lambda b, h: (b, v_block0 + h)),
            pl.BlockSpec((seq, M_DV), lambda b, h: (b, og_block0 + h)),
            pl.BlockSpec((seq, LANES), lambda b, h: (b, 0)),
            pl.BlockSpec((1, LANES), lambda b, h: (0, 0)),
            pl.BlockSpec((CONV_K, M_DQK), lambda b, h: (0, h)),
            pl.BlockSpec((1, M_DQK), lambda b, h: (0, h)),
            pl.BlockSpec((CONV_K, M_DQK), lambda b, h: (0, qk_blocks + h)),
            pl.BlockSpec((1, M_DQK), lambda b, h: (0, qk_blocks + h)),
            pl.BlockSpec((1, M_DV), lambda b, h: (0, h)),
        ],
        out_specs=pl.BlockSpec((seq, M_DV), lambda b, h: (b, h)),
        scratch_shapes=[pltpu.VMEM((seq, M_DQK), BF16), pltpu.VMEM((seq, M_DQK), F32),
                        pltpu.VMEM((seq, LANES), F32)],
        compiler_params=_params(("parallel", "parallel"), vmem),
        name="mlstm",
    )(z, z, z, z, gates, gate_bias, conv_w, conv_b, conv_w, conv_b, norm)


def _t5_bucket_np(n):
    nf = np.maximum(n, 1).astype(np.float64)
    val = np.log(nf / MAX_EXACT) / math.log(MAX_DISTANCE / MAX_EXACT) * (NUM_BUCKETS - MAX_EXACT)
    frac = val[(n > MAX_EXACT) & (n < MAX_DISTANCE)]
    assert np.all(np.abs(frac - np.round(frac)) > 1e-3)
    large = np.minimum(MAX_EXACT + val.astype(np.int64), NUM_BUCKETS - 1)
    return np.where(n < MAX_EXACT, n, large)


def _bucket_tile(blk):
    rel = np.arange(blk)[None, :] + blk - np.arange(2 * blk)[:, None]
    return np.where(rel >= 0, _t5_bucket_np(np.maximum(rel, 0)), -1).astype(np.int32)


def _bias_body(rb_ref, bucket_ref, t_ref):
    h = pl.program_id(0)
    bucket = bucket_ref[...]
    acc = jnp.zeros(bucket.shape, F32)
    for b in range(NUM_BUCKETS):
        acc = jnp.where(bucket == b, rb_ref[h * NUM_BUCKETS + b], acc)
    far = rb_ref[h * NUM_BUCKETS + NUM_BUCKETS - 1]
    t_ref[0] = jnp.where(bucket < 0, -jnp.inf, (acc - far) * LOG2E)


def _bias_tiles(rel_bias, blk):
    assert blk >= MAX_DISTANCE
    bucket = jnp.asarray(_bucket_tile(blk))
    return pl.pallas_call(
        _bias_body,
        out_shape=jax.ShapeDtypeStruct((DA_HEADS, 2 * blk, blk), F32),
        grid=(DA_HEADS,),
        in_specs=[pl.BlockSpec(memory_space=pltpu.SMEM),
                  pl.BlockSpec((2 * blk, blk), lambda h: (0, 0))],
        out_specs=pl.BlockSpec((1, 2 * blk, blk), lambda h: (h, 0, 0)),
        compiler_params=_params(("arbitrary",), 16 << 20),
        name="t5_bias_tiles",
    )(rel_bias.T.reshape(-1), bucket)


VT_ROWS = DA_DV + 16


def _diffattn_body(lam_ref, q_ref, k_ref, v_ref, t_ref, g_ref, o_ref, vt_ref, *, blk, lam_init):
    nb = q_ref.shape[0] // blk
    ones_rows = (lax.broadcasted_iota(jnp.int32, (VT_ROWS - DA_DV, blk), 0) == 0).astype(F32)
    for c in range(nb):
        vt = v_ref[c * blk:(c + 1) * blk, :].astype(F32).T
        vt_ref[c] = jnp.concatenate([vt, ones_rows], axis=0).astype(BF16)

    lp = lam_ref[...]
    lam = (jnp.exp(jnp.sum(lp[0:1] * lp[1:2], axis=-1, keepdims=True))
           - jnp.exp(jnp.sum(lp[2:3] * lp[3:4], axis=-1, keepdims=True)) + lam_init)
    sub = lax.broadcasted_iota(jnp.int32, (LANES, blk), 0)
    gain = g_ref[...]

    def query_maps(qi):
        qt = q_ref[qi * blk:(qi + 1) * blk, :].astype(F32).T
        return (jnp.where(sub < DA_D, qt, 0.0).astype(BF16), jnp.where(sub >= DA_D, qt, 0.0).astype(BF16))

    def scores(qts, qi, kc):
        kb = k_ref[kc * blk:(kc + 1) * blk, :]
        s = [_dot(kb, qt_m) for qt_m in qts]
        if kc >= qi - 1:
            bias = t_ref[0, blk:, :] if kc == qi else t_ref[0, :blk, :]
            s = [s_m + bias for s_m in s]
        return s

    units = [(qi, kc) for qi in range(nb) for kc in range(qi, -1, -1)]
    qts = query_maps(0)
    s_next = scores(qts, *units[0])
    state = [None, None]
    for u, (qi, kc) in enumerate(units):
        s_cur = s_next
        if u + 1 < len(units):
            if units[u + 1][0] != qi:
                qts = query_maps(units[u + 1][0])
            s_next = scores(qts, *units[u + 1])
        for a in range(2):
            s = s_cur[a]
            blk_max = jnp.max(s, axis=0, keepdims=True)
            if state[a] is None:
                m_new = blk_max
                acc = _dot(vt_ref[kc], jnp.exp2(s - m_new).astype(BF16))
            else:
                m, acc = state[a]
                m_new = jnp.maximum(m, blk_max)
                acc = acc * jnp.exp2(m - m_new) + _dot(vt_ref[kc], jnp.exp2(s - m_new).astype(BF16))
            state[a] = (m_new, acc)
        if kc == 0:
            outs = [acc[:DA_DV] / acc[DA_DV:DA_DV + 1] for _, acc in state]
            o = (outs[0] - lam * outs[1]).T
            o_ref[qi * blk:(qi + 1) * blk, :] = (_rmsnorm(o, gain) * (1.0 - lam_init)).astype(o_ref.dtype)
            state = [None, None]


def _diffattn(z, tiles, lam_p, subln, *, batch, seq, lam_init, col0):
    blk = tiles.shape[2]
    q0 = col0 // LANES
    k0 = q0 + DA_HEADS
    v0 = k0 + DA_HEADS
    vmem = 8 * seq * LANES * 2 + seq * VT_ROWS * 2 + 4 * blk * blk * 4 + 24 * blk * blk * 4 + (8 << 20)
    return pl.pallas_call(
        functools.partial(_diffattn_body, blk=blk, lam_init=lam_init),
        out_shape=jax.ShapeDtypeStruct((batch * seq, DA_HEADS * DA_DV), BF16),
        grid=(batch, DA_HEADS),
        in_specs=[
            pl.BlockSpec((4, DA_D), lambda b, h: (0, 0)),
            pl.BlockSpec((seq, LANES), lambda b, h: (b, q0 + h)),
            pl.BlockSpec((seq, LANES), lambda b, h: (b, k0 + h)),
            pl.BlockSpec((seq, LANES), lambda b, h: (b, v0 + h)),
            pl.BlockSpec((1, 2 * blk, blk), lambda b, h: (h, 0, 0)),
            pl.BlockSpec((1, DA_DV), lambda b, h: (0, 0)),
        ],
        out_specs=pl.BlockSpec((seq, DA_DV), lambda b, h: (b, h)),
        scratch_shapes=[pltpu.VMEM((seq // blk, VT_ROWS, blk), BF16)],
        compiler_params=_params(("parallel", "parallel"), vmem),
        name="diffattn",
    )(lam_p, z, z, z, tiles, subln)


def _xattn_body(q_ref, k_ref, v_ref, o_ref):
    dh = q_ref.shape[1]
    s = _dot_nt(q_ref[...], k_ref[...]) * (dh ** -0.5)
    e = jnp.exp(s - jnp.max(s, axis=-1, keepdims=True))
    a = e / jnp.sum(e, axis=-1, keepdims=True)
    o_ref[...] = _dot(a.astype(BF16), v_ref[...]).astype(o_ref.dtype)


def _xattn(q, kv, *, batch, seq, mem_len, bq=512):
    d = q.shape[1]
    dh = d // X_HEADS
    bq = min(bq, seq)
    nq = seq // bq
    vmem = 4 * bq * dh * 2 + 4 * mem_len * dh * 2 + 4 * bq * mem_len * 4 + (8 << 20)
    return pl.pallas_call(
        _xattn_body,
        out_shape=jax.ShapeDtypeStruct((batch * seq, d), BF16),
        grid=(batch, X_HEADS, nq),
        in_specs=[
            pl.BlockSpec((bq, dh), lambda b, h, i: (b * nq + i, h)),
            pl.BlockSpec((mem_len, dh), lambda b, h, i: (b, h)),
            pl.BlockSpec((mem_len, dh), lambda b, h, i: (b, X_HEADS + h)),
        ],
        out_specs=pl.BlockSpec((bq, dh), lambda b, h, i: (b * nq + i, h)),
        compiler_params=_params(("parallel", "parallel", "parallel"), vmem),
        name="xattn",
    )(q, kv, kv)


def kernel(x, mem, rel_bias, ffn1_norm_pre, ffn1_norm_post, ffn1_w_gate, ffn1_w_up, ffn1_w_down, mix_norm_pre, mix_norm_post, w_in, conv_w, conv_b, b_igate, b_fgate, mlstm_norm, diff_lambda, diff_subln, w_out, xattn_norm_pre, xattn_norm_post, mem_norm, xattn_wq, xattn_wk, xattn_wv, xattn_wo, ffn2_norm_pre, ffn2_norm_post, ffn2_w_gate, ffn2_w_up, ffn2_w_down):
    batch, seq, d = x.shape
    mem_len = mem.shape[1]
    depth = w_in.shape[0]
    m_qk = 2 * M_HEADS * M_DQK
    m_v = M_HEADS * M_DV
    n_gate = 2 * M_HEADS
    da_qw = DA_HEADS * 2 * DA_D
    gate0 = m_qk + 2 * m_v
    xf = x.reshape(batch * seq, d)
    memf = mem.reshape(batch * mem_len, d)
    row = lambda v: v.reshape(1, -1).astype(F32)
    bf = lambda w: w.astype(BF16)

    tiles = _bias_tiles(rel_bias, min(DA_BLOCK, seq))

    for l in range(depth):
        lam_init = 0.8 - 0.6 * math.exp(-0.3 * l)
        xf = _ffn(xf, row(ffn1_norm_pre[l]), row(ffn1_norm_post[l]),
                  bf(ffn1_w_gate[l]), bf(ffn1_w_up[l]), bf(ffn1_w_down[l]))

        w_l = w_in[l]
        da_q = w_l[:, gate0 + n_gate:gate0 + n_gate + da_qw] * (DA_D ** -0.5 * LOG2E)
        w_main = bf(jnp.concatenate([w_l[:, :gate0], da_q, w_l[:, gate0 + n_gate + da_qw:]], axis=1))
        w_gate = bf(jnp.pad(w_l[:, gate0:gate0 + n_gate], ((0, 0), (0, LANES - n_gate))))
        z, gates = _normproj(xf, row(mix_norm_pre[l]), w_main, w_gate)
        gate_bias = jnp.pad(jnp.concatenate([b_igate[l], b_fgate[l]]), (0, LANES - n_gate)).reshape(1, LANES)
        y_m = _mlstm(z, gates, gate_bias.astype(F32), conv_w[l].astype(F32), row(conv_b[l]),
                     row(mlstm_norm[l]), batch=batch, seq=seq)
        y_d = _diffattn(z, tiles, diff_lambda[l].astype(F32), row(diff_subln[l]),
                        batch=batch, seq=seq, lam_init=lam_init, col0=gate0)
        w_o = bf(w_out[l])
        xf = _outproj(xf, row(mix_norm_post[l]), [(y_m, w_o[:m_v]), (y_d, w_o[m_v:])])

        q = _normproj(xf, row(xattn_norm_pre[l]), bf(xattn_wq[l]))
        kv = _normproj(memf, row(mem_norm[l]), bf(jnp.concatenate([xattn_wk[l], xattn_wv[l]], axis=1)))
        c = _xattn(q, kv, batch=batch, seq=seq, mem_len=mem_len)
        xf = _outproj(xf, row(xattn_norm_post[l]), [(c, bf(xattn_wo[l]))])

        xf = _ffn(xf, row(ffn2_norm_pre[l]), row(ffn2_norm_post[l]),
                  bf(ffn2_w_gate[l]), bf(ffn2_w_up[l]), bf(ffn2_w_down[l]))
    return xf.reshape(batch, seq, d)
```

```python
import functools
import math

import numpy as np
import jax
import jax.numpy as jnp
from jax import lax
from jax.experimental import pallas as pl
from jax.experimental.pallas import tpu as pltpu

F32 = jnp.float32
BF16 = jnp.bfloat16
EPS = 1e-6
LOG2E = math.log2(math.e)

M_HEADS = 4
M_DQK = 128
M_DV = 256
CONV_K = 4
DA_HEADS = 8
DA_D = 64
DA_DV = 128
X_HEADS = 4
NUM_BUCKETS = 32
MAX_EXACT = NUM_BUCKETS // 2
MAX_DISTANCE = 128
DEPTH = 2

LANES = 128
V7X_VMEM_BYTES = 64 * 2 ** 20
MLSTM_CHUNK = 256
DA_BLOCK = 256


def _params(semantics, vmem_bytes):
    assert vmem_bytes <= V7X_VMEM_BYTES
    return pltpu.CompilerParams(dimension_semantics=semantics, vmem_limit_bytes=int(vmem_bytes))


def _rmsnorm(xf, g):
    return xf * lax.rsqrt(jnp.mean(xf * xf, axis=-1, keepdims=True) + EPS) * g


def _log_sigmoid(x):
    return jnp.minimum(x, 0.0) - jnp.log1p(jnp.exp(-jnp.abs(x)))


def _dot(a, b):
    return jnp.dot(a, b, preferred_element_type=F32)


def _dot_nt(a, b):
    return lax.dot_general(a, b, (((1,), (1,)), ((), ())), preferred_element_type=F32)


def _dot_tn(a, b):
    return lax.dot_general(a, b, (((0,), (0,)), ((), ())), preferred_element_type=F32)


def _ffn_body(x_ref, xlag_ref, gpre_ref, gpost_ref, wg_ref, wu_ref, wd_ref, o_ref, xn_ref, h_ref, acc_ref, *, nf):
    t = pl.program_id(0)
    j = t % nf
    slot = t % 2

    @pl.when(t == 0)
    def _():
        h_ref[1] = jnp.zeros(h_ref.shape[1:], h_ref.dtype)
        acc_ref[...] = jnp.zeros(acc_ref.shape, acc_ref.dtype)

    @pl.when(j == 0)
    def _():
        xn_ref[...] = _rmsnorm(x_ref[...], gpre_ref[...]).astype(BF16)

    xn = xn_ref[...]
    g = _dot(xn, wg_ref[0])
    u = _dot(xn, wu_ref[0])
    h_ref[slot] = (g * jax.nn.sigmoid(g) * u).astype(BF16)
    acc_ref[...] += _dot(h_ref[1 - slot], wd_ref[...])

    @pl.when((j == 0) & (t > 0))
    def _():
        o_ref[...] = xlag_ref[...] + 0.5 * _rmsnorm(acc_ref[...], gpost_ref[...])
        acc_ref[...] = jnp.zeros(acc_ref.shape, acc_ref.dtype)


def _col_blocks(w, bn):
    k, n = w.shape
    return w.reshape(k, n // bn, bn).transpose(1, 0, 2).astype(BF16)


def _ffn(x, g_pre, g_post, w_gate, w_up, w_down, *, bm=512, bf=512):
    m, d = x.shape
    f = w_gate.shape[1]
    bm, bf = min(bm, m), min(bf, f)
    nf = f // bf
    units = (m // bm) * nf
    w_gate, w_up, w_down = _col_blocks(w_gate, bf), _col_blocks(w_up, bf), w_down.astype(BF16)
    cur = lambda t: jnp.minimum(t, units - 1)
    prev = lambda t: jnp.maximum(t - 1, 0)
    vmem = 7 * bm * d * 4 + bm * d * 2 + 6 * d * bf * 2 + 6 * bm * bf * 4 + (4 << 20)
    return pl.pallas_call(
        functools.partial(_ffn_body, nf=nf),
        out_shape=jax.ShapeDtypeStruct((m, d), F32),
        grid=(units + 1,),
        in_specs=[
            pl.BlockSpec((bm, d), lambda t: (cur(t) // nf, 0)),
            pl.BlockSpec((bm, d), lambda t: (prev(t) // nf, 0)),
            pl.BlockSpec((1, d), lambda t: (0, 0)),
            pl.BlockSpec((1, d), lambda t: (0, 0)),
            pl.BlockSpec((1, d, bf), lambda t: (cur(t) % nf, 0, 0)),
            pl.BlockSpec((1, d, bf), lambda t: (cur(t) % nf, 0, 0)),
            pl.BlockSpec((bf, d), lambda t: (prev(t) % nf, 0)),
        ],
        out_specs=pl.BlockSpec((bm, d), lambda t: (prev(t) // nf, 0)),
        scratch_shapes=[pltpu.VMEM((bm, d), BF16), pltpu.VMEM((2, bm, bf), BF16), pltpu.VMEM((bm, d), F32)],
        compiler_params=_params(("arbitrary",), vmem),
        name="ffn",
    )(x, x, g_pre, g_post, w_gate, w_up, w_down)


def _normproj_body(x_ref, g_ref, w_ref, *rest, with_gate):
    if with_gate:
        wgate_ref, o_ref, gate_ref, xn_ref = rest
    else:
        o_ref, xn_ref = rest
    j = pl.program_id(1)

    @pl.when(j == 0)
    def _():
        xn = _rmsnorm(x_ref[...], g_ref[...]).astype(BF16)
        xn_ref[...] = xn
        if with_gate:
            gate_ref[...] = _dot(xn, wgate_ref[...])

    o_ref[...] = _dot(xn_ref[...], w_ref[...]).astype(o_ref.dtype)


def _normproj(x, g, w, w_gate=None, *, bm=1024, bn=512):
    m, d = x.shape
    n = w.shape[1]
    bm, bn = min(bm, m), min(bn, n)
    with_gate = w_gate is not None
    in_specs = [
        pl.BlockSpec((bm, d), lambda i, j: (i, 0)),
        pl.BlockSpec((1, d), lambda i, j: (0, 0)),
        pl.BlockSpec((d, bn), lambda i, j: (0, j)),
    ]
    out_shape = [jax.ShapeDtypeStruct((m, n), BF16)]
    out_specs = [pl.BlockSpec((bm, bn), lambda i, j: (i, j))]
    args = [x, g, w]
    if with_gate:
        in_specs.append(pl.BlockSpec((d, LANES), lambda i, j: (0, 0)))
        out_shape.append(jax.ShapeDtypeStruct((m, LANES), F32))
        out_specs.append(pl.BlockSpec((bm, LANES), lambda i, j: (i, 0)))
        args.append(w_gate)
    vmem = 2 * bm * d * 4 + bm * d * 2 + 2 * d * bn * 2 + 2 * bm * bn * 2 + bm * bn * 4 + (6 << 20)
    out = pl.pallas_call(
        functools.partial(_normproj_body, with_gate=with_gate),
        out_shape=out_shape,
        grid=(m // bm, n // bn),
        in_specs=in_specs,
        out_specs=out_specs,
        scratch_shapes=[pltpu.VMEM((bm, d), BF16)],
        compiler_params=_params(("parallel", "arbitrary"), vmem),
        name="normproj_gate" if with_gate else "normproj",
    )(*args)
    return out if with_gate else out[0]


def _outproj_body(*refs, n_parts):
    a_refs = refs[:n_parts]
    w_refs = refs[n_parts:2 * n_parts]
    x_ref, g_ref, o_ref = refs[2 * n_parts:]
    y = _dot(a_refs[0][...], w_refs[0][...])
    for a_ref, w_ref in zip(a_refs[1:], w_refs[1:]):
        y += _dot(a_ref[...], w_ref[...])
    o_ref[...] = x_ref[...] + _rmsnorm(y, g_ref[...])


def _outproj(x, g, parts, *, bm=512):
    m, d = x.shape
    bm = min(bm, m)
    acts = [a for a, _ in parts]
    ws = [w for _, w in parts]
    in_specs = [pl.BlockSpec((bm, a.shape[1]), lambda i: (i, 0)) for a in acts]
    in_specs += [pl.BlockSpec(w.shape, lambda i: (0, 0)) for w in ws]
    in_specs += [pl.BlockSpec((bm, d), lambda i: (i, 0)), pl.BlockSpec((1, d), lambda i: (0, 0))]
    k_total = sum(w.shape[0] for w in ws)
    vmem = 2 * k_total * d * 2 + 2 * bm * k_total * 2 + 5 * bm * d * 4 + (4 << 20)
    return pl.pallas_call(
        functools.partial(_outproj_body, n_parts=len(parts)),
        out_shape=jax.ShapeDtypeStruct((m, d), F32),
        grid=(m // bm,),
        in_specs=in_specs,
        out_specs=pl.BlockSpec((bm, d), lambda i: (i, 0)),
        compiler_params=_params(("parallel",), vmem),
        name="outproj",
    )(*acts, *ws, x, g)


def _conv_silu(raw, w, b):
    rows = lax.broadcasted_iota(jnp.int32, raw.shape, 0)
    acc = raw * w[CONV_K - 1:CONV_K, :] + b
    for shift in range(1, CONV_K):
        shifted = jnp.where(rows >= shift, pltpu.roll(raw, shift, 0), 0.0)
        acc += shifted * w[CONV_K - 1 - shift:CONV_K - shift, :]
    return acc * jax.nn.sigmoid(acc)


def _mlstm_body(q_ref, k_ref, v_ref, og_ref, gates_ref, gbias_ref, cwq_ref, cbq_ref, cwk_ref, cbk_ref,
                norm_ref, o_ref, qs_ref, ks_ref, gs_ref, *, chunk):
    s = q_ref.shape[0]
    head = pl.program_id(1)
    qs_ref[...] = _conv_silu(q_ref[...].astype(F32), cwq_ref[...], cbq_ref[...]).astype(BF16)
    ks_ref[...] = _conv_silu(k_ref[...].astype(F32), cwk_ref[...], cbk_ref[...]) * (M_DQK ** -0.5)
    gs_ref[...] = gates_ref[...] + gbias_ref[...]

    r_i = lax.broadcasted_iota(jnp.int32, (chunk, chunk), 0)
    c_i = lax.broadcasted_iota(jnp.int32, (chunk, chunk), 1)
    causal = r_i >= c_i
    tri = causal.astype(F32)
    lane_is_head = lax.broadcasted_iota(jnp.int32, (chunk, LANES), 1) == head
    sub_is_head = lax.broadcasted_iota(jnp.int32, (LANES, chunk), 0) == head
    gain = norm_ref[...]

    def step(c, carry):
        cmat, nvec, mrun = carry
        rows = pl.ds(pl.multiple_of(c * chunk, chunk), chunk)
        gc = gs_ref[rows, :]
        bcum = jnp.dot(tri, _log_sigmoid(gc), preferred_element_type=F32, precision=lax.Precision.HIGHEST)
        bsh = pltpu.roll(bcum, LANES - M_HEADS, 1)
        b_col = jnp.sum(jnp.where(lane_is_head, bsh, 0.0), axis=-1, keepdims=True)
        i_col = jnp.sum(jnp.where(lane_is_head, gc, 0.0), axis=-1, keepdims=True)
        r_row = jnp.sum(jnp.where(sub_is_head, (gc - bsh).T, 0.0), axis=0, keepdims=True)
        g_end = b_col[chunk - 1:chunk, :]

        qb = qs_ref[rows, :]
        kf = ks_ref[rows, :]
        vb = v_ref[rows, :]
        dmat = jnp.where(causal, b_col + r_row, -jnp.inf)
        inter = b_col + mrun
        m_t = jnp.maximum(inter, jnp.max(dmat, axis=-1, keepdims=True))
        w_inter = jnp.exp(inter - m_t)
        p = jnp.exp(dmat - m_t) * _dot_nt(qb, kf.astype(BF16))
        num = w_inter * _dot(qb, cmat.astype(BF16)) + _dot(p.astype(BF16), vb)
        den = (w_inter * jnp.sum(qb.astype(F32) * nvec, axis=-1, keepdims=True)
               + jnp.sum(p, axis=-1, keepdims=True))
        hc = num / jnp.maximum(jnp.abs(den), jnp.exp(-m_t))

        s_w = g_end - b_col + i_col
        m_new = jnp.maximum(g_end + mrun, jnp.max(s_w, axis=0, keepdims=True))
        decay = jnp.exp(g_end + mrun - m_new)
        kw = kf * jnp.exp(s_w - m_new)
        cmat = decay * cmat + _dot_tn(kw.astype(BF16), vb)
        nvec = decay * nvec + jnp.sum(kw, axis=0, keepdims=True)

        y = _rmsnorm(hc, gain) * jax.nn.sigmoid(og_ref[rows, :].astype(F32))
        o_ref[rows, :] = y.astype(o_ref.dtype)
        return cmat, nvec, m_new

    init = (jnp.zeros((M_DQK, M_DV), F32), jnp.zeros((1, M_DQK), F32), jnp.zeros((1, 1), F32))
    lax.fori_loop(0, s // chunk, step, init)


def _mlstm(z, gates, gate_bias, conv_w, conv_b, norm, *, batch, seq):
    chunk = min(MLSTM_CHUNK, seq)
    qk_blocks = M_HEADS
    v_block0 = 2 * M_HEADS * M_DQK // M_DV
    og_block0 = v_block0 + M_HEADS
    vmem = seq * (2 * 2 * M_DQK * 2 + 3 * 2 * M_DV * 2 + 3 * LANES * 4 + M_DQK * 6) + (24 << 20)
    return pl.pallas_call(
        functools.partial(_mlstm_body, chunk=chunk),
        out_shape=jax.ShapeDtypeStruct((batch * seq, M_HEADS * M_DV), BF16),
        grid=(batch, M_HEADS),
        in_specs=[
            pl.BlockSpec((seq, M_DQK), lambda b, h: (b, h)),
            pl.BlockSpec((seq, M_DQK), lambda b, h: (b, qk_blocks + h)),
            pl.BlockSpec((seq, M_DV), lambda b, h: (b, v_block0 + h)),
            pl.BlockSpec((seq, M_DV), lambda b, h: (b, og_block0 + h)),
            pl.BlockSpec((seq, LANES), lambda b, h: (b, 0)),
            pl.BlockSpec((1, LANES), lambda b, h: (0, 0)),
            pl.BlockSpec((CONV_K, M_DQK), lambda b, h: (0, h)),
            pl.BlockSpec((1, M_DQK), lambda b, h: (0, h)),
            pl.BlockSpec((CONV_K, M_DQK), lambda b, h: (0, qk_blocks + h)),
            pl.BlockSpec((1, M_DQK), lambda b, h: (0, qk_blocks + h)),
            pl.BlockSpec((1, M_DV), lambda b, h: (0, h)),
        ],
        out_specs=pl.BlockSpec((seq, M_DV), lambda b, h: (b, h)),
        scratch_shapes=[pltpu.VMEM((seq, M_DQK), BF16), pltpu.VMEM((seq, M_DQK), F32),
                        pltpu.VMEM((seq, LANES), F32)],
        compiler_params=_params(("parallel", "parallel"), vmem),
        name="mlstm",
    )(z, z, z, z, gates, gate_bias, conv_w, conv_b, conv_w, conv_b, norm)


def _t5_bucket_np(n):
    nf = np.maximum(n, 1).astype(np.float64)
    val = np.log(nf / MAX_EXACT) / math.log(MAX_DISTANCE / MAX_EXACT) * (NUM_BUCKETS - MAX_EXACT)
    frac = val[(n > MAX_EXACT) & (n < MAX_DISTANCE)]
    assert np.all(np.abs(frac - np.round(frac)) > 1e-3)
    large = np.minimum(MAX_EXACT + val.astype(np.int64), NUM_BUCKETS - 1)
    return np.where(n < MAX_EXACT, n, large)


def _bucket_tile(blk):
    rel = np.arange(blk)[None, :] + blk - np.arange(2 * blk)[:, None]
    return np.where(rel >= 0, _t5_bucket_np(np.maximum(rel, 0)), -1).astype(np.int32)


def _bias_body(rb_ref, bucket_ref, t_ref):
    h = pl.program_id(0)
    bucket = bucket_ref[...]
    acc = jnp.zeros(bucket.shape, F32)
    for b in range(NUM_BUCKETS):
        acc = jnp.where(bucket == b, rb_ref[h * NUM_BUCKETS + b], acc)
    far = rb_ref[h * NUM_BUCKETS + NUM_BUCKETS - 1]
    t_ref[0] = jnp.where(bucket < 0, -jnp.inf, (acc - far) * LOG2E)


def _bias_tiles(rel_bias, blk):
    assert blk >= MAX_DISTANCE
    bucket = jnp.asarray(_bucket_tile(blk))
    return pl.pallas_call(
        _bias_body,
        out_shape=jax.ShapeDtypeStruct((DA_HEADS, 2 * blk, blk), F32),
        grid=(DA_HEADS,),
        in_specs=[pl.BlockSpec(memory_space=pltpu.SMEM),
                  pl.BlockSpec((2 * blk, blk), lambda h: (0, 0))],
        out_specs=pl.BlockSpec((1, 2 * blk, blk), lambda h: (h, 0, 0)),
        compiler_params=_params(("arbitrary",), 16 << 20),
        name="t5_bias_tiles",
    )(rel_bias.T.reshape(-1), bucket)


VT_ROWS = DA_DV + 16


def _diffattn_body(lam_ref, q_ref, k_ref, v_ref, t_ref, g_ref, o_ref, vt_ref, *, blk, lam_init):
    nb = q_ref.shape[0] // blk
    ones_rows = (lax.broadcasted_iota(jnp.int32, (VT_ROWS - DA_DV, blk), 0) == 0).astype(F32)
    for c in range(nb):
        vt = v_ref[c * blk:(c + 1) * blk, :].astype(F32).T
        vt_ref[c] = jnp.concatenate([vt, ones_rows], axis=0).astype(BF16)

    lp = lam_ref[...]
    lam = (jnp.exp(jnp.sum(lp[0:1] * lp[1:2], axis=-1, keepdims=True))
           - jnp.exp(jnp.sum(lp[2:3] * lp[3:4], axis=-1, keepdims=True)) + lam_init)
    sub = lax.broadcasted_iota(jnp.int32, (LANES, blk), 0)
    gain = g_ref[...]

    def query_maps(qi):
        qt = q_ref[qi * blk:(qi + 1) * blk, :].astype(F32).T
        return (jnp.where(sub < DA_D, qt, 0.0).astype(BF16), jnp.where(sub >= DA_D, qt, 0.0).astype(BF16))

    def scores(qts, qi, kc):
        kb = k_ref[kc * blk:(kc + 1) * blk, :]
        s = [_dot(kb, qt_m) for qt_m in qts]
        if kc >= qi - 1:
            bias = t_ref[0, blk:, :] if kc == qi else t_ref[0, :blk, :]
            s = [s_m + bias for s_m in s]
        return s

    units = [(qi, kc) for qi in range(nb) for kc in range(qi, -1, -1)]
    qts = query_maps(0)
    s_next = scores(qts, *units[0])
    state = [None, None]
    for u, (qi, kc) in enumerate(units):
        s_cur = s_next
        if u + 1 < len(units):
            if units[u + 1][0] != qi:
                qts = query_maps(units[u + 1][0])
            s_next = scores(qts, *units[u + 1])
        for a in range(2):
            s = s_cur[a]
            blk_max = jnp.max(s, axis=0, keepdims=True)
            if state[a] is None:
                m_new = blk_max
                acc = _dot(vt_ref[kc], jnp.exp2(s - m_new).astype(BF16))
            else:
                m, acc = state[a]
                m_new = jnp.maximum(m, blk_max)
                acc = acc * jnp.exp2(m - m_new) + _dot(vt_ref[kc], jnp.exp2(s - m_new).astype(BF16))
            state[a] = (m_new, acc)
        if kc == 0:
            outs = [acc[:DA_DV] / acc[DA_DV:DA_DV + 1] for _, acc in state]
            o = (outs[0] - lam * outs[1]).T
            o_ref[qi * blk:(qi + 1) * blk, :] = (_rmsnorm(o, gain) * (1.0 - lam_init)).astype(o_ref.dtype)
            state = [None, None]


def _diffattn(z, tiles, lam_p, subln, *, batch, seq, lam_init, col0):
    blk = tiles.shape[2]
    q0 = col0 // LANES
    k0 = q0 + DA_HEADS
    v0 = k0 + DA_HEADS
    vmem = 8 * seq * LANES * 2 + seq * VT_ROWS * 2 + 4 * blk * blk * 4 + 24 * blk * blk * 4 + (8 << 20)
    return pl.pallas_call(
        functools.partial(_diffattn_body, blk=blk, lam_init=lam_init),
        out_shape=jax.ShapeDtypeStruct((batch * seq, DA_HEADS * DA_DV), BF16),
        grid=(batch, DA_HEADS),
        in_specs=[
            pl.BlockSpec((4, DA_D), lambda b, h: (0, 0)),
            pl.BlockSpec((seq, LANES), lambda b, h: (b, q0 + h)),
            pl.BlockSpec((seq, LANES), lambda b, h: (b, k0 + h)),
            pl.BlockSpec((seq, LANES), lambda b, h: (b, v0 + h)),
            pl.BlockSpec((1, 2 * blk, blk), lambda b, h: (h, 0, 0)),
            pl.BlockSpec((1, DA_DV), lambda b, h: (0, 0)),
        ],
        out_specs=pl.BlockSpec((seq, DA_DV), lambda b, h: (b, h)),
        scratch_shapes=[pltpu.VMEM((seq // blk, VT_ROWS, blk), BF16)],
        compiler_params=_params(("parallel", "parallel"), vmem),
        name="diffattn",
    )(lam_p, z, z, z, tiles, subln)


def _xattn_body(q_ref, k_ref, v_ref, o_ref):
    dh = q_ref.shape[1]
    s = _dot_nt(q_ref[...], k_ref[...]) * (dh ** -0.5)
    e = jnp.exp(s - jnp.max(s, axis=-1, keepdims=True))
    a = e / jnp.sum(e, axis=-1, keepdims=True)
    o_ref[...] = _dot(a.astype(BF16), v_ref[...]).astype(o_ref.dtype)


def _xattn(q, kv, *, batch, seq, mem_len, bq=512):
    d = q.shape[1]
    dh = d // X_HEADS
    bq = min(bq, seq)
    nq = seq // bq
    vmem = 4 * bq * dh * 2 + 4 * mem_len * dh * 2 + 4 * bq * mem_len * 4 + (8 << 20)
    return pl.pallas_call(
        _xattn_body,
        out_shape=jax.ShapeDtypeStruct((batch * seq, d), BF16),
        grid=(batch, X_HEADS, nq),
        in_specs=[
            pl.BlockSpec((bq, dh), lambda b, h, i: (b * nq + i, h)),
            pl.BlockSpec((mem_len, dh), lambda b, h, i: (b, h)),
            pl.BlockSpec((mem_len, dh), lambda b, h, i: (b, X_HEADS + h)),
        ],
        out_specs=pl.BlockSpec((bq, dh), lambda b, h, i: (b * nq + i, h)),
        compiler_params=_params(("parallel", "parallel", "parallel"), vmem),
        name="xattn",
    )(q, kv, kv)


def kernel(x, mem, rel_bias, ffn1_norm_pre, ffn1_norm_post, ffn1_w_gate, ffn1_w_up, ffn1_w_down, mix_norm_pre, mix_norm_post, w_in, conv_w, conv_b, b_igate, b_fgate, mlstm_norm, diff_lambda, diff_subln, w_out, xattn_norm_pre, xattn_norm_post, mem_norm, xattn_wq, xattn_wk, xattn_wv, xattn_wo, ffn2_norm_pre, ffn2_norm_post, ffn2_w_gate, ffn2_w_up, ffn2_w_down):
    batch, seq, d = x.shape
    mem_len = mem.shape[1]
    depth = w_in.shape[0]
    m_qk = 2 * M_HEADS * M_DQK
    m_v = M_HEADS * M_DV
    n_gate = 2 * M_HEADS
    da_qw = DA_HEADS * 2 * DA_D
    gate0 = m_qk + 2 * m_v
    xf = x.reshape(batch * seq, d)
    memf = mem.reshape(batch * mem_len, d)
    row = lambda v: v.reshape(1, -1).astype(F32)
    bf = lambda w: w.astype(BF16)

    tiles = _bias_tiles(rel_bias, min(DA_BLOCK, seq))

    for l in range(depth):
        lam_init = 0.8 - 0.6 * math.exp(-0.3 * l)
        xf = _ffn(xf, row(ffn1_norm_pre[l]), row(ffn1_norm_post[l]),
                  ffn1_w_gate[l], ffn1_w_up[l], ffn1_w_down[l])

        w_l = w_in[l]
        da_q = w_l[:, gate0 + n_gate:gate0 + n_gate + da_qw] * (DA_D ** -0.5 * LOG2E)
        w_main = bf(jnp.concatenate([w_l[:, :gate0], da_q, w_l[:, gate0 + n_gate + da_qw:]], axis=1))
        w_gate = bf(jnp.pad(w_l[:, gate0:gate0 + n_gate], ((0, 0), (0, LANES - n_gate))))
        z, gates = _normproj(xf, row(mix_norm_pre[l]), w_main, w_gate)
        gate_bias = jnp.pad(jnp.concatenate([b_igate[l], b_fgate[l]]), (0, LANES - n_gate)).reshape(1, LANES)
        y_m = _mlstm(z, gates, gate_bias.astype(F32), conv_w[l].astype(F32), row(conv_b[l]),
                     row(mlstm_norm[l]), batch=batch, seq=seq)
        y_d = _diffattn(z, tiles, diff_lambda[l].astype(F32), row(diff_subln[l]),
                        batch=batch, seq=seq, lam_init=lam_init, col0=gate0)
        w_o = bf(w_out[l])
        xf = _outproj(xf, row(mix_norm_post[l]), [(y_m, w_o[:m_v]), (y_d, w_o[m_v:])])

        q = _normproj(xf, row(xattn_norm_pre[l]), bf(xattn_wq[l]))
        kv = _normproj(memf, row(mem_norm[l]), bf(jnp.concatenate([xattn_wk[l], xattn_wv[l]], axis=1)))
        c = _xattn(q, kv, batch=batch, seq=seq, mem_len=mem_len)
        xf = _outproj(xf, row(xattn_norm_post[l]), [(c, bf(xattn_wo[l]))])

        xf = _ffn(xf, row(ffn2_norm_pre[l]), row(ffn2_norm_post[l]),
                  ffn2_w_gate[l], ffn2_w_up[l], ffn2_w_down[l])
    return xf.reshape(batch, seq, d)
```

```python
import functools
import math

import numpy as np
import jax
import jax.numpy as jnp
from jax import lax
from jax.experimental import pallas as pl
from jax.experimental.pallas import tpu as pltpu

F32 = jnp.float32
BF16 = jnp.bfloat16
EPS = 1e-6
LOG2E = math.log2(math.e)

M_HEADS = 4
M_DQK = 128
M_DV = 256
CONV_K = 4
DA_HEADS = 8
DA_D = 64
DA_DV = 128
X_HEADS = 4
NUM_BUCKETS = 32
MAX_EXACT = NUM_BUCKETS // 2
MAX_DISTANCE = 128

LANES = 128
V7X_VMEM_BYTES = 64 * 2 ** 20
MLSTM_CHUNK = 256
DA_BLOCK = 256
FFN_BLOCK = 512
PROJ_BLOCK = 512


def _params(semantics, vmem_bytes):
    assert vmem_bytes <= V7X_VMEM_BYTES
    return pltpu.CompilerParams(dimension_semantics=semantics, vmem_limit_bytes=int(vmem_bytes))


def _rmsnorm(xf, g):
    return xf * lax.rsqrt(jnp.mean(xf * xf, axis=-1, keepdims=True) + EPS) * g


def _log_sigmoid(x):
    return jnp.minimum(x, 0.0) - jnp.log1p(jnp.exp(-jnp.abs(x)))


def _dot(a, b):
    return jnp.dot(a, b, preferred_element_type=F32)


def _dot_nt(a, b):
    return lax.dot_general(a, b, (((1,), (1,)), ((), ())), preferred_element_type=F32)


def _dot_tn(a, b):
    return lax.dot_general(a, b, (((0,), (0,)), ((), ())), preferred_element_type=F32)


def _cast_body(*refs, bn):
    *in_refs, o_ref = refs
    c = 0
    for x_ref in in_refs:
        for b in range(x_ref.shape[2] // bn):
            o_ref[0, c] = x_ref[0, :, b * bn:(b + 1) * bn].astype(BF16)
            c += 1


def _cast_blocks(ws, bn=None, *, kb=256):
    depth, k, _ = ws[0].shape
    bn = bn or ws[0].shape[2]
    n_total = sum(w.shape[2] for w in ws)
    kb = min(kb, k)
    vmem = 2 * kb * n_total * (4 + 2) + (4 << 20)
    return pl.pallas_call(
        functools.partial(_cast_body, bn=bn),
        out_shape=jax.ShapeDtypeStruct((depth, n_total // bn, k, bn), BF16),
        grid=(depth, k // kb),
        in_specs=[pl.BlockSpec((1, kb, w.shape[2]), lambda l, i: (l, i, 0)) for w in ws],
        out_specs=pl.BlockSpec((1, n_total // bn, kb, bn), lambda l, i: (l, 0, i, 0)),
        compiler_params=_params(("parallel", "parallel"), vmem),
        name="cast_blocks",
    )(*ws)


def _cast_w_in_body(x_ref, o_ref, gate_ref, *, gate0, n_gate, bn, scaled, scale):
    x = x_ref[0]
    left = x[:, :gate0]
    right = x[:, gate0 + n_gate:]
    right = jnp.concatenate([right[:, :scaled] * scale, right[:, scaled:]], axis=1)
    c = 0
    for part in (left, right):
        for b in range(part.shape[1] // bn):
            o_ref[0, c] = part[:, b * bn:(b + 1) * bn].astype(BF16)
            c += 1
    group = x[:, gate0:gate0 + LANES]
    lane = lax.broadcasted_iota(jnp.int32, group.shape, 1)
    gate_ref[0] = jnp.where(lane < n_gate, group, 0.0).astype(BF16)


def _cast_w_in(w_in, *, gate0, n_gate, scaled, scale, bn=PROJ_BLOCK, kb=256):
    depth, k, n = w_in.shape
    n_main = n - n_gate
    kb = min(kb, k)
    vmem = 2 * kb * n * 4 + 2 * kb * n_main * 2 + 3 * kb * n * 4 + (4 << 20)
    return pl.pallas_call(
        functools.partial(_cast_w_in_body, gate0=gate0, n_gate=n_gate, bn=bn, scaled=scaled, scale=scale),
        out_shape=[jax.ShapeDtypeStruct((depth, n_main // bn, k, bn), BF16),
                   jax.ShapeDtypeStruct((depth, k, LANES), BF16)],
        grid=(depth, k // kb),
        in_specs=[pl.BlockSpec((1, kb, n), lambda l, i: (l, i, 0))],
        out_specs=[pl.BlockSpec((1, n_main // bn, kb, bn), lambda l, i: (l, 0, i, 0)),
                   pl.BlockSpec((1, kb, LANES), lambda l, i: (l, i, 0))],
        compiler_params=_params(("parallel", "parallel"), vmem),
        name="cast_w_in",
    )(w_in)


def _ffn_body(x_ref, xlag_ref, gpre_ref, gpost_ref, wg_ref, wu_ref, wd_ref, o_ref, xn_ref, h_ref, acc_ref, *, nf):
    t = pl.program_id(0)
    j = t % nf
    slot = t % 2

    @pl.when(t == 0)
    def _():
        h_ref[1] = jnp.zeros(h_ref.shape[1:], h_ref.dtype)
        acc_ref[...] = jnp.zeros(acc_ref.shape, acc_ref.dtype)

    @pl.when(j == 0)
    def _():
        xn_ref[...] = _rmsnorm(x_ref[...], gpre_ref[...]).astype(BF16)

    xn = xn_ref[...]
    g = _dot(xn, wg_ref[0, 0])
    u = _dot(xn, wu_ref[0, 0])
    h_ref[slot] = (g * jax.nn.sigmoid(g) * u).astype(BF16)
    acc_ref[...] += _dot(h_ref[1 - slot], wd_ref[0, 0])

    @pl.when((j == 0) & (t > 0))
    def _():
        o_ref[...] = xlag_ref[...] + 0.5 * _rmsnorm(acc_ref[...], gpost_ref[...])
        acc_ref[...] = jnp.zeros(acc_ref.shape, acc_ref.dtype)


def _ffn(x, g_pre, g_post, w_gate, w_up, w_down, layer, *, bm=512):
    m, d = x.shape
    nf, bf = w_gate.shape[1], w_gate.shape[3]
    bm = min(bm, m)
    units = (m // bm) * nf
    cur = lambda t: jnp.minimum(t, units - 1)
    prev = lambda t: jnp.maximum(t - 1, 0)
    vmem = 7 * bm * d * 4 + bm * d * 2 + 6 * d * bf * 2 + 6 * bm * bf * 4 + (4 << 20)
    return pl.pallas_call(
        functools.partial(_ffn_body, nf=nf),
        out_shape=jax.ShapeDtypeStruct((m, d), F32),
        grid=(units + 1,),
        in_specs=[
            pl.BlockSpec((bm, d), lambda t: (cur(t) // nf, 0)),
            pl.BlockSpec((bm, d), lambda t: (prev(t) // nf, 0)),
            pl.BlockSpec((1, d), lambda t: (0, 0)),
            pl.BlockSpec((1, d), lambda t: (0, 0)),
            pl.BlockSpec((1, 1, d, bf), lambda t: (layer, cur(t) % nf, 0, 0)),
            pl.BlockSpec((1, 1, d, bf), lambda t: (layer, cur(t) % nf, 0, 0)),
            pl.BlockSpec((1, 1, bf, d), lambda t: (layer, 0, prev(t) % nf, 0)),
        ],
        out_specs=pl.BlockSpec((bm, d), lambda t: (prev(t) // nf, 0)),
        scratch_shapes=[pltpu.VMEM((bm, d), BF16), pltpu.VMEM((2, bm, bf), BF16), pltpu.VMEM((bm, d), F32)],
        compiler_params=_params(("arbitrary",), vmem),
        name="ffn",
    )(x, x, g_pre, g_post, w_gate, w_up, w_down)


def _normproj_body(x_ref, g_ref, w_ref, *rest, with_gate):
    if with_gate:
        wgate_ref, o_ref, gate_ref, xn_ref = rest
    else:
        o_ref, xn_ref = rest
    j = pl.program_id(1)

    @pl.when(j == 0)
    def _():
        xn = _rmsnorm(x_ref[...], g_ref[...]).astype(BF16)
        xn_ref[...] = xn
        if with_gate:
            gate_ref[...] = _dot(xn, wgate_ref[0])

    o_ref[...] = _dot(xn_ref[...], w_ref[0, 0]).astype(o_ref.dtype)


def _normproj(x, g, w, layer, w_gate=None, *, bm=1024):
    m, d = x.shape
    nb, bn = w.shape[1], w.shape[3]
    bm = min(bm, m)
    with_gate = w_gate is not None
    in_specs = [
        pl.BlockSpec((bm, d), lambda i, j: (i, 0)),
        pl.BlockSpec((1, d), lambda i, j: (0, 0)),
        pl.BlockSpec((1, 1, d, bn), lambda i, j: (layer, j, 0, 0)),
    ]
    out_shape = [jax.ShapeDtypeStruct((m, nb * bn), BF16)]
    out_specs = [pl.BlockSpec((bm, bn), lambda i, j: (i, j))]
    args = [x, g, w]
    if with_gate:
        in_specs.append(pl.BlockSpec((1, d, LANES), lambda i, j: (layer, 0, 0)))
        out_shape.append(jax.ShapeDtypeStruct((m, LANES), F32))
        out_specs.append(pl.BlockSpec((bm, LANES), lambda i, j: (i, 0)))
        args.append(w_gate)
    vmem = 2 * bm * d * 4 + bm * d * 2 + 2 * d * bn * 2 + 2 * bm * bn * 2 + bm * bn * 4 + (6 << 20)
    out = pl.pallas_call(
        functools.partial(_normproj_body, with_gate=with_gate),
        out_shape=out_shape,
        grid=(m // bm, nb),
        in_specs=in_specs,
        out_specs=out_specs,
        scratch_shapes=[pltpu.VMEM((bm, d), BF16)],
        compiler_params=_params(("parallel", "arbitrary"), vmem),
        name="normproj_gate" if with_gate else "normproj",
    )(*args)
    return out if with_gate else out[0]


def _outproj_body(*refs, n_parts):
    a_refs = refs[:n_parts]
    w_refs = refs[n_parts:2 * n_parts]
    x_ref, g_ref, o_ref = refs[2 * n_parts:]
    y = _dot(a_refs[0][...], w_refs[0][0, 0])
    for a_ref, w_ref in zip(a_refs[1:], w_refs[1:]):
        y += _dot(a_ref[...], w_ref[0, 0])
    o_ref[...] = x_ref[...] + _rmsnorm(y, g_ref[...])


def _outproj(x, g, acts, w, layer, *, bm=512):
    m, d = x.shape
    bm = min(bm, m)
    k_total = w.shape[2]
    in_specs = [pl.BlockSpec((bm, a.shape[1]), lambda i: (i, 0)) for a in acts]
    row0 = 0
    for a in acts:
        k = a.shape[1]
        assert row0 % k == 0
        in_specs.append(pl.BlockSpec((1, 1, k, d), functools.partial(lambda i, rb: (layer, 0, rb, 0), rb=row0 // k)))
        row0 += k
    assert row0 == k_total
    in_specs += [pl.BlockSpec((bm, d), lambda i: (i, 0)), pl.BlockSpec((1, d), lambda i: (0, 0))]
    vmem = 2 * k_total * d * 2 + 2 * bm * k_total * 2 + 5 * bm * d * 4 + (4 << 20)
    return pl.pallas_call(
        functools.partial(_outproj_body, n_parts=len(acts)),
        out_shape=jax.ShapeDtypeStruct((m, d), F32),
        grid=(m // bm,),
        in_specs=in_specs,
        out_specs=pl.BlockSpec((bm, d), lambda i: (i, 0)),
        compiler_params=_params(("parallel",), vmem),
        name="outproj",
    )(*acts, *([w] * len(acts)), x, g)


def _conv_silu(raw, w, b):
    rows = lax.broadcasted_iota(jnp.int32, raw.shape, 0)
    acc = raw * w[CONV_K - 1:CONV_K, :] + b
    for shift in range(1, CONV_K):
        shifted = jnp.where(rows >= shift, pltpu.roll(raw, shift, 0), 0.0)
        acc += shifted * w[CONV_K - 1 - shift:CONV_K - shift, :]
    return acc * jax.nn.sigmoid(acc)


def _mlstm_body(q_ref, k_ref, v_ref, og_ref, gates_ref, gbias_ref, cwq_ref, cbq_ref, cwk_ref, cbk_ref,
                norm_ref, o_ref, qs_ref, ks_ref, gs_ref, *, chunk):
    s = q_ref.shape[0]
    head = pl.program_id(1)
    qs_ref[...] = _conv_silu(q_ref[...].astype(F32), cwq_ref[...], cbq_ref[...]).astype(BF16)
    ks_ref[...] = _conv_silu(k_ref[...].astype(F32), cwk_ref[...], cbk_ref[...]) * (M_DQK ** -0.5)
    gs_ref[...] = gates_ref[...] + gbias_ref[...]

    r_i = lax.broadcasted_iota(jnp.int32, (chunk, chunk), 0)
    c_i = lax.broadcasted_iota(jnp.int32, (chunk, chunk), 1)
    causal = r_i >= c_i
    tri = causal.astype(F32)
    lane_is_head = lax.broadcasted_iota(jnp.int32, (chunk, LANES), 1) == head
    sub_is_head = lax.broadcasted_iota(jnp.int32, (LANES, chunk), 0) == head
    gain = norm_ref[...]

    def step(c, carry):
        cmat, nvec, mrun = carry
        rows = pl.ds(pl.multiple_of(c * chunk, chunk), chunk)
        gc = gs_ref[rows, :]
        bcum = jnp.dot(tri, _log_sigmoid(gc), preferred_element_type=F32, precision=lax.Precision.HIGHEST)
        bsh = pltpu.roll(bcum, LANES - M_HEADS, 1)
        b_col = jnp.sum(jnp.where(lane_is_head, bsh, 0.0), axis=-1, keepdims=True)
        i_col = jnp.sum(jnp.where(lane_is_head, gc, 0.0), axis=-1, keepdims=True)
        r_row = jnp.sum(jnp.where(sub_is_head, (gc - bsh).T, 0.0), axis=0, keepdims=True)
        g_end = b_col[chunk - 1:chunk, :]

        qb = qs_ref[rows, :]
        kf = ks_ref[rows, :]
        vb = v_ref[rows, :]
        dmat = jnp.where(causal, b_col + r_row, -jnp.inf)
        inter = b_col + mrun
        m_t = jnp.maximum(inter, jnp.max(dmat, axis=-1, keepdims=True))
        w_inter = jnp.exp(inter - m_t)
        p = jnp.exp(dmat - m_t) * _dot_nt(qb, kf.astype(BF16))
        num = w_inter * _dot(qb, cmat.astype(BF16)) + _dot(p.astype(BF16), vb)
        den = (w_inter * jnp.sum(qb.astype(F32) * nvec, axis=-1, keepdims=True)
               + jnp.sum(p, axis=-1, keepdims=True))
        hc = num / jnp.maximum(jnp.abs(den), jnp.exp(-m_t))

        s_w = g_end - b_col + i_col
        m_new = jnp.maximum(g_end + mrun, jnp.max(s_w, axis=0, keepdims=True))
        decay = jnp.exp(g_end + mrun - m_new)
        kw = kf * jnp.exp(s_w - m_new)
        cmat = decay * cmat + _dot_tn(kw.astype(BF16), vb)
        nvec = decay * nvec + jnp.sum(kw, axis=0, keepdims=True)

        y = _rmsnorm(hc, gain) * jax.nn.sigmoid(og_ref[rows, :].astype(F32))
        o_ref[rows, :] = y.astype(o_ref.dtype)
        return cmat, nvec, m_new

    init = (jnp.zeros((M_DQK, M_DV), F32), jnp.zeros((1, M_DQK), F32), jnp.zeros((1, 1), F32))
    lax.fori_loop(0, s // chunk, step, init)


def _mlstm(z, gates, gate_bias, conv_w, conv_b, norm, *, batch, seq):
    chunk = min(MLSTM_CHUNK, seq)
    qk_blocks = M_HEADS
    v_block0 = 2 * M_HEADS * M_DQK // M_DV
    og_block0 = v_block0 + M_HEADS
    vmem = seq * (2 * 2 * M_DQK * 2 + 3 * 2 * M_DV * 2 + 3 * LANES * 4 + M_DQK * 6) + (24 << 20)
    return pl.pallas_call(
        functools.partial(_mlstm_body, chunk=chunk),
        out_shape=jax.ShapeDtypeStruct((batch * seq, M_HEADS * M_DV), BF16),
        grid=(batch, M_HEADS),
        in_specs=[
            pl.BlockSpec((seq, M_DQK), lambda b, h: (b, h)),
            pl.BlockSpec((seq, M_DQK), lambda b, h: (b, qk_blocks + h)),
            pl.BlockSpec((seq, M_DV), lambda b, h: (b, v_block0 + h)),
            pl.BlockSpec((seq, M_DV), lambda b, h: (b, og_block0 + h)),
            pl.BlockSpec((seq, LANES), lambda b, h: (b, 0)),
            pl.BlockSpec((1, LANES), lambda b, h: (0, 0)),
            pl.BlockSpec((CONV_K, M_DQK), lambda b, h: (0, h)),
            pl.BlockSpec((1, M_DQK), lambda b, h: (0, h)),
            pl.BlockSpec((CONV_K, M_DQK), lambda b, h: (0, qk_blocks + h)),
            pl.BlockSpec((1, M_DQK), lambda b, h: (0, qk_blocks + h)),
            pl.BlockSpec((1, M_DV), lambda b, h: (0, h)),
        ],
        out_specs=pl.BlockSpec((seq, M_DV), lambda b, h: (b, h)),
        scratch_shapes=[pltpu.VMEM((seq, M_DQK), BF16), pltpu.VMEM((seq, M_DQK), F32),
                        pltpu.VMEM((seq, LANES), F32)],
        compiler_params=_params(("parallel", "parallel"), vmem),
        name="mlstm",
    )(z, z, z, z, gates, gate_bias, conv_w, conv_b, conv_w, conv_b, norm)


def _t5_bucket_np(n):
    nf = np.maximum(n, 1).astype(np.float64)
    val = np.log(nf / MAX_EXACT) / math.log(MAX_DISTANCE / MAX_EXACT) * (NUM_BUCKETS - MAX_EXACT)
    frac = val[(n > MAX_EXACT) & (n < MAX_DISTANCE)]
    assert np.all(np.abs(frac - np.round(frac)) > 1e-3)
    large = np.minimum(MAX_EXACT + val.astype(np.int64), NUM_BUCKETS - 1)
    return np.where(n < MAX_EXACT, n, large)


def _bucket_tile(blk):
    rel = np.arange(blk)[None, :] + blk - np.arange(2 * blk)[:, None]
    return np.where(rel >= 0, _t5_bucket_np(np.maximum(rel, 0)), -1).astype(np.int32)


def _bias_body(rb_ref, bucket_ref, t_ref):
    h = pl.program_id(0)
    bucket = bucket_ref[...]
    acc = jnp.zeros(bucket.shape, F32)
    for b in range(NUM_BUCKETS):
        acc = jnp.where(bucket == b, rb_ref[h * NUM_BUCKETS + b], acc)
    far = rb_ref[h * NUM_BUCKETS + NUM_BUCKETS - 1]
    t_ref[0] = jnp.where(bucket < 0, -jnp.inf, (acc - far) * LOG2E)


def _bias_tiles(rel_bias, blk):
    assert blk >= MAX_DISTANCE
    bucket = jnp.asarray(_bucket_tile(blk))
    return pl.pallas_call(
        _bias_body,
        out_shape=jax.ShapeDtypeStruct((DA_HEADS, 2 * blk, blk), F32),
        grid=(DA_HEADS,),
        in_specs=[pl.BlockSpec(memory_space=pltpu.SMEM),
                  pl.BlockSpec((2 * blk, blk), lambda h: (0, 0))],
        out_specs=pl.BlockSpec((1, 2 * blk, blk), lambda h: (h, 0, 0)),
        compiler_params=_params(("arbitrary",), 16 << 20),
        name="t5_bias_tiles",
    )(rel_bias.T.reshape(-1), bucket)


VT_ROWS = DA_DV + 16


def _diffattn_body(lam_ref, q_ref, k_ref, v_ref, t_ref, g_ref, o_ref, vt_ref, *, blk, lam_init):
    nb = q_ref.shape[0] // blk
    ones_rows = (lax.broadcasted_iota(jnp.int32, (VT_ROWS - DA_DV, blk), 0) == 0).astype(F32)
    for c in range(nb):
        vt = v_ref[c * blk:(c + 1) * blk, :].astype(F32).T
        vt_ref[c] = jnp.concatenate([vt, ones_rows], axis=0).astype(BF16)

    lp = lam_ref[...]
    lam = (jnp.exp(jnp.sum(lp[0:1] * lp[1:2], axis=-1, keepdims=True))
           - jnp.exp(jnp.sum(lp[2:3] * lp[3:4], axis=-1, keepdims=True)) + lam_init)
    sub = lax.broadcasted_iota(jnp.int32, (LANES, blk), 0)
    gain = g_ref[...]

    def query_maps(qi):
        qt = q_ref[qi * blk:(qi + 1) * blk, :].astype(F32).T
        return (jnp.where(sub < DA_D, qt, 0.0).astype(BF16), jnp.where(sub >= DA_D, qt, 0.0).astype(BF16))

    def scores(qts, qi, kc):
        kb = k_ref[kc * blk:(kc + 1) * blk, :]
        s = [_dot(kb, qt_m) for qt_m in qts]
        if kc >= qi - 1:
            bias = t_ref[0, blk:, :] if kc == qi else t_ref[0, :blk, :]
            s = [s_m + bias for s_m in s]
        return s

    units = [(qi, kc) for qi in range(nb) for kc in range(qi, -1, -1)]
    qts = query_maps(0)
    s_next = scores(qts, *units[0])
    state = [None, None]
    for u, (qi, kc) in enumerate(units):
        s_cur = s_next
        if u + 1 < len(units):
            if units[u + 1][0] != qi:
                qts = query_maps(units[u + 1][0])
            s_next = scores(qts, *units[u + 1])
        for a in range(2):
            s = s_cur[a]
            blk_max = jnp.max(s, axis=0, keepdims=True)
            if state[a] is None:
                m_new = blk_max
                acc = _dot(vt_ref[kc], jnp.exp2(s - m_new).astype(BF16))
            else:
                m, acc = state[a]
                m_new = jnp.maximum(m, blk_max)
                acc = acc * jnp.exp2(m - m_new) + _dot(vt_ref[kc], jnp.exp2(s - m_new).astype(BF16))
            state[a] = (m_new, acc)
        if kc == 0:
            outs = [acc[:DA_DV] / acc[DA_DV:DA_DV + 1] for _, acc in state]
            o = (outs[0] - lam * outs[1]).T
            o_ref[qi * blk:(qi + 1) * blk, :] = (_rmsnorm(o, gain) * (1.0 - lam_init)).astype(o_ref.dtype)
            state = [None, None]


def _diffattn(z, tiles, lam_p, subln, *, batch, seq, lam_init, col0):
    blk = tiles.shape[2]
    q0 = col0 // LANES
    k0 = q0 + DA_HEADS
    v0 = k0 + DA_HEADS
    vmem = 8 * seq * LANES * 2 + seq * VT_ROWS * 2 + 4 * blk * blk * 4 + 24 * blk * blk * 4 + (8 << 20)
    return pl.pallas_call(
        functools.partial(_diffattn_body, blk=blk, lam_init=lam_init),
        out_shape=jax.ShapeDtypeStruct((batch * seq, DA_HEADS * DA_DV), BF16),
        grid=(batch, DA_HEADS),
        in_specs=[
            pl.BlockSpec((4, DA_D), lambda b, h: (0, 0)),
            pl.BlockSpec((seq, LANES), lambda b, h: (b, q0 + h)),
            pl.BlockSpec((seq, LANES), lambda b, h: (b, k0 + h)),
            pl.BlockSpec((seq, LANES), lambda b, h: (b, v0 + h)),
            pl.BlockSpec((1, 2 * blk, blk), lambda b, h: (h, 0, 0)),
            pl.BlockSpec((1, DA_DV), lambda b, h: (0, 0)),
        ],
        out_specs=pl.BlockSpec((seq, DA_DV), lambda b, h: (b, h)),
        scratch_shapes=[pltpu.VMEM((seq // blk, VT_ROWS, blk), BF16)],
        compiler_params=_params(("parallel", "parallel"), vmem),
        name="diffattn",
    )(lam_p, z, z, z, tiles, subln)


def _xattn_body(q_ref, k_ref, v_ref, o_ref):
    dh = q_ref.shape[1]
    s = _dot_nt(q_ref[...], k_ref[...]) * (dh ** -0.5)
    e = jnp.exp(s - jnp.max(s, axis=-1, keepdims=True))
    a = e / jnp.sum(e, axis=-1, keepdims=True)
    o_ref[...] = _dot(a.astype(BF16), v_ref[...]).astype(o_ref.dtype)


def _xattn(q, kv, *, batch, seq, mem_len, bq=512):
    d = q.shape[1]
    dh = d // X_HEADS
    bq = min(bq, seq)
    nq = seq // bq
    vmem = 4 * bq * dh * 2 + 4 * mem_len * dh * 2 + 4 * bq * mem_len * 4 + (8 << 20)
    return pl.pallas_call(
        _xattn_body,
        out_shape=jax.ShapeDtypeStruct((batch * seq, d), BF16),
        grid=(batch, X_HEADS, nq),
        in_specs=[
            pl.BlockSpec((bq, dh), lambda b, h, i: (b * nq + i, h)),
            pl.BlockSpec((mem_len, dh), lambda b, h, i: (b, h)),
            pl.BlockSpec((mem_len, dh), lambda b, h, i: (b, X_HEADS + h)),
        ],
        out_specs=pl.BlockSpec((bq, dh), lambda b, h, i: (b * nq + i, h)),
        compiler_params=_params(("parallel", "parallel", "parallel"), vmem),
        name="xattn",
    )(q, kv, kv)


def kernel(x, mem, rel_bias, ffn1_norm_pre, ffn1_norm_post, ffn1_w_gate, ffn1_w_up, ffn1_w_down, mix_norm_pre, mix_norm_post, w_in, conv_w, conv_b, b_igate, b_fgate, mlstm_norm, diff_lambda, diff_subln, w_out, xattn_norm_pre, xattn_norm_post, mem_norm, xattn_wq, xattn_wk, xattn_wv, xattn_wo, ffn2_norm_pre, ffn2_norm_post, ffn2_w_gate, ffn2_w_up, ffn2_w_down):
    batch, seq, d = x.shape
    mem_len = mem.shape[1]
    depth = w_in.shape[0]
    n_gate = 2 * M_HEADS
    da_qw = DA_HEADS * 2 * DA_D
    gate0 = 2 * M_HEADS * M_DQK + 2 * M_HEADS * M_DV
    xf = x.reshape(batch * seq, d)
    memf = mem.reshape(batch * mem_len, d)
    row = lambda v: v.reshape(1, -1).astype(F32)

    tiles = _bias_tiles(rel_bias, min(DA_BLOCK, seq))
    w_main, w_gate = _cast_w_in(w_in, gate0=gate0, n_gate=n_gate, scaled=da_qw, scale=DA_D ** -0.5 * LOG2E)
    ffn1 = (_cast_blocks([ffn1_w_gate], FFN_BLOCK), _cast_blocks([ffn1_w_up], FFN_BLOCK), _cast_blocks([ffn1_w_down]))
    ffn2 = (_cast_blocks([ffn2_w_gate], FFN_BLOCK), _cast_blocks([ffn2_w_up], FFN_BLOCK), _cast_blocks([ffn2_w_down]))
    w_o = _cast_blocks([w_out])
    w_q = _cast_blocks([xattn_wq], PROJ_BLOCK)
    w_kv = _cast_blocks([xattn_wk, xattn_wv], PROJ_BLOCK)
    w_xo = _cast_blocks([xattn_wo])

    for l in range(depth):
        lam_init = 0.8 - 0.6 * math.exp(-0.3 * l)
        xf = _ffn(xf, row(ffn1_norm_pre[l]), row(ffn1_norm_post[l]), *ffn1, l)

        z, gates = _normproj(xf, row(mix_norm_pre[l]), w_main, l, w_gate)
        gate_bias = jnp.pad(jnp.concatenate([b_igate[l], b_fgate[l]]), (0, LANES - n_gate)).reshape(1, LANES)
        y_m = _mlstm(z, gates, gate_bias.astype(F32), conv_w[l].astype(F32), row(conv_b[l]),
                     row(mlstm_norm[l]), batch=batch, seq=seq)
        y_d = _diffattn(z, tiles, diff_lambda[l].astype(F32), row(diff_subln[l]),
                        batch=batch, seq=seq, lam_init=lam_init, col0=gate0)
        xf = _outproj(xf, row(mix_norm_post[l]), [y_m, y_d], w_o, l)

        q = _normproj(xf, row(xattn_norm_pre[l]), w_q, l)
        kv = _normproj(memf, row(mem_norm[l]), w_kv, l)
        c = _xattn(q, kv, batch=batch, seq=seq, mem_len=mem_len)
        xf = _outproj(xf, row(xattn_norm_post[l]), [c], w_xo, l)

        xf = _ffn(xf, row(ffn2_norm_pre[l]), row(ffn2_norm_post[l]), *ffn2, l)
    return xf.reshape(batch, seq, d)
```

```python
import functools
import math

import numpy as np
import jax
import jax.numpy as jnp
from jax import lax
from jax.experimental import pallas as pl
from jax.experimental.pallas import tpu as pltpu

F32 = jnp.float32
BF16 = jnp.bfloat16
EPS = 1e-6
LOG2E = math.log2(math.e)

M_HEADS = 4
M_DQK = 128
M_DV = 256
CONV_K = 4
DA_HEADS = 8
DA_D = 64
DA_DV = 128
X_HEADS = 4
NUM_BUCKETS = 32
MAX_EXACT = NUM_BUCKETS // 2
MAX_DISTANCE = 128

LANES = 128
V7X_VMEM_BYTES = 64 * 2 ** 20
MLSTM_CHUNK = 256
DA_BLOCK = 256
DA_LOOKAHEAD = 2
FFN_BLOCK = 512
PROJ_BLOCK = 512


def _params(semantics, vmem_bytes):
    assert vmem_bytes <= V7X_VMEM_BYTES
    return pltpu.CompilerParams(dimension_semantics=semantics, vmem_limit_bytes=int(vmem_bytes))


def _rmsnorm(xf, g):
    return xf * lax.rsqrt(jnp.mean(xf * xf, axis=-1, keepdims=True) + EPS) * g


def _log_sigmoid(x):
    return jnp.minimum(x, 0.0) - jnp.log1p(jnp.exp(-jnp.abs(x)))


def _dot(a, b):
    return jnp.dot(a, b, preferred_element_type=F32)


def _dot_nt(a, b):
    return lax.dot_general(a, b, (((1,), (1,)), ((), ())), preferred_element_type=F32)


def _dot_tn(a, b):
    return lax.dot_general(a, b, (((0,), (0,)), ((), ())), preferred_element_type=F32)


def _cast_body(*refs, bn, scale):
    *in_refs, o_ref = refs
    c = 0
    for x_ref in in_refs:
        for b in range(x_ref.shape[2] // bn):
            blk = x_ref[0, :, b * bn:(b + 1) * bn]
            o_ref[0, c] = (blk if scale is None else blk * scale).astype(BF16)
            c += 1


def _cast_blocks(ws, bn=None, *, scale=None, kb=256):
    depth, k, _ = ws[0].shape
    bn = bn or ws[0].shape[2]
    n_total = sum(w.shape[2] for w in ws)
    kb = min(kb, k)
    vmem = 2 * kb * n_total * (4 + 2) + (4 << 20)
    return pl.pallas_call(
        functools.partial(_cast_body, bn=bn, scale=scale),
        out_shape=jax.ShapeDtypeStruct((depth, n_total // bn, k, bn), BF16),
        grid=(depth, k // kb),
        in_specs=[pl.BlockSpec((1, kb, w.shape[2]), lambda l, i: (l, i, 0)) for w in ws],
        out_specs=pl.BlockSpec((1, n_total // bn, kb, bn), lambda l, i: (l, 0, i, 0)),
        compiler_params=_params(("parallel", "parallel"), vmem),
        name="cast_blocks",
    )(*ws)


def _cast_w_in_body(x_ref, o_ref, gate_ref, *, gate0, n_gate, bn, scaled, scale):
    x = x_ref[0]
    left = x[:, :gate0]
    right = x[:, gate0 + n_gate:]
    right = jnp.concatenate([right[:, :scaled] * scale, right[:, scaled:]], axis=1)
    c = 0
    for part in (left, right):
        for b in range(part.shape[1] // bn):
            o_ref[0, c] = part[:, b * bn:(b + 1) * bn].astype(BF16)
            c += 1
    group = x[:, gate0:gate0 + LANES]
    lane = lax.broadcasted_iota(jnp.int32, group.shape, 1)
    gate_ref[0] = jnp.where(lane < n_gate, group, 0.0).astype(BF16)


def _cast_w_in(w_in, *, gate0, n_gate, scaled, scale, bn=PROJ_BLOCK, kb=256):
    depth, k, n = w_in.shape
    n_main = n - n_gate
    kb = min(kb, k)
    vmem = 2 * kb * n * 4 + 2 * kb * n_main * 2 + 3 * kb * n * 4 + (4 << 20)
    return pl.pallas_call(
        functools.partial(_cast_w_in_body, gate0=gate0, n_gate=n_gate, bn=bn, scaled=scaled, scale=scale),
        out_shape=[jax.ShapeDtypeStruct((depth, n_main // bn, k, bn), BF16),
                   jax.ShapeDtypeStruct((depth, k, LANES), BF16)],
        grid=(depth, k // kb),
        in_specs=[pl.BlockSpec((1, kb, n), lambda l, i: (l, i, 0))],
        out_specs=[pl.BlockSpec((1, n_main // bn, kb, bn), lambda l, i: (l, 0, i, 0)),
                   pl.BlockSpec((1, kb, LANES), lambda l, i: (l, i, 0))],
        compiler_params=_params(("parallel", "parallel"), vmem),
        name="cast_w_in",
    )(w_in)


def _ffn_body(x_ref, xlag_ref, gpre_ref, gpost_ref, wg_ref, wu_ref, wd_ref, o_ref, xn_ref, h_ref, acc_ref, *, nf):
    t = pl.program_id(0)
    j = t % nf
    slot = t % 2

    @pl.when(t == 0)
    def _():
        h_ref[1] = jnp.zeros(h_ref.shape[1:], h_ref.dtype)
        acc_ref[...] = jnp.zeros(acc_ref.shape, acc_ref.dtype)

    @pl.when(j == 0)
    def _():
        xn_ref[...] = _rmsnorm(x_ref[...], gpre_ref[...]).astype(BF16)

    xn = xn_ref[...]
    g = _dot(xn, wg_ref[0, 0])
    u = _dot(xn, wu_ref[0, 0])
    h_ref[slot] = (g * jax.nn.sigmoid(g) * u).astype(BF16)
    acc_ref[...] += _dot(h_ref[1 - slot], wd_ref[0, 0])

    @pl.when((j == 0) & (t > 0))
    def _():
        o_ref[...] = xlag_ref[...] + 0.5 * _rmsnorm(acc_ref[...], gpost_ref[...])
        acc_ref[...] = jnp.zeros(acc_ref.shape, acc_ref.dtype)


def _ffn(x, g_pre, g_post, w_gate, w_up, w_down, layer, *, bm=512):
    m, d = x.shape
    nf, bf = w_gate.shape[1], w_gate.shape[3]
    bm = min(bm, m)
    units = (m // bm) * nf
    cur = lambda t: jnp.minimum(t, units - 1)
    prev = lambda t: jnp.maximum(t - 1, 0)
    vmem = 7 * bm * d * 4 + bm * d * 2 + 6 * d * bf * 2 + 6 * bm * bf * 4 + (4 << 20)
    return pl.pallas_call(
        functools.partial(_ffn_body, nf=nf),
        out_shape=jax.ShapeDtypeStruct((m, d), F32),
        grid=(units + 1,),
        in_specs=[
            pl.BlockSpec((bm, d), lambda t: (cur(t) // nf, 0)),
            pl.BlockSpec((bm, d), lambda t: (prev(t) // nf, 0)),
            pl.BlockSpec((1, d), lambda t: (0, 0)),
            pl.BlockSpec((1, d), lambda t: (0, 0)),
            pl.BlockSpec((1, 1, d, bf), lambda t: (layer, cur(t) % nf, 0, 0)),
            pl.BlockSpec((1, 1, d, bf), lambda t: (layer, cur(t) % nf, 0, 0)),
            pl.BlockSpec((1, 1, bf, d), lambda t: (layer, 0, prev(t) % nf, 0)),
        ],
        out_specs=pl.BlockSpec((bm, d), lambda t: (prev(t) // nf, 0)),
        scratch_shapes=[pltpu.VMEM((bm, d), BF16), pltpu.VMEM((2, bm, bf), BF16), pltpu.VMEM((bm, d), F32)],
        compiler_params=_params(("arbitrary",), vmem),
        name="ffn",
    )(x, x, g_pre, g_post, w_gate, w_up, w_down)


def _normproj_body(x_ref, g_ref, w_ref, *rest, with_gate):
    if with_gate:
        wgate_ref, o_ref, gate_ref, xn_ref = rest
    else:
        o_ref, xn_ref = rest
    j = pl.program_id(1)

    @pl.when(j == 0)
    def _():
        xn = _rmsnorm(x_ref[...], g_ref[...]).astype(BF16)
        xn_ref[...] = xn
        if with_gate:
            gate_ref[...] = _dot(xn, wgate_ref[0])

    o_ref[...] = _dot(xn_ref[...], w_ref[0, 0]).astype(o_ref.dtype)


def _normproj(x, g, w, layer, w_gate=None, *, bm=1024):
    m, d = x.shape
    nb, bn = w.shape[1], w.shape[3]
    bm = min(bm, m)
    with_gate = w_gate is not None
    in_specs = [
        pl.BlockSpec((bm, d), lambda i, j: (i, 0)),
        pl.BlockSpec((1, d), lambda i, j: (0, 0)),
        pl.BlockSpec((1, 1, d, bn), lambda i, j: (layer, j, 0, 0)),
    ]
    out_shape = [jax.ShapeDtypeStruct((m, nb * bn), BF16)]
    out_specs = [pl.BlockSpec((bm, bn), lambda i, j: (i, j))]
    args = [x, g, w]
    if with_gate:
        in_specs.append(pl.BlockSpec((1, d, LANES), lambda i, j: (layer, 0, 0)))
        out_shape.append(jax.ShapeDtypeStruct((m, LANES), F32))
        out_specs.append(pl.BlockSpec((bm, LANES), lambda i, j: (i, 0)))
        args.append(w_gate)
    vmem = 2 * bm * d * 4 + bm * d * 2 + 2 * d * bn * 2 + 2 * bm * bn * 2 + bm * bn * 4 + (6 << 20)
    out = pl.pallas_call(
        functools.partial(_normproj_body, with_gate=with_gate),
        out_shape=out_shape,
        grid=(m // bm, nb),
        in_specs=in_specs,
        out_specs=out_specs,
        scratch_shapes=[pltpu.VMEM((bm, d), BF16)],
        compiler_params=_params(("parallel", "arbitrary"), vmem),
        name="normproj_gate" if with_gate else "normproj",
    )(*args)
    return out if with_gate else out[0]


def _outproj_body(*refs, n_parts):
    a_refs = refs[:n_parts]
    w_refs = refs[n_parts:2 * n_parts]
    x_ref, g_ref, o_ref = refs[2 * n_parts:]
    y = _dot(a_refs[0][...], w_refs[0][0, 0])
    for a_ref, w_ref in zip(a_refs[1:], w_refs[1:]):
        y += _dot(a_ref[...], w_ref[0, 0])
    o_ref[...] = x_ref[...] + _rmsnorm(y, g_ref[...])


def _outproj(x, g, acts, w, layer, *, bm=512):
    m, d = x.shape
    bm = min(bm, m)
    k_total = w.shape[2]
    in_specs = [pl.BlockSpec((bm, a.shape[1]), lambda i: (i, 0)) for a in acts]
    row0 = 0
    for a in acts:
        k = a.shape[1]
        assert row0 % k == 0
        in_specs.append(pl.BlockSpec((1, 1, k, d), functools.partial(lambda i, rb: (layer, 0, rb, 0), rb=row0 // k)))
        row0 += k
    assert row0 == k_total
    in_specs += [pl.BlockSpec((bm, d), lambda i: (i, 0)), pl.BlockSpec((1, d), lambda i: (0, 0))]
    vmem = 2 * k_total * d * 2 + 2 * bm * k_total * 2 + 5 * bm * d * 4 + (4 << 20)
    return pl.pallas_call(
        functools.partial(_outproj_body, n_parts=len(acts)),
        out_shape=jax.ShapeDtypeStruct((m, d), F32),
        grid=(m // bm,),
        in_specs=in_specs,
        out_specs=pl.BlockSpec((bm, d), lambda i: (i, 0)),
        compiler_params=_params(("parallel",), vmem),
        name="outproj",
    )(*acts, *([w] * len(acts)), x, g)


def _conv_silu(raw, w, b):
    rows = lax.broadcasted_iota(jnp.int32, raw.shape, 0)
    acc = raw * w[CONV_K - 1:CONV_K, :] + b
    for shift in range(1, CONV_K):
        shifted = jnp.where(rows >= shift, pltpu.roll(raw, shift, 0), 0.0)
        acc += shifted * w[CONV_K - 1 - shift:CONV_K - shift, :]
    return acc * jax.nn.sigmoid(acc)


GATE_ROWS = 16


def _split3(x):
    hi = x.astype(BF16)
    r1 = x - hi.astype(F32)
    mid = r1.astype(BF16)
    return hi, mid, (r1 - mid.astype(F32)).astype(BF16)


def _mlstm_body(q_ref, k_ref, v_ref, og_ref, gates_ref, gbias_ref, cwq_ref, cbq_ref, cwk_ref, cbk_ref,
                norm_ref, o_ref, qs_ref, ks_ref, vx_ref, gt_ref, bt_ref, row_ref, *, chunk):
    s = q_ref.shape[0]
    nc = s // chunk
    head = pl.program_id(1)
    ext = vx_ref.shape[1]
    qs_ref[...] = _conv_silu(q_ref[...].astype(F32), cwq_ref[...], cbq_ref[...]).astype(BF16)
    ks_ref[...] = _conv_silu(k_ref[...].astype(F32), cwk_ref[...], cbk_ref[...]) * (M_DQK ** -0.5)
    ones_col = (lax.broadcasted_iota(jnp.int32, (s, ext - M_DV), 1) == 0).astype(BF16)
    vx_ref[...] = jnp.concatenate([v_ref[...], ones_col], axis=1)

    upper = (lax.broadcasted_iota(jnp.int32, (chunk, chunk), 0)
             <= lax.broadcasted_iota(jnp.int32, (chunk, chunk), 1)).astype(BF16)
    for c in range(nc):
        cols = slice(c * chunk, (c + 1) * chunk)
        gt = (gates_ref[cols, :] + gbias_ref[...]).T[:GATE_ROWS]
        gt_ref[:, cols] = gt
        bt_ref[:, cols] = sum(_dot(piece, upper) for piece in _split3(_log_sigmoid(gt)))

    i_row = gt_ref[pl.ds(head, 1), :]
    b_row = bt_ref[pl.ds(M_HEADS + head, 1), :]
    r_row = i_row - b_row
    pos = lax.broadcasted_iota(jnp.int32, (1, s), 1) % chunk
    cmax = r_row
    shift = 1
    while shift < chunk:
        cmax = jnp.maximum(cmax, jnp.where(pos >= shift, pltpu.roll(cmax, shift, 1), -jnp.inf))
        shift *= 2

    row_ref[...] = jnp.zeros(row_ref.shape, F32)
    m = jnp.zeros((1, 1), F32)
    ws_rows, decays = [], []
    for c in range(nc):
        cols = slice(c * chunk, (c + 1) * chunk)
        a = jnp.maximum(m, cmax[:, cols])
        a_last = a[:, chunk - 1:]
        b_c = b_row[:, cols]
        row_ref[0:1, cols] = a
        row_ref[1:2, cols] = jnp.exp(m - a)
        row_ref[2:3, cols] = jnp.exp(-(b_c + a))
        ws_rows.append(jnp.exp(r_row[:, cols] - a_last))
        decays.append(jnp.exp(m - a_last))
        m = b_c[:, chunk - 1:] + a_last

    causal = (lax.broadcasted_iota(jnp.int32, (chunk, chunk), 0)
              >= lax.broadcasted_iota(jnp.int32, (chunk, chunk), 1))
    gain = norm_ref[...]

    def independent(c):
        rows = slice(c * chunk, (c + 1) * chunk)
        kf = ks_ref[rows, :]
        s_qk = _dot_nt(qs_ref[rows, :], kf.astype(BF16))
        kv = _dot((kf.T * ws_rows[c]).astype(BF16), vx_ref[rows, :])
        return s_qk, kv

    cmat = None
    nxt = independent(0)
    for c in range(nc):
        rows = slice(c * chunk, (c + 1) * chunk)
        s_qk, kv = nxt
        if c + 1 < nc:
            nxt = independent(c + 1)
        col = row_ref[:, rows].T
        a_col, wi_col, en_col = col[:, 0:1], col[:, 1:2], col[:, 2:3]
        p = jnp.exp(jnp.where(causal, r_row[:, rows] - a_col, -jnp.inf)) * s_qk
        hx = _dot(p.astype(BF16), vx_ref[rows, :])
        if cmat is not None:
            hx += wi_col * _dot(qs_ref[rows, :], cmat.astype(BF16))
        hc = hx[:, :M_DV] / jnp.maximum(jnp.abs(hx[:, M_DV:M_DV + 1]), en_col)
        cmat = kv if cmat is None else decays[c] * cmat + kv
        y = _rmsnorm(hc, gain) * jax.nn.sigmoid(og_ref[rows, :].astype(F32))
        o_ref[rows, :] = y.astype(o_ref.dtype)


def _mlstm(z, gates, gate_bias, conv_w, conv_b, norm, *, batch, seq):
    chunk = min(MLSTM_CHUNK, seq)
    qk_blocks = M_HEADS
    v_block0 = 2 * M_HEADS * M_DQK // M_DV
    og_block0 = v_block0 + M_HEADS
    vmem = seq * (2 * 2 * M_DQK * 2 + 3 * 2 * M_DV * 2 + 3 * LANES * 4 + M_DQK * 6) + (24 << 20)
    return pl.pallas_call(
        functools.partial(_mlstm_body, chunk=chunk),
        out_shape=jax.ShapeDtypeStruct((batch * seq, M_HEADS * M_DV), BF16),
        grid=(batch, M_HEADS),
        in_specs=[
            pl.BlockSpec((seq, M_DQK), lambda b, h: (b, h)),
            pl.BlockSpec((seq, M_DQK), lambda b, h: (b, qk_blocks + h)),
            pl.BlockSpec((seq, M_DV), lambda b, h: (b, v_block0 + h)),
            pl.BlockSpec((seq, M_DV), lambda b, h: (b, og_block0 + h)),
            pl.BlockSpec((seq, LANES), lambda b, h: (b, 0)),
            pl.BlockSpec((1, LANES), lambda b, h: (0, 0)),
            pl.BlockSpec((CONV_K, M_DQK), lambda b, h: (0, h)),
            pl.BlockSpec((1, M_DQK), lambda b, h: (0, h)),
            pl.BlockSpec((CONV_K, M_DQK), lambda b, h: (0, qk_blocks + h)),
            pl.BlockSpec((1, M_DQK), lambda b, h: (0, qk_blocks + h)),
            pl.BlockSpec((1, M_DV), lambda b, h: (0, h)),
        ],
        out_specs=pl.BlockSpec((seq, M_DV), lambda b, h: (b, h)),
        scratch_shapes=[pltpu.VMEM((seq, M_DQK), BF16), pltpu.VMEM((seq, M_DQK), F32),
                        pltpu.VMEM((seq, M_DV + LANES), BF16), pltpu.VMEM((GATE_ROWS, seq), F32),
                        pltpu.VMEM((GATE_ROWS, seq), F32), pltpu.VMEM((LANES, seq), F32)],
        compiler_params=_params(("parallel", "parallel"), vmem),
        name="mlstm",
    )(z, z, z, z, gates, gate_bias, conv_w, conv_b, conv_w, conv_b, norm)


def _t5_bucket_np(n):
    nf = np.maximum(n, 1).astype(np.float64)
    val = np.log(nf / MAX_EXACT) / math.log(MAX_DISTANCE / MAX_EXACT) * (NUM_BUCKETS - MAX_EXACT)
    frac = val[(n > MAX_EXACT) & (n < MAX_DISTANCE)]
    assert np.all(np.abs(frac - np.round(frac)) > 1e-3)
    large = np.minimum(MAX_EXACT + val.astype(np.int64), NUM_BUCKETS - 1)
    return np.where(n < MAX_EXACT, n, large)


def _bucket_tile(blk):
    rel = np.arange(blk)[None, :] + blk - np.arange(2 * blk)[:, None]
    return np.where(rel >= 0, _t5_bucket_np(np.maximum(rel, 0)), -1).astype(np.int32)


def _bias_body(rb_ref, bucket_ref, t_ref):
    h = pl.program_id(0)
    bucket = bucket_ref[...]
    acc = jnp.zeros(bucket.shape, F32)
    for b in range(NUM_BUCKETS):
        acc = jnp.where(bucket == b, rb_ref[h * NUM_BUCKETS + b], acc)
    far = rb_ref[h * NUM_BUCKETS + NUM_BUCKETS - 1]
    t_ref[0] = jnp.where(bucket < 0, -jnp.inf, (acc - far) * LOG2E)


def _bias_tiles(rel_bias, blk):
    assert blk >= MAX_DISTANCE
    bucket = jnp.asarray(_bucket_tile(blk))
    return pl.pallas_call(
        _bias_body,
        out_shape=jax.ShapeDtypeStruct((DA_HEADS, 2 * blk, blk), F32),
        grid=(DA_HEADS,),
        in_specs=[pl.BlockSpec(memory_space=pltpu.SMEM),
                  pl.BlockSpec((2 * blk, blk), lambda h: (0, 0))],
        out_specs=pl.BlockSpec((1, 2 * blk, blk), lambda h: (h, 0, 0)),
        compiler_params=_params(("arbitrary",), 16 << 20),
        name="t5_bias_tiles",
    )(rel_bias.T.reshape(-1), bucket)


VT_ROWS = DA_DV + 16


def _diffattn_body(lam_ref, q_ref, k_ref, v_ref, t_ref, g_ref, o_ref, vt_ref, *, blk, lam_init):
    nb = q_ref.shape[0] // blk
    ones_rows = (lax.broadcasted_iota(jnp.int32, (VT_ROWS - DA_DV, blk), 0) == 0).astype(F32)
    for c in range(nb):
        vt = v_ref[c * blk:(c + 1) * blk, :].astype(F32).T
        vt_ref[c] = jnp.concatenate([vt, ones_rows], axis=0).astype(BF16)

    lp = lam_ref[...]
    lam = (jnp.exp(jnp.sum(lp[0:1] * lp[1:2], axis=-1, keepdims=True))
           - jnp.exp(jnp.sum(lp[2:3] * lp[3:4], axis=-1, keepdims=True)) + lam_init)
    sub = lax.broadcasted_iota(jnp.int32, (LANES, blk), 0)
    gain = g_ref[...]

    def query_maps(qi):
        qt = q_ref[qi * blk:(qi + 1) * blk, :].astype(F32).T
        return (jnp.where(sub < DA_D, qt, 0.0).astype(BF16), jnp.where(sub >= DA_D, qt, 0.0).astype(BF16))

    def scores(qts, qi, kc):
        kb = k_ref[kc * blk:(kc + 1) * blk, :]
        s = [_dot(kb, qt_m) for qt_m in qts]
        if kc >= qi - 1:
            bias = t_ref[0, blk:, :] if kc == qi else t_ref[0, :blk, :]
            s = [s_m + bias for s_m in s]
        return s

    units = [(qi, kc) for qi in range(nb) for kc in range(qi, -1, -1)]
    qts_of = {}

    def unit_scores(u):
        qi, kc = units[u]
        if qi not in qts_of:
            qts_of.clear()
            qts_of[qi] = query_maps(qi)
        return scores(qts_of[qi], qi, kc)

    ahead = [unit_scores(u) for u in range(min(DA_LOOKAHEAD, len(units)))]
    state = [None, None]
    for u, (qi, kc) in enumerate(units):
        s_cur = ahead.pop(0)
        if u + DA_LOOKAHEAD < len(units):
            ahead.append(unit_scores(u + DA_LOOKAHEAD))
        for a in range(2):
            s = s_cur[a]
            blk_max = jnp.max(s, axis=0, keepdims=True)
            if state[a] is None:
                m_new = blk_max
                acc = _dot(vt_ref[kc], jnp.exp2(s - m_new).astype(BF16))
            else:
                m, acc = state[a]
                m_new = jnp.maximum(m, blk_max)
                acc = acc * jnp.exp2(m - m_new) + _dot(vt_ref[kc], jnp.exp2(s - m_new).astype(BF16))
            state[a] = (m_new, acc)
        if kc == 0:
            outs = [acc[:DA_DV] / acc[DA_DV:DA_DV + 1] for _, acc in state]
            o = (outs[0] - lam * outs[1]).T
            o_ref[qi * blk:(qi + 1) * blk, :] = (_rmsnorm(o, gain) * (1.0 - lam_init)).astype(o_ref.dtype)
            state = [None, None]


def _diffattn(z, tiles, lam_p, subln, *, batch, seq, lam_init, col0):
    blk = tiles.shape[2]
    q0 = col0 // LANES
    k0 = q0 + DA_HEADS
    v0 = k0 + DA_HEADS
    vmem = 8 * seq * LANES * 2 + seq * VT_ROWS * 2 + 4 * blk * blk * 4 + 24 * blk * blk * 4 + (8 << 20)
    return pl.pallas_call(
        functools.partial(_diffattn_body, blk=blk, lam_init=lam_init),
        out_shape=jax.ShapeDtypeStruct((batch * seq, DA_HEADS * DA_DV), BF16),
        grid=(batch, DA_HEADS),
        in_specs=[
            pl.BlockSpec((4, DA_D), lambda b, h: (0, 0)),
            pl.BlockSpec((seq, LANES), lambda b, h: (b, q0 + h)),
            pl.BlockSpec((seq, LANES), lambda b, h: (b, k0 + h)),
            pl.BlockSpec((seq, LANES), lambda b, h: (b, v0 + h)),
            pl.BlockSpec((1, 2 * blk, blk), lambda b, h: (h, 0, 0)),
            pl.BlockSpec((1, DA_DV), lambda b, h: (0, 0)),
        ],
        out_specs=pl.BlockSpec((seq, DA_DV), lambda b, h: (b, h)),
        scratch_shapes=[pltpu.VMEM((seq // blk, VT_ROWS, blk), BF16)],
        compiler_params=_params(("parallel", "parallel"), vmem),
        name="diffattn",
    )(lam_p, z, z, z, tiles, subln)


def _xattn_body(q_ref, k_ref, v_ref, o_ref, *, blk):
    nb = q_ref.shape[0] // blk
    kb = k_ref[...]
    vb = v_ref[...]
    s_next = _dot_nt(q_ref[0:blk, :], kb)
    for j in range(nb):
        s = s_next
        if j + 1 < nb:
            s_next = _dot_nt(q_ref[(j + 1) * blk:(j + 2) * blk, :], kb)
        p = jnp.exp2(s - jnp.max(s, axis=-1, keepdims=True))
        o = _dot(p.astype(BF16), vb) / jnp.sum(p, axis=-1, keepdims=True)
        o_ref[j * blk:(j + 1) * blk, :] = o.astype(o_ref.dtype)


def _xattn(q, kv, *, batch, seq, mem_len, blk=256):
    d = q.shape[1]
    dh = d // X_HEADS
    blk = min(blk, seq)
    vmem = 4 * seq * dh * 2 + 4 * mem_len * dh * 2 + 8 * blk * (mem_len + dh) * 4 + (8 << 20)
    return pl.pallas_call(
        functools.partial(_xattn_body, blk=blk),
        out_shape=jax.ShapeDtypeStruct((batch * seq, d), BF16),
        grid=(batch, X_HEADS),
        in_specs=[
            pl.BlockSpec((seq, dh), lambda b, h: (b, h)),
            pl.BlockSpec((mem_len, dh), lambda b, h: (b, h)),
            pl.BlockSpec((mem_len, dh), lambda b, h: (b, X_HEADS + h)),
        ],
        out_specs=pl.BlockSpec((seq, dh), lambda b, h: (b, h)),
        compiler_params=_params(("parallel", "parallel"), vmem),
        name="xattn",
    )(q, kv, kv)


def kernel(x, mem, rel_bias, ffn1_norm_pre, ffn1_norm_post, ffn1_w_gate, ffn1_w_up, ffn1_w_down, mix_norm_pre, mix_norm_post, w_in, conv_w, conv_b, b_igate, b_fgate, mlstm_norm, diff_lambda, diff_subln, w_out, xattn_norm_pre, xattn_norm_post, mem_norm, xattn_wq, xattn_wk, xattn_wv, xattn_wo, ffn2_norm_pre, ffn2_norm_post, ffn2_w_gate, ffn2_w_up, ffn2_w_down):
    batch, seq, d = x.shape
    mem_len = mem.shape[1]
    depth = w_in.shape[0]
    n_gate = 2 * M_HEADS
    da_qw = DA_HEADS * 2 * DA_D
    gate0 = 2 * M_HEADS * M_DQK + 2 * M_HEADS * M_DV
    xf = x.reshape(batch * seq, d)
    memf = mem.reshape(batch * mem_len, d)
    row = lambda v: v.reshape(1, -1).astype(F32)

    tiles = _bias_tiles(rel_bias, min(DA_BLOCK, seq))
    w_main, w_gate = _cast_w_in(w_in, gate0=gate0, n_gate=n_gate, scaled=da_qw, scale=DA_D ** -0.5 * LOG2E)
    ffn1 = (_cast_blocks([ffn1_w_gate], FFN_BLOCK), _cast_blocks([ffn1_w_up], FFN_BLOCK), _cast_blocks([ffn1_w_down]))
    ffn2 = (_cast_blocks([ffn2_w_gate], FFN_BLOCK), _cast_blocks([ffn2_w_up], FFN_BLOCK), _cast_blocks([ffn2_w_down]))
    w_o = _cast_blocks([w_out])
    w_q = _cast_blocks([xattn_wq], PROJ_BLOCK, scale=(d // X_HEADS) ** -0.5 * LOG2E)
    w_kv = _cast_blocks([xattn_wk, xattn_wv], PROJ_BLOCK)
    w_xo = _cast_blocks([xattn_wo])

    for l in range(depth):
        lam_init = 0.8 - 0.6 * math.exp(-0.3 * l)
        xf = _ffn(xf, row(ffn1_norm_pre[l]), row(ffn1_norm_post[l]), *ffn1, l)

        z, gates = _normproj(xf, row(mix_norm_pre[l]), w_main, l, w_gate)
        gate_bias = jnp.pad(jnp.concatenate([b_igate[l], b_fgate[l]]), (0, LANES - n_gate)).reshape(1, LANES)
        y_m = _mlstm(z, gates, gate_bias.astype(F32), conv_w[l].astype(F32), row(conv_b[l]),
                     row(mlstm_norm[l]), batch=batch, seq=seq)
        y_d = _diffattn(z, tiles, diff_lambda[l].astype(F32), row(diff_subln[l]),
                        batch=batch, seq=seq, lam_init=lam_init, col0=gate0)
        xf = _outproj(xf, row(mix_norm_post[l]), [y_m, y_d], w_o, l)

        q = _normproj(xf, row(xattn_norm_pre[l]), w_q, l)
        kv = _normproj(memf, row(mem_norm[l]), w_kv, l)
        c = _xattn(q, kv, batch=batch, seq=seq, mem_len=mem_len)
        xf = _outproj(xf, row(xattn_norm_post[l]), [c], w_xo, l)

        xf = _ffn(xf, row(ffn2_norm_pre[l]), row(ffn2_norm_post[l]), *ffn2, l)
    return xf.reshape(batch, seq, d)
```

```python
import functools
import math

import numpy as np
import jax
import jax.numpy as jnp
from jax import lax
from jax.experimental import pallas as pl
from jax.experimental.pallas import tpu as pltpu

F32 = jnp.float32
BF16 = jnp.bfloat16
EPS = 1e-6
LOG2E = math.log2(math.e)

M_HEADS = 4
M_DQK = 128
M_DV = 256
CONV_K = 4
DA_HEADS = 8
DA_D = 64
DA_DV = 128
X_HEADS = 4
NUM_BUCKETS = 32
MAX_EXACT = NUM_BUCKETS // 2
MAX_DISTANCE = 128

LANES = 128
MXU_N = 256
V7X_VMEM_BYTES = 64 * 2 ** 20
MLSTM_CHUNK = 256
DA_BLOCK = 256
DA_LOOKAHEAD = 2
FFN_BLOCK = 512
PROJ_BLOCK = 512


def _params(semantics, vmem_bytes):
    assert vmem_bytes <= V7X_VMEM_BYTES
    return pltpu.CompilerParams(dimension_semantics=semantics, vmem_limit_bytes=int(vmem_bytes))


def _rmsnorm(xf, g):
    return xf * lax.rsqrt(jnp.mean(xf * xf, axis=-1, keepdims=True) + EPS) * g


def _log_sigmoid(x):
    return jnp.minimum(x, 0.0) - jnp.log1p(jnp.exp(-jnp.abs(x)))


def _dot(a, b):
    return jnp.dot(a, b, preferred_element_type=F32)


def _dot_nt(a, b):
    return lax.dot_general(a, b, (((1,), (1,)), ((), ())), preferred_element_type=F32)


def _dot_tn(a, b):
    return lax.dot_general(a, b, (((0,), (0,)), ((), ())), preferred_element_type=F32)


def _cast_body(*refs, bn, scale):
    *in_refs, o_ref = refs
    c = 0
    for x_ref in in_refs:
        for b in range(x_ref.shape[2] // bn):
            blk = x_ref[0, :, b * bn:(b + 1) * bn]
            o_ref[0, c] = (blk if scale is None else blk * scale).astype(BF16)
            c += 1


def _cast_blocks(ws, bn=None, *, scale=None, kb=256):
    depth, k, _ = ws[0].shape
    bn = bn or ws[0].shape[2]
    n_total = sum(w.shape[2] for w in ws)
    kb = min(kb, k)
    vmem = 2 * kb * n_total * (4 + 2) + (4 << 20)
    return pl.pallas_call(
        functools.partial(_cast_body, bn=bn, scale=scale),
        out_shape=jax.ShapeDtypeStruct((depth, n_total // bn, k, bn), BF16),
        grid=(depth, k // kb),
        in_specs=[pl.BlockSpec((1, kb, w.shape[2]), lambda l, i: (l, i, 0)) for w in ws],
        out_specs=pl.BlockSpec((1, n_total // bn, kb, bn), lambda l, i: (l, 0, i, 0)),
        compiler_params=_params(("parallel", "parallel"), vmem),
        name="cast_blocks",
    )(*ws)


def _cast_w_in_body(x_ref, o_ref, gate_ref, *, gate0, n_gate, bn, scaled, scale):
    x = x_ref[0]
    left = x[:, :gate0]
    right = x[:, gate0 + n_gate:]
    right = jnp.concatenate([right[:, :scaled] * scale, right[:, scaled:]], axis=1)
    c = 0
    for part in (left, right):
        for b in range(part.shape[1] // bn):
            o_ref[0, c] = part[:, b * bn:(b + 1) * bn].astype(BF16)
            c += 1
    group = x[:, gate0:gate0 + LANES]
    lane = lax.broadcasted_iota(jnp.int32, group.shape, 1)
    gate_ref[0] = jnp.where(lane < n_gate, group, 0.0).astype(BF16)


def _cast_w_in(w_in, *, gate0, n_gate, scaled, scale, bn=PROJ_BLOCK, kb=256):
    depth, k, n = w_in.shape
    n_main = n - n_gate
    kb = min(kb, k)
    vmem = 2 * kb * n * 4 + 2 * kb * n_main * 2 + 3 * kb * n * 4 + (4 << 20)
    return pl.pallas_call(
        functools.partial(_cast_w_in_body, gate0=gate0, n_gate=n_gate, bn=bn, scaled=scaled, scale=scale),
        out_shape=[jax.ShapeDtypeStruct((depth, n_main // bn, k, bn), BF16),
                   jax.ShapeDtypeStruct((depth, k, LANES), BF16)],
        grid=(depth, k // kb),
        in_specs=[pl.BlockSpec((1, kb, n), lambda l, i: (l, i, 0))],
        out_specs=[pl.BlockSpec((1, n_main // bn, kb, bn), lambda l, i: (l, 0, i, 0)),
                   pl.BlockSpec((1, kb, LANES), lambda l, i: (l, i, 0))],
        compiler_params=_params(("parallel", "parallel"), vmem),
        name="cast_w_in",
    )(w_in)


def _ffn_up_body(x_ref, g_ref, wg_ref, wu_ref, h_ref, xn_ref):
    @pl.when(pl.program_id(1) == 0)
    def _():
        xn_ref[...] = _rmsnorm(x_ref[...], g_ref[...]).astype(BF16)

    xn = xn_ref[...]
    g = _dot(xn, wg_ref[0, 0])
    u = _dot(xn, wu_ref[0, 0])
    h_ref[...] = (g * jax.nn.sigmoid(g) * u).astype(h_ref.dtype)


def _ffn(x, g_pre, g_post, w_gate, w_up, w_down, layer, *, bm=1024, bm_down=256):
    m, d = x.shape
    nf, bf = w_gate.shape[1], w_gate.shape[3]
    f = nf * bf
    bm = min(bm, m)
    assert m % bm == 0
    vmem_up = 2 * bm * d * 4 + bm * d * 2 + 4 * d * bf * 2 + 2 * bm * bf * 2 + 3 * bm * bf * 4 + (4 << 20)
    hidden = pl.pallas_call(
        _ffn_up_body,
        out_shape=jax.ShapeDtypeStruct((m, f), BF16),
        grid=(m // bm, nf),
        in_specs=[
            pl.BlockSpec((bm, d), lambda i, j: (i, 0)),
            pl.BlockSpec((1, d), lambda i, j: (0, 0)),
            pl.BlockSpec((1, 1, d, bf), lambda i, j: (layer, j, 0, 0)),
            pl.BlockSpec((1, 1, d, bf), lambda i, j: (layer, j, 0, 0)),
        ],
        out_specs=pl.BlockSpec((bm, bf), lambda i, j: (i, j)),
        scratch_shapes=[pltpu.VMEM((bm, d), BF16)],
        compiler_params=_params(("parallel", "arbitrary"), vmem_up),
        name="ffn_up",
    )(x, g_pre, w_gate, w_up)
    return _outproj(x, g_post, [hidden], w_down, layer, res_scale=0.5, bm=bm_down)


def _normproj_body(x_ref, g_ref, w_ref, *rest, with_gate):
    if with_gate:
        wgate_ref, o_ref, gate_ref, xn_ref = rest
    else:
        o_ref, xn_ref = rest
    j = pl.program_id(1)

    @pl.when(j == 0)
    def _():
        xn = _rmsnorm(x_ref[...], g_ref[...]).astype(BF16)
        xn_ref[...] = xn
        if with_gate:
            gate_ref[...] = _dot(xn, wgate_ref[0])

    o_ref[...] = _dot(xn_ref[...], w_ref[0, 0]).astype(o_ref.dtype)


def _normproj(x, g, w, layer, w_gate=None, *, bm=1024):
    m, d = x.shape
    nb, bn = w.shape[1], w.shape[3]
    bm = min(bm, m)
    assert m % bm == 0
    with_gate = w_gate is not None
    in_specs = [
        pl.BlockSpec((bm, d), lambda i, j: (i, 0)),
        pl.BlockSpec((1, d), lambda i, j: (0, 0)),
        pl.BlockSpec((1, 1, d, bn), lambda i, j: (layer, j, 0, 0)),
    ]
    out_shape = [jax.ShapeDtypeStruct((m, nb * bn), BF16)]
    out_specs = [pl.BlockSpec((bm, bn), lambda i, j: (i, j))]
    args = [x, g, w]
    if with_gate:
        in_specs.append(pl.BlockSpec((1, d, LANES), lambda i, j: (layer, 0, 0)))
        out_shape.append(jax.ShapeDtypeStruct((m, LANES), F32))
        out_specs.append(pl.BlockSpec((bm, LANES), lambda i, j: (i, 0)))
        args.append(w_gate)
    vmem = 2 * bm * d * 4 + bm * d * 2 + 2 * d * bn * 2 + 2 * bm * bn * 2 + bm * bn * 4 + (6 << 20)
    out = pl.pallas_call(
        functools.partial(_normproj_body, with_gate=with_gate),
        out_shape=out_shape,
        grid=(m // bm, nb),
        in_specs=in_specs,
        out_specs=out_specs,
        scratch_shapes=[pltpu.VMEM((bm, d), BF16)],
        compiler_params=_params(("parallel", "arbitrary"), vmem),
        name="normproj_gate" if with_gate else "normproj",
    )(*args)
    return out if with_gate else out[0]


def _outproj_body(*refs, n_parts, res_scale):
    a_refs = refs[:n_parts]
    w_refs = refs[n_parts:2 * n_parts]
    x_ref, g_ref, o_ref = refs[2 * n_parts:]
    y = _dot(a_refs[0][...], w_refs[0][0, 0])
    for a_ref, w_ref in zip(a_refs[1:], w_refs[1:]):
        y += _dot(a_ref[...], w_ref[0, 0])
    o_ref[...] = x_ref[...] + res_scale * _rmsnorm(y, g_ref[...])


def _outproj(x, g, acts, w, layer, *, res_scale=1.0, bm=512):
    m, d = x.shape
    bm = min(bm, m)
    assert m % bm == 0
    k_total = w.shape[2]
    in_specs = [pl.BlockSpec((bm, a.shape[1]), lambda i: (i, 0)) for a in acts]
    row0 = 0
    for a in acts:
        k = a.shape[1]
        assert row0 % k == 0
        in_specs.append(pl.BlockSpec((1, 1, k, d), functools.partial(lambda i, rb: (layer, 0, rb, 0), rb=row0 // k),
                                     pipeline_mode=pl.Buffered(1)))
        row0 += k
    assert row0 == k_total
    in_specs += [pl.BlockSpec((bm, d), lambda i: (i, 0)), pl.BlockSpec((1, d), lambda i: (0, 0))]
    vmem = k_total * d * 2 + 3 * bm * k_total * 2 + 6 * bm * d * 4 + (6 << 20)
    return pl.pallas_call(
        functools.partial(_outproj_body, n_parts=len(acts), res_scale=res_scale),
        out_shape=jax.ShapeDtypeStruct((m, d), F32),
        grid=(m // bm,),
        in_specs=in_specs,
        out_specs=pl.BlockSpec((bm, d), lambda i: (i, 0)),
        compiler_params=_params(("parallel",), vmem),
        name="outproj",
    )(*acts, *([w] * len(acts)), x, g)


def _conv_silu(raw, w, b):
    rows = lax.broadcasted_iota(jnp.int32, raw.shape, 0)
    acc = raw * w[CONV_K - 1:CONV_K, :] + b
    for shift in range(1, CONV_K):
        shifted = jnp.where(rows >= shift, pltpu.roll(raw, shift, 0), 0.0)
        acc += shifted * w[CONV_K - 1 - shift:CONV_K - shift, :]
    return acc * jax.nn.sigmoid(acc)


GATE_ROWS = 16


def _split3(x):
    hi = x.astype(BF16)
    r1 = x - hi.astype(F32)
    mid = r1.astype(BF16)
    return hi, mid, (r1 - mid.astype(F32)).astype(BF16)


def _mlstm_body(q_ref, k_ref, v_ref, og_ref, gates_ref, gbias_ref, cwq_ref, cbq_ref, cwk_ref, cbk_ref,
                norm_ref, o_ref, qs_ref, ks_ref, vx_ref, gt_ref, bt_ref, row_ref, *, chunk):
    s = q_ref.shape[0]
    nc = s // chunk
    head = pl.program_id(1)
    ext = vx_ref.shape[1]
    qs_ref[...] = _conv_silu(q_ref[...].astype(F32), cwq_ref[...], cbq_ref[...]).astype(BF16)
    ks_ref[...] = _conv_silu(k_ref[...].astype(F32), cwk_ref[...], cbk_ref[...]) * (M_DQK ** -0.5)
    ones_col = (lax.broadcasted_iota(jnp.int32, (s, ext - M_DV), 1) == 0).astype(BF16)
    vx_ref[...] = jnp.concatenate([v_ref[...], ones_col], axis=1)

    upper = (lax.broadcasted_iota(jnp.int32, (chunk, chunk), 0)
             <= lax.broadcasted_iota(jnp.int32, (chunk, chunk), 1)).astype(BF16)
    for c in range(nc):
        cols = slice(c * chunk, (c + 1) * chunk)
        gt = (gates_ref[cols, :] + gbias_ref[...]).T[:GATE_ROWS]
        gt_ref[:, cols] = gt
        bt_ref[:, cols] = sum(_dot(piece, upper) for piece in _split3(_log_sigmoid(gt)))

    i_row = gt_ref[pl.ds(head, 1), :]
    b_row = bt_ref[pl.ds(M_HEADS + head, 1), :]
    r_row = i_row - b_row
    pos = lax.broadcasted_iota(jnp.int32, (1, s), 1) % chunk
    cmax = r_row
    shift = 1
    while shift < chunk:
        cmax = jnp.maximum(cmax, jnp.where(pos >= shift, pltpu.roll(cmax, shift, 1), -jnp.inf))
        shift *= 2

    row_ref[...] = jnp.zeros(row_ref.shape, F32)
    m = jnp.zeros((1, 1), F32)
    ws_rows, decays = [], []
    for c in range(nc):
        cols = slice(c * chunk, (c + 1) * chunk)
        a = jnp.maximum(m, cmax[:, cols])
        a_last = a[:, chunk - 1:]
        b_c = b_row[:, cols]
        row_ref[0:1, cols] = a
        row_ref[1:2, cols] = jnp.exp(m - a)
        row_ref[2:3, cols] = jnp.exp(-(b_c + a))
        ws_rows.append(jnp.exp(r_row[:, cols] - a_last))
        decays.append(jnp.exp(m - a_last))
        m = b_c[:, chunk - 1:] + a_last

    causal = (lax.broadcasted_iota(jnp.int32, (chunk, chunk), 0)
              >= lax.broadcasted_iota(jnp.int32, (chunk, chunk), 1))
    gain = norm_ref[...]

    def independent(c):
        rows = slice(c * chunk, (c + 1) * chunk)
        kf = ks_ref[rows, :]
        s_qk = _dot_nt(qs_ref[rows, :], kf.astype(BF16))
        kv = _dot((kf.T * ws_rows[c]).astype(BF16), vx_ref[rows, :])
        return s_qk, kv

    cmat = None
    nxt = independent(0)
    for c in range(nc):
        rows = slice(c * chunk, (c + 1) * chunk)
        s_qk, kv = nxt
        if c + 1 < nc:
            nxt = independent(c + 1)
        col = row_ref[:, rows].T
        a_col, wi_col, en_col = col[:, 0:1], col[:, 1:2], col[:, 2:3]
        p = jnp.exp(jnp.where(causal, r_row[:, rows] - a_col, -jnp.inf)) * s_qk
        hx = _dot(p.astype(BF16), vx_ref[rows, :])
        if cmat is not None:
            hx += wi_col * _dot(qs_ref[rows, :], cmat.astype(BF16))
        hc = hx[:, :M_DV] / jnp.maximum(jnp.abs(hx[:, M_DV:M_DV + 1]), en_col)
        cmat = kv if cmat is None else decays[c] * cmat + kv
        y = _rmsnorm(hc, gain) * jax.nn.sigmoid(og_ref[rows, :].astype(F32))
        o_ref[rows, :] = y.astype(o_ref.dtype)


def _mlstm(z, gates, gate_bias, conv_w, conv_b, norm, *, batch, seq):
    chunk = min(MLSTM_CHUNK, seq)
    qk_blocks = M_HEADS
    v_block0 = 2 * M_HEADS * M_DQK // M_DV
    og_block0 = v_block0 + M_HEADS
    vmem = seq * (2 * 2 * M_DQK * 2 + 3 * 2 * M_DV * 2 + 3 * LANES * 4 + M_DQK * 6) + (24 << 20)
    return pl.pallas_call(
        functools.partial(_mlstm_body, chunk=chunk),
        out_shape=jax.ShapeDtypeStruct((batch * seq, M_HEADS * M_DV), BF16),
        grid=(batch, M_HEADS),
        in_specs=[
            pl.BlockSpec((seq, M_DQK), lambda b, h: (b, h)),
            pl.BlockSpec((seq, M_DQK), lambda b, h: (b, qk_blocks + h)),
            pl.BlockSpec((seq, M_DV), lambda b, h: (b, v_block0 + h)),
            pl.BlockSpec((seq, M_DV), lambda b, h: (b, og_block0 + h)),
            pl.BlockSpec((seq, LANES), lambda b, h: (b, 0)),
            pl.BlockSpec((1, LANES), lambda b, h: (0, 0)),
            pl.BlockSpec((CONV_K, M_DQK), lambda b, h: (0, h)),
            pl.BlockSpec((1, M_DQK), lambda b, h: (0, h)),
            pl.BlockSpec((CONV_K, M_DQK), lambda b, h: (0, qk_blocks + h)),
            pl.BlockSpec((1, M_DQK), lambda b, h: (0, qk_blocks + h)),
            pl.BlockSpec((1, M_DV), lambda b, h: (0, h)),
        ],
        out_specs=pl.BlockSpec((seq, M_DV), lambda b, h: (b, h)),
        scratch_shapes=[pltpu.VMEM((seq, M_DQK), BF16), pltpu.VMEM((seq, M_DQK), F32),
                        pltpu.VMEM((seq, M_DV + LANES), BF16), pltpu.VMEM((GATE_ROWS, seq), F32),
                        pltpu.VMEM((GATE_ROWS, seq), F32), pltpu.VMEM((LANES, seq), F32)],
        compiler_params=_params(("parallel", "parallel"), vmem),
        name="mlstm",
    )(z, z, z, z, gates, gate_bias, conv_w, conv_b, conv_w, conv_b, norm)


def _t5_bucket_np(n):
    nf = np.maximum(n, 1).astype(np.float64)
    val = np.log(nf / MAX_EXACT) / math.log(MAX_DISTANCE / MAX_EXACT) * (NUM_BUCKETS - MAX_EXACT)
    frac = val[(n > MAX_EXACT) & (n < MAX_DISTANCE)]
    assert np.all(np.abs(frac - np.round(frac)) > 1e-3)
    large = np.minimum(MAX_EXACT + val.astype(np.int64), NUM_BUCKETS - 1)
    return np.where(n < MAX_EXACT, n, large)


def _bucket_tile(blk):
    rel = np.arange(blk)[None, :] + blk - np.arange(2 * blk)[:, None]
    return np.where(rel >= 0, _t5_bucket_np(np.maximum(rel, 0)), -1).astype(np.int32)


def _bias_body(rb_ref, bucket_ref, t_ref):
    h = pl.program_id(0)
    bucket = bucket_ref[...]
    acc = jnp.zeros(bucket.shape, F32)
    for b in range(NUM_BUCKETS):
        acc = jnp.where(bucket == b, rb_ref[h * NUM_BUCKETS + b], acc)
    far = rb_ref[h * NUM_BUCKETS + NUM_BUCKETS - 1]
    t_ref[0] = jnp.where(bucket < 0, -jnp.inf, (acc - far) * LOG2E)


def _bias_tiles(rel_bias, blk):
    assert blk >= MAX_DISTANCE
    bucket = jnp.asarray(_bucket_tile(blk))
    return pl.pallas_call(
        _bias_body,
        out_shape=jax.ShapeDtypeStruct((DA_HEADS, 2 * blk, blk), F32),
        grid=(DA_HEADS,),
        in_specs=[pl.BlockSpec(memory_space=pltpu.SMEM),
                  pl.BlockSpec((2 * blk, blk), lambda h: (0, 0))],
        out_specs=pl.BlockSpec((1, 2 * blk, blk), lambda h: (h, 0, 0)),
        compiler_params=_params(("arbitrary",), 16 << 20),
        name="t5_bias_tiles",
    )(rel_bias.T.reshape(-1), bucket)


VT_ROWS = DA_DV + 16


def _diffattn_body(lam_ref, q_ref, k_ref, v_ref, t_ref, g_ref, o_ref, vt_ref, *, blk, lam_init):
    nb = q_ref.shape[0] // blk
    ones_rows = (lax.broadcasted_iota(jnp.int32, (VT_ROWS - DA_DV, blk), 0) == 0).astype(F32)
    for c in range(nb):
        vt = v_ref[c * blk:(c + 1) * blk, :].astype(F32).T
        vt_ref[c] = jnp.concatenate([vt, ones_rows], axis=0).astype(BF16)

    lp = lam_ref[...]
    lam = (jnp.exp(jnp.sum(lp[0:1] * lp[1:2], axis=-1, keepdims=True))
           - jnp.exp(jnp.sum(lp[2:3] * lp[3:4], axis=-1, keepdims=True)) + lam_init)
    sub = lax.broadcasted_iota(jnp.int32, (LANES, blk), 0)
    gain = g_ref[...]

    def query_maps(qi):
        qt = q_ref[qi * blk:(qi + 1) * blk, :].astype(F32).T
        return (jnp.where(sub < DA_D, qt, 0.0).astype(BF16), jnp.where(sub >= DA_D, qt, 0.0).astype(BF16))

    def scores(qts, qi, kc):
        kb = k_ref[kc * blk:(kc + 1) * blk, :]
        s = [_dot(kb, qt_m) for qt_m in qts]
        if kc >= qi - 1:
            bias = t_ref[0, blk:, :] if kc == qi else t_ref[0, :blk, :]
            s = [s_m + bias for s_m in s]
        return s

    units = [(qi, kc) for qi in range(nb) for kc in range(qi, -1, -1)]
    qts_of = {}

    def unit_scores(u):
        qi, kc = units[u]
        if qi not in qts_of:
            qts_of.clear()
            qts_of[qi] = query_maps(qi)
        return scores(qts_of[qi], qi, kc)

    ahead = [unit_scores(u) for u in range(min(DA_LOOKAHEAD, len(units)))]
    state = [None, None]
    for u, (qi, kc) in enumerate(units):
        s_cur = ahead.pop(0)
        if u + DA_LOOKAHEAD < len(units):
            ahead.append(unit_scores(u + DA_LOOKAHEAD))
        for a in range(2):
            s = s_cur[a]
            blk_max = jnp.max(s, axis=0, keepdims=True)
            if state[a] is None:
                m_new = blk_max
                acc = _dot(vt_ref[kc], jnp.exp2(s - m_new).astype(BF16))
            else:
                m, acc = state[a]
                m_new = jnp.maximum(m, blk_max)
                acc = acc * jnp.exp2(m - m_new) + _dot(vt_ref[kc], jnp.exp2(s - m_new).astype(BF16))
            state[a] = (m_new, acc)
        if kc == 0:
            outs = [acc[:DA_DV] / acc[DA_DV:DA_DV + 1] for _, acc in state]
            o = (outs[0] - lam * outs[1]).T
            o_ref[qi * blk:(qi + 1) * blk, :] = (_rmsnorm(o, gain) * (1.0 - lam_init)).astype(o_ref.dtype)
            state = [None, None]


def _diffattn(z, tiles, lam_p, subln, *, batch, seq, lam_init, col0):
    blk = tiles.shape[2]
    q0 = col0 // LANES
    k0 = q0 + DA_HEADS
    v0 = k0 + DA_HEADS
    vmem = 8 * seq * LANES * 2 + seq * VT_ROWS * 2 + 4 * blk * blk * 4 + 24 * blk * blk * 4 + (8 << 20)
    return pl.pallas_call(
        functools.partial(_diffattn_body, blk=blk, lam_init=lam_init),
        out_shape=jax.ShapeDtypeStruct((batch * seq, DA_HEADS * DA_DV), BF16),
        grid=(batch, DA_HEADS),
        in_specs=[
            pl.BlockSpec((4, DA_D), lambda b, h: (0, 0)),
            pl.BlockSpec((seq, LANES), lambda b, h: (b, q0 + h)),
            pl.BlockSpec((seq, LANES), lambda b, h: (b, k0 + h)),
            pl.BlockSpec((seq, LANES), lambda b, h: (b, v0 + h)),
            pl.BlockSpec((1, 2 * blk, blk), lambda b, h: (h, 0, 0)),
            pl.BlockSpec((1, DA_DV), lambda b, h: (0, 0)),
        ],
        out_specs=pl.BlockSpec((seq, DA_DV), lambda b, h: (b, h)),
        scratch_shapes=[pltpu.VMEM((seq // blk, VT_ROWS, blk), BF16)],
        compiler_params=_params(("parallel", "parallel"), vmem),
        name="diffattn",
    )(lam_p, z, z, z, tiles, subln)


def _xattn_body(q_ref, k_ref, v_ref, o_ref, *, blk):
    nb = q_ref.shape[0] // blk
    kb = k_ref[...]
    vb = v_ref[...]
    s_next = _dot_nt(q_ref[0:blk, :], kb)
    for j in range(nb):
        s = s_next
        if j + 1 < nb:
            s_next = _dot_nt(q_ref[(j + 1) * blk:(j + 2) * blk, :], kb)
        p = jnp.exp2(s - jnp.max(s, axis=-1, keepdims=True))
        o = _dot(p.astype(BF16), vb) / jnp.sum(p, axis=-1, keepdims=True)
        o_ref[j * blk:(j + 1) * blk, :] = o.astype(o_ref.dtype)


def _xattn(q, kv, *, batch, seq, mem_len, blk=256):
    d = q.shape[1]
    dh = d // X_HEADS
    blk = min(blk, seq)
    vmem = 4 * seq * dh * 2 + 4 * mem_len * dh * 2 + 8 * blk * (mem_len + dh) * 4 + (8 << 20)
    return pl.pallas_call(
        functools.partial(_xattn_body, blk=blk),
        out_shape=jax.ShapeDtypeStruct((batch * seq, d), BF16),
        grid=(batch, X_HEADS),
        in_specs=[
            pl.BlockSpec((seq, dh), lambda b, h: (b, h)),
            pl.BlockSpec((mem_len, dh), lambda b, h: (b, h)),
            pl.BlockSpec((mem_len, dh), lambda b, h: (b, X_HEADS + h)),
        ],
        out_specs=pl.BlockSpec((seq, dh), lambda b, h: (b, h)),
        compiler_params=_params(("parallel", "parallel"), vmem),
        name="xattn",
    )(q, kv, kv)


def kernel(x, mem, rel_bias, ffn1_norm_pre, ffn1_norm_post, ffn1_w_gate, ffn1_w_up, ffn1_w_down, mix_norm_pre, mix_norm_post, w_in, conv_w, conv_b, b_igate, b_fgate, mlstm_norm, diff_lambda, diff_subln, w_out, xattn_norm_pre, xattn_norm_post, mem_norm, xattn_wq, xattn_wk, xattn_wv, xattn_wo, ffn2_norm_pre, ffn2_norm_post, ffn2_w_gate, ffn2_w_up, ffn2_w_down):
    batch, seq, d = x.shape
    mem_len = mem.shape[1]
    depth = w_in.shape[0]
    n_gate = 2 * M_HEADS
    da_qw = DA_HEADS * 2 * DA_D
    gate0 = 2 * M_HEADS * M_DQK + 2 * M_HEADS * M_DV
    xf = x.reshape(batch * seq, d)
    memf = mem.reshape(batch * mem_len, d)
    row = lambda v: v.reshape(1, -1).astype(F32)

    tiles = _bias_tiles(rel_bias, min(DA_BLOCK, seq))
    w_main, w_gate = _cast_w_in(w_in, gate0=gate0, n_gate=n_gate, scaled=da_qw, scale=DA_D ** -0.5 * LOG2E)
    ffn1 = (_cast_blocks([ffn1_w_gate], FFN_BLOCK), _cast_blocks([ffn1_w_up], FFN_BLOCK), _cast_blocks([ffn1_w_down]))
    ffn2 = (_cast_blocks([ffn2_w_gate], FFN_BLOCK), _cast_blocks([ffn2_w_up], FFN_BLOCK), _cast_blocks([ffn2_w_down]))
    w_o = _cast_blocks([w_out])
    w_q = _cast_blocks([xattn_wq], PROJ_BLOCK, scale=(d // X_HEADS) ** -0.5 * LOG2E)
    w_kv = _cast_blocks([xattn_wk, xattn_wv], PROJ_BLOCK)
    w_xo = _cast_blocks([xattn_wo])

    for l in range(depth):
        lam_init = 0.8 - 0.6 * math.exp(-0.3 * l)
        xf = _ffn(xf, row(ffn1_norm_pre[l]), row(ffn1_norm_post[l]), *ffn1, l)

        z, gates = _normproj(xf, row(mix_norm_pre[l]), w_main, l, w_gate)
        gate_bias = jnp.pad(jnp.concatenate([b_igate[l], b_fgate[l]]), (0, LANES - n_gate)).reshape(1, LANES)
        y_m = _mlstm(z, gates, gate_bias.astype(F32), conv_w[l].astype(F32), row(conv_b[l]),
                     row(mlstm_norm[l]), batch=batch, seq=seq)
        y_d = _diffattn(z, tiles, diff_lambda[l].astype(F32), row(diff_subln[l]),
                        batch=batch, seq=seq, lam_init=lam_init, col0=gate0)
        xf = _outproj(xf, row(mix_norm_post[l]), [y_m, y_d], w_o, l)

        q = _normproj(xf, row(xattn_norm_pre[l]), w_q, l)
        kv = _normproj(memf, row(mem_norm[l]), w_kv, l)
        c = _xattn(q, kv, batch=batch, seq=seq, mem_len=mem_len)
        xf = _outproj(xf, row(xattn_norm_post[l]), [c], w_xo, l)

        xf = _ffn(xf, row(ffn2_norm_pre[l]), row(ffn2_norm_post[l]), *ffn2, l)
    return xf.reshape(batch, seq, d)
```

```python
import functools
import math

import numpy as np
import jax
import jax.numpy as jnp
from jax import lax
from jax.experimental import pallas as pl
from jax.experimental.pallas import tpu as pltpu

F32 = jnp.float32
BF16 = jnp.bfloat16
EPS = 1e-6
LOG2E = math.log2(math.e)

M_HEADS = 4
M_DQK = 128
M_DV = 256
CONV_K = 4
DA_HEADS = 8
DA_D = 64
DA_DV = 128
X_HEADS = 4
NUM_BUCKETS = 32
MAX_EXACT = NUM_BUCKETS // 2
MAX_DISTANCE = 128

LANES = 128
MXU_N = 256
V7X_VMEM_BYTES = 64 * 2 ** 20
MLSTM_CHUNK = 256
DA_BLOCK = 256
DA_LOOKAHEAD = 2
FFN_BLOCK = 512
PROJ_BLOCK = 1024


def _params(semantics, vmem_bytes):
    assert vmem_bytes <= V7X_VMEM_BYTES
    return pltpu.CompilerParams(dimension_semantics=semantics, vmem_limit_bytes=int(vmem_bytes))


def _rmsnorm(xf, g):
    return xf * lax.rsqrt(jnp.mean(xf * xf, axis=-1, keepdims=True) + EPS) * g


def _log_sigmoid(x):
    return jnp.minimum(x, 0.0) - jnp.log1p(jnp.exp(-jnp.abs(x)))


def _dot(a, b):
    return jnp.dot(a, b, preferred_element_type=F32)


def _dot_nt(a, b):
    return lax.dot_general(a, b, (((1,), (1,)), ((), ())), preferred_element_type=F32)


def _dot_tn(a, b):
    return lax.dot_general(a, b, (((0,), (0,)), ((), ())), preferred_element_type=F32)


def _cast_body(*refs, bn, scale):
    *in_refs, o_ref = refs
    c = 0
    for x_ref in in_refs:
        for b in range(x_ref.shape[2] // bn):
            blk = x_ref[0, :, b * bn:(b + 1) * bn]
            o_ref[0, c] = (blk if scale is None else blk * scale).astype(BF16)
            c += 1


def _cast_blocks(ws, bn=None, *, scale=None, kb=256):
    depth, k, _ = ws[0].shape
    bn = bn or ws[0].shape[2]
    n_total = sum(w.shape[2] for w in ws)
    kb = min(kb, k)
    vmem = 2 * kb * n_total * (4 + 2) + (4 << 20)
    return pl.pallas_call(
        functools.partial(_cast_body, bn=bn, scale=scale),
        out_shape=jax.ShapeDtypeStruct((depth, n_total // bn, k, bn), BF16),
        grid=(depth, k // kb),
        in_specs=[pl.BlockSpec((1, kb, w.shape[2]), lambda l, i: (l, i, 0)) for w in ws],
        out_specs=pl.BlockSpec((1, n_total // bn, kb, bn), lambda l, i: (l, 0, i, 0)),
        compiler_params=_params(("parallel", "parallel"), vmem),
        name="cast_blocks",
    )(*ws)


def _cast_w_in_body(x_ref, o_ref, gate_ref, *, gate0, n_gate, bn, scaled, scale):
    x = x_ref[0]
    left = x[:, :gate0]
    right = x[:, gate0 + n_gate:]
    right = jnp.concatenate([right[:, :scaled] * scale, right[:, scaled:]], axis=1)
    c = 0
    for part in (left, right):
        for b in range(part.shape[1] // bn):
            o_ref[0, c] = part[:, b * bn:(b + 1) * bn].astype(BF16)
            c += 1
    group = x[:, gate0:gate0 + LANES]
    lane = lax.broadcasted_iota(jnp.int32, group.shape, 1)
    gate_ref[0] = jnp.where(lane < n_gate, group, 0.0).astype(BF16)


def _cast_w_in(w_in, *, gate0, n_gate, scaled, scale, bn=PROJ_BLOCK, kb=256):
    depth, k, n = w_in.shape
    n_main = n - n_gate
    kb = min(kb, k)
    vmem = 2 * kb * n * 4 + 2 * kb * n_main * 2 + 3 * kb * n * 4 + (4 << 20)
    return pl.pallas_call(
        functools.partial(_cast_w_in_body, gate0=gate0, n_gate=n_gate, bn=bn, scaled=scaled, scale=scale),
        out_shape=[jax.ShapeDtypeStruct((depth, n_main // bn, k, bn), BF16),
                   jax.ShapeDtypeStruct((depth, k, LANES), BF16)],
        grid=(depth, k // kb),
        in_specs=[pl.BlockSpec((1, kb, n), lambda l, i: (l, i, 0))],
        out_specs=[pl.BlockSpec((1, n_main // bn, kb, bn), lambda l, i: (l, 0, i, 0)),
                   pl.BlockSpec((1, kb, LANES), lambda l, i: (l, i, 0))],
        compiler_params=_params(("parallel", "parallel"), vmem),
        name="cast_w_in",
    )(w_in)


def _ffn_up_body(x_ref, g_ref, wg_ref, wu_ref, h_ref, xn_ref):
    @pl.when(pl.program_id(1) == 0)
    def _():
        xn_ref[...] = _rmsnorm(x_ref[...], g_ref[...]).astype(BF16)

    xn = xn_ref[...]
    g = _dot(xn, wg_ref[0, 0])
    u = _dot(xn, wu_ref[0, 0])
    h_ref[...] = (g * jax.nn.sigmoid(g) * u).astype(h_ref.dtype)


def _ffn(x, g_pre, g_post, w_gate, w_up, w_down, layer, *, bm=1024, bm_down=256):
    m, d = x.shape
    nf, bf = w_gate.shape[1], w_gate.shape[3]
    f = nf * bf
    bm = min(bm, m)
    assert m % bm == 0
    vmem_up = 2 * bm * d * 4 + bm * d * 2 + 4 * d * bf * 2 + 2 * bm * bf * 2 + 3 * bm * bf * 4 + (4 << 20)
    hidden = pl.pallas_call(
        _ffn_up_body,
        out_shape=jax.ShapeDtypeStruct((m, f), BF16),
        grid=(m // bm, nf),
        in_specs=[
            pl.BlockSpec((bm, d), lambda i, j: (i, 0)),
            pl.BlockSpec((1, d), lambda i, j: (0, 0)),
            pl.BlockSpec((1, 1, d, bf), lambda i, j: (layer, j, 0, 0)),
            pl.BlockSpec((1, 1, d, bf), lambda i, j: (layer, j, 0, 0)),
        ],
        out_specs=pl.BlockSpec((bm, bf), lambda i, j: (i, j)),
        scratch_shapes=[pltpu.VMEM((bm, d), BF16)],
        compiler_params=_params(("parallel", "arbitrary"), vmem_up),
        name="ffn_up",
    )(x, g_pre, w_gate, w_up)
    return _outproj(x, g_post, [hidden], w_down, layer, res_scale=0.5, bm=bm_down)


def _normproj_body(x_ref, g_ref, w_ref, *rest, with_gate):
    if with_gate:
        wgate_ref, o_ref, gate_ref, xn_ref = rest
    else:
        o_ref, xn_ref = rest
    j = pl.program_id(1)

    @pl.when(j == 0)
    def _():
        xn = _rmsnorm(x_ref[...], g_ref[...]).astype(BF16)
        xn_ref[...] = xn
        if with_gate:
            gate_ref[...] = _dot(xn, wgate_ref[0])

    o_ref[...] = _dot(xn_ref[...], w_ref[0, 0]).astype(o_ref.dtype)


def _normproj(x, g, w, layer, w_gate=None, *, bm=1024):
    m, d = x.shape
    nb, bn = w.shape[1], w.shape[3]
    bm = min(bm, m)
    assert m % bm == 0
    with_gate = w_gate is not None
    in_specs = [
        pl.BlockSpec((bm, d), lambda i, j: (i, 0)),
        pl.BlockSpec((1, d), lambda i, j: (0, 0)),
        pl.BlockSpec((1, 1, d, bn), lambda i, j: (layer, j, 0, 0)),
    ]
    out_shape = [jax.ShapeDtypeStruct((m, nb * bn), BF16)]
    out_specs = [pl.BlockSpec((bm, bn), lambda i, j: (i, j))]
    args = [x, g, w]
    if with_gate:
        in_specs.append(pl.BlockSpec((1, d, LANES), lambda i, j: (layer, 0, 0)))
        out_shape.append(jax.ShapeDtypeStruct((m, LANES), F32))
        out_specs.append(pl.BlockSpec((bm, LANES), lambda i, j: (i, 0)))
        args.append(w_gate)
    vmem = 2 * bm * d * 4 + bm * d * 2 + 2 * d * bn * 2 + 2 * bm * bn * 2 + bm * bn * 4 + (6 << 20)
    out = pl.pallas_call(
        functools.partial(_normproj_body, with_gate=with_gate),
        out_shape=out_shape,
        grid=(m // bm, nb),
        in_specs=in_specs,
        out_specs=out_specs,
        scratch_shapes=[pltpu.VMEM((bm, d), BF16)],
        compiler_params=_params(("parallel", "arbitrary"), vmem),
        name="normproj_gate" if with_gate else "normproj",
    )(*args)
    return out if with_gate else out[0]


def _outproj_body(*refs, n_parts, res_scale):
    a_refs = refs[:n_parts]
    w_refs = refs[n_parts:2 * n_parts]
    x_ref, g_ref, o_ref = refs[2 * n_parts:]
    y = _dot(a_refs[0][...], w_refs[0][0, 0])
    for a_ref, w_ref in zip(a_refs[1:], w_refs[1:]):
        y += _dot(a_ref[...], w_ref[0, 0])
    o_ref[...] = x_ref[...] + res_scale * _rmsnorm(y, g_ref[...])


def _outproj(x, g, acts, w, layer, *, res_scale=1.0, bm=512):
    m, d = x.shape
    bm = min(bm, m)
    assert m % bm == 0
    k_total = w.shape[2]
    in_specs = [pl.BlockSpec((bm, a.shape[1]), lambda i: (i, 0)) for a in acts]
    row0 = 0
    for a in acts:
        k = a.shape[1]
        assert row0 % k == 0
        in_specs.append(pl.BlockSpec((1, 1, k, d), functools.partial(lambda i, rb: (layer, 0, rb, 0), rb=row0 // k),
                                     pipeline_mode=pl.Buffered(1)))
        row0 += k
    assert row0 == k_total
    in_specs += [pl.BlockSpec((bm, d), lambda i: (i, 0)), pl.BlockSpec((1, d), lambda i: (0, 0))]
    vmem = k_total * d * 2 + 3 * bm * k_total * 2 + 6 * bm * d * 4 + (6 << 20)
    return pl.pallas_call(
        functools.partial(_outproj_body, n_parts=len(acts), res_scale=res_scale),
        out_shape=jax.ShapeDtypeStruct((m, d), F32),
        grid=(m // bm,),
        in_specs=in_specs,
        out_specs=pl.BlockSpec((bm, d), lambda i: (i, 0)),
        compiler_params=_params(("parallel",), vmem),
        name="outproj",
    )(*acts, *([w] * len(acts)), x, g)


def _conv_silu(raw, w, b):
    rows = lax.broadcasted_iota(jnp.int32, raw.shape, 0)
    acc = raw * w[CONV_K - 1:CONV_K, :] + b
    for shift in range(1, CONV_K):
        shifted = jnp.where(rows >= shift, pltpu.roll(raw, shift, 0), 0.0)
        acc += shifted * w[CONV_K - 1 - shift:CONV_K - shift, :]
    return acc * jax.nn.sigmoid(acc)


GATE_ROWS = 16


def _split3(x):
    hi = x.astype(BF16)
    r1 = x - hi.astype(F32)
    mid = r1.astype(BF16)
    return hi, mid, (r1 - mid.astype(F32)).astype(BF16)


def _mlstm_body(q_ref, k_ref, v_ref, og_ref, gates_ref, gbias_ref, cwq_ref, cbq_ref, cwk_ref, cbk_ref,
                norm_ref, o_ref, qs_ref, ks_ref, vx_ref, gt_ref, bt_ref, row_ref, *, chunk):
    s = q_ref.shape[0]
    nc = s // chunk
    head = pl.program_id(1)
    ext = vx_ref.shape[1]
    qs_ref[...] = _conv_silu(q_ref[...].astype(F32), cwq_ref[...], cbq_ref[...]).astype(BF16)
    ks_ref[...] = _conv_silu(k_ref[...].astype(F32), cwk_ref[...], cbk_ref[...]) * (M_DQK ** -0.5)
    ones_col = (lax.broadcasted_iota(jnp.int32, (s, ext - M_DV), 1) == 0).astype(BF16)
    vx_ref[...] = jnp.concatenate([v_ref[...], ones_col], axis=1)

    upper = (lax.broadcasted_iota(jnp.int32, (chunk, chunk), 0)
             <= lax.broadcasted_iota(jnp.int32, (chunk, chunk), 1)).astype(BF16)
    for c in range(nc):
        cols = slice(c * chunk, (c + 1) * chunk)
        gt = (gates_ref[cols, :] + gbias_ref[...]).T[:GATE_ROWS]
        gt_ref[:, cols] = gt
        bt_ref[:, cols] = sum(_dot(piece, upper) for piece in _split3(_log_sigmoid(gt)))

    i_row = gt_ref[pl.ds(head, 1), :]
    b_row = bt_ref[pl.ds(M_HEADS + head, 1), :]
    r_row = i_row - b_row
    pos = lax.broadcasted_iota(jnp.int32, (1, s), 1) % chunk
    cmax = r_row
    shift = 1
    while shift < chunk:
        cmax = jnp.maximum(cmax, jnp.where(pos >= shift, pltpu.roll(cmax, shift, 1), -jnp.inf))
        shift *= 2

    row_ref[...] = jnp.zeros(row_ref.shape, F32)
    m = jnp.zeros((1, 1), F32)
    ws_rows, decays = [], []
    for c in range(nc):
        cols = slice(c * chunk, (c + 1) * chunk)
        a = jnp.maximum(m, cmax[:, cols])
        a_last = a[:, chunk - 1:]
        b_c = b_row[:, cols]
        row_ref[0:1, cols] = a
        row_ref[1:2, cols] = jnp.exp(m - a)
        row_ref[2:3, cols] = jnp.exp(-(b_c + a))
        ws_rows.append(jnp.exp(r_row[:, cols] - a_last))
        decays.append(jnp.exp(m - a_last))
        m = b_c[:, chunk - 1:] + a_last

    causal = (lax.broadcasted_iota(jnp.int32, (chunk, chunk), 0)
              >= lax.broadcasted_iota(jnp.int32, (chunk, chunk), 1))
    gain = norm_ref[...]

    def independent(c):
        rows = slice(c * chunk, (c + 1) * chunk)
        kf = ks_ref[rows, :]
        s_qk = _dot_nt(qs_ref[rows, :], kf.astype(BF16))
        kv = _dot((kf.T * ws_rows[c]).astype(BF16), vx_ref[rows, :])
        return s_qk, kv

    cmat = None
    nxt = independent(0)
    for c in range(nc):
        rows = slice(c * chunk, (c + 1) * chunk)
        s_qk, kv = nxt
        if c + 1 < nc:
            nxt = independent(c + 1)
        col = row_ref[:, rows].T
        a_col, wi_col, en_col = col[:, 0:1], col[:, 1:2], col[:, 2:3]
        p = jnp.exp(jnp.where(causal, r_row[:, rows] - a_col, -jnp.inf)) * s_qk
        hx = _dot(p.astype(BF16), vx_ref[rows, :])
        if cmat is not None:
            hx += wi_col * _dot(qs_ref[rows, :], cmat.astype(BF16))
        hc = hx[:, :M_DV] / jnp.maximum(jnp.abs(hx[:, M_DV:M_DV + 1]), en_col)
        cmat = kv if cmat is None else decays[c] * cmat + kv
        y = _rmsnorm(hc, gain) * jax.nn.sigmoid(og_ref[rows, :].astype(F32))
        o_ref[rows, :] = y.astype(o_ref.dtype)


def _mlstm(z, gates, gate_bias, conv_w, conv_b, norm, *, batch, seq):
    chunk = min(MLSTM_CHUNK, seq)
    qk_blocks = M_HEADS
    v_block0 = 2 * M_HEADS * M_DQK // M_DV
    og_block0 = v_block0 + M_HEADS
    vmem = seq * (2 * 2 * M_DQK * 2 + 3 * 2 * M_DV * 2 + 3 * LANES * 4 + M_DQK * 6) + (24 << 20)
    return pl.pallas_call(
        functools.partial(_mlstm_body, chunk=chunk),
        out_shape=jax.ShapeDtypeStruct((batch * seq, M_HEADS * M_DV), BF16),
        grid=(batch, M_HEADS),
        in_specs=[
            pl.BlockSpec((seq, M_DQK), lambda b, h: (b, h)),
            pl.BlockSpec((seq, M_DQK), lambda b, h: (b, qk_blocks + h)),
            pl.BlockSpec((seq, M_DV), lambda b, h: (b, v_block0 + h)),
            pl.BlockSpec((seq, M_DV), lambda b, h: (b, og_block0 + h)),
            pl.BlockSpec((seq, LANES), lambda b, h: (b, 0)),
            pl.BlockSpec((1, LANES), lambda b, h: (0, 0)),
            pl.BlockSpec((CONV_K, M_DQK), lambda b, h: (0, h)),
            pl.BlockSpec((1, M_DQK), lambda b, h: (0, h)),
            pl.BlockSpec((CONV_K, M_DQK), lambda b, h: (0, qk_blocks + h)),
            pl.BlockSpec((1, M_DQK), lambda b, h: (0, qk_blocks + h)),
            pl.BlockSpec((1, M_DV), lambda b, h: (0, h)),
        ],
        out_specs=pl.BlockSpec((seq, M_DV), lambda b, h: (b, h)),
        scratch_shapes=[pltpu.VMEM((seq, M_DQK), BF16), pltpu.VMEM((seq, M_DQK), F32),
                        pltpu.VMEM((seq, M_DV + LANES), BF16), pltpu.VMEM((GATE_ROWS, seq), F32),
                        pltpu.VMEM((GATE_ROWS, seq), F32), pltpu.VMEM((LANES, seq), F32)],
        compiler_params=_params(("parallel", "parallel"), vmem),
        name="mlstm",
    )(z, z, z, z, gates, gate_bias, conv_w, conv_b, conv_w, conv_b, norm)


def _t5_bucket_np(n):
    nf = np.maximum(n, 1).astype(np.float64)
    val = np.log(nf / MAX_EXACT) / math.log(MAX_DISTANCE / MAX_EXACT) * (NUM_BUCKETS - MAX_EXACT)
    frac = val[(n > MAX_EXACT) & (n < MAX_DISTANCE)]
    assert np.all(np.abs(frac - np.round(frac)) > 1e-3)
    large = np.minimum(MAX_EXACT + val.astype(np.int64), NUM_BUCKETS - 1)
    return np.where(n < MAX_EXACT, n, large)


def _bucket_tile(blk):
    rel = np.arange(blk)[None, :] + blk - np.arange(2 * blk)[:, None]
    return np.where(rel >= 0, _t5_bucket_np(np.maximum(rel, 0)), -1).astype(np.int32)


def _bias_body(rb_ref, bucket_ref, t_ref):
    h = pl.program_id(0)
    bucket = bucket_ref[...]
    acc = jnp.zeros(bucket.shape, F32)
    for b in range(NUM_BUCKETS):
        acc = jnp.where(bucket == b, rb_ref[h * NUM_BUCKETS + b], acc)
    far = rb_ref[h * NUM_BUCKETS + NUM_BUCKETS - 1]
    t_ref[0] = jnp.where(bucket < 0, -jnp.inf, (acc - far) * LOG2E)


def _bias_tiles(rel_bias, blk):
    assert blk >= MAX_DISTANCE
    bucket = jnp.asarray(_bucket_tile(blk))
    return pl.pallas_call(
        _bias_body,
        out_shape=jax.ShapeDtypeStruct((DA_HEADS, 2 * blk, blk), F32),
        grid=(DA_HEADS,),
        in_specs=[pl.BlockSpec(memory_space=pltpu.SMEM),
                  pl.BlockSpec((2 * blk, blk), lambda h: (0, 0))],
        out_specs=pl.BlockSpec((1, 2 * blk, blk), lambda h: (h, 0, 0)),
        compiler_params=_params(("arbitrary",), 16 << 20),
        name="t5_bias_tiles",
    )(rel_bias.T.reshape(-1), bucket)


VT_ROWS = DA_DV + 16


def _diffattn_body(lam_ref, q_ref, k_ref, v_ref, t_ref, g_ref, o_ref, vt_ref, *, blk, lam_init):
    nb = q_ref.shape[0] // blk
    ones_rows = (lax.broadcasted_iota(jnp.int32, (VT_ROWS - DA_DV, blk), 0) == 0).astype(F32)
    for c in range(nb):
        vt = v_ref[c * blk:(c + 1) * blk, :].astype(F32).T
        vt_ref[c] = jnp.concatenate([vt, ones_rows], axis=0).astype(BF16)

    lp = lam_ref[...]
    lam = (jnp.exp(jnp.sum(lp[0:1] * lp[1:2], axis=-1, keepdims=True))
           - jnp.exp(jnp.sum(lp[2:3] * lp[3:4], axis=-1, keepdims=True)) + lam_init)
    sub = lax.broadcasted_iota(jnp.int32, (LANES, blk), 0)
    gain = g_ref[...]

    def query_maps(qi):
        qt = q_ref[qi * blk:(qi + 1) * blk, :].astype(F32).T
        return (jnp.where(sub < DA_D, qt, 0.0).astype(BF16), jnp.where(sub >= DA_D, qt, 0.0).astype(BF16))

    def scores(qts, qi, kc):
        kb = k_ref[kc * blk:(kc + 1) * blk, :]
        s = [_dot(kb, qt_m) for qt_m in qts]
        if kc >= qi - 1:
            bias = t_ref[0, blk:, :] if kc == qi else t_ref[0, :blk, :]
            s = [s_m + bias for s_m in s]
        return s

    units = [(qi, kc) for qi in range(nb) for kc in range(qi, -1, -1)]
    qts_of = {}

    def unit_scores(u):
        qi, kc = units[u]
        if qi not in qts_of:
            qts_of.clear()
            qts_of[qi] = query_maps(qi)
        return scores(qts_of[qi], qi, kc)

    ahead = [unit_scores(u) for u in range(min(DA_LOOKAHEAD, len(units)))]
    state = [None, None]
    for u, (qi, kc) in enumerate(units):
        s_cur = ahead.pop(0)
        if u + DA_LOOKAHEAD < len(units):
            ahead.append(unit_scores(u + DA_LOOKAHEAD))
        for a in range(2):
            s = s_cur[a]
            blk_max = jnp.max(s, axis=0, keepdims=True)
            if state[a] is None:
                m_new = blk_max
                acc = _dot(vt_ref[kc], jnp.exp2(s - m_new).astype(BF16))
            else:
                m, acc = state[a]
                m_new = jnp.maximum(m, blk_max)
                acc = acc * jnp.exp2(m - m_new) + _dot(vt_ref[kc], jnp.exp2(s - m_new).astype(BF16))
            state[a] = (m_new, acc)
        if kc == 0:
            outs = [acc[:DA_DV] / acc[DA_DV:DA_DV + 1] for _, acc in state]
            o = (outs[0] - lam * outs[1]).T
            o_ref[qi * blk:(qi + 1) * blk, :] = (_rmsnorm(o, gain) * (1.0 - lam_init)).astype(o_ref.dtype)
            state = [None, None]


def _diffattn(z, tiles, lam_p, subln, *, batch, seq, lam_init, col0):
    blk = tiles.shape[2]
    q0 = col0 // LANES
    k0 = q0 + DA_HEADS
    v0 = k0 + DA_HEADS
    vmem = 8 * seq * LANES * 2 + seq * VT_ROWS * 2 + 4 * blk * blk * 4 + 24 * blk * blk * 4 + (8 << 20)
    return pl.pallas_call(
        functools.partial(_diffattn_body, blk=blk, lam_init=lam_init),
        out_shape=jax.ShapeDtypeStruct((batch * seq, DA_HEADS * DA_DV), BF16),
        grid=(batch, DA_HEADS),
        in_specs=[
            pl.BlockSpec((4, DA_D), lambda b, h: (0, 0)),
            pl.BlockSpec((seq, LANES), lambda b, h: (b, q0 + h)),
            pl.BlockSpec((seq, LANES), lambda b, h: (b, k0 + h)),
            pl.BlockSpec((seq, LANES), lambda b, h: (b, v0 + h)),
            pl.BlockSpec((1, 2 * blk, blk), lambda b, h: (h, 0, 0)),
            pl.BlockSpec((1, DA_DV), lambda b, h: (0, 0)),
        ],
        out_specs=pl.BlockSpec((seq, DA_DV), lambda b, h: (b, h)),
        scratch_shapes=[pltpu.VMEM((seq // blk, VT_ROWS, blk), BF16)],
        compiler_params=_params(("parallel", "parallel"), vmem),
        name="diffattn",
    )(lam_p, z, z, z, tiles, subln)


def _xattn_body(q_ref, k_ref, v_ref, o_ref, *, blk):
    nb = q_ref.shape[0] // blk
    kb = k_ref[...]
    vb = v_ref[...]
    s_next = _dot_nt(q_ref[0:blk, :], kb)
    for j in range(nb):
        s = s_next
        if j + 1 < nb:
            s_next = _dot_nt(q_ref[(j + 1) * blk:(j + 2) * blk, :], kb)
        p = jnp.exp2(s - jnp.max(s, axis=-1, keepdims=True))
        o = _dot(p.astype(BF16), vb) / jnp.sum(p, axis=-1, keepdims=True)
        o_ref[j * blk:(j + 1) * blk, :] = o.astype(o_ref.dtype)


def _xattn(q, kv, *, batch, seq, mem_len, blk=256):
    d = q.shape[1]
    dh = d // X_HEADS
    blk = min(blk, seq)
    vmem = 4 * seq * dh * 2 + 4 * mem_len * dh * 2 + 8 * blk * (mem_len + dh) * 4 + (8 << 20)
    return pl.pallas_call(
        functools.partial(_xattn_body, blk=blk),
        out_shape=jax.ShapeDtypeStruct((batch * seq, d), BF16),
        grid=(batch, X_HEADS),
        in_specs=[
            pl.BlockSpec((seq, dh), lambda b, h: (b, h)),
            pl.BlockSpec((mem_len, dh), lambda b, h: (b, h)),
            pl.BlockSpec((mem_len, dh), lambda b, h: (b, X_HEADS + h)),
        ],
        out_specs=pl.BlockSpec((seq, dh), lambda b, h: (b, h)),
        compiler_params=_params(("parallel", "parallel"), vmem),
        name="xattn",
    )(q, kv, kv)


def kernel(x, mem, rel_bias, ffn1_norm_pre, ffn1_norm_post, ffn1_w_gate, ffn1_w_up, ffn1_w_down, mix_norm_pre, mix_norm_post, w_in, conv_w, conv_b, b_igate, b_fgate, mlstm_norm, diff_lambda, diff_subln, w_out, xattn_norm_pre, xattn_norm_post, mem_norm, xattn_wq, xattn_wk, xattn_wv, xattn_wo, ffn2_norm_pre, ffn2_norm_post, ffn2_w_gate, ffn2_w_up, ffn2_w_down):
    batch, seq, d = x.shape
    mem_len = mem.shape[1]
    depth = w_in.shape[0]
    n_gate = 2 * M_HEADS
    da_qw = DA_HEADS * 2 * DA_D
    gate0 = 2 * M_HEADS * M_DQK + 2 * M_HEADS * M_DV
    xf = x.reshape(batch * seq, d)
    memf = mem.reshape(batch * mem_len, d)
    row = lambda v: v.reshape(1, -1).astype(F32)

    tiles = _bias_tiles(rel_bias, min(DA_BLOCK, seq))
    w_main, w_gate = _cast_w_in(w_in, gate0=gate0, n_gate=n_gate, scaled=da_qw, scale=DA_D ** -0.5 * LOG2E)
    ffn1 = (_cast_blocks([ffn1_w_gate], FFN_BLOCK), _cast_blocks([ffn1_w_up], FFN_BLOCK), _cast_blocks([ffn1_w_down]))
    ffn2 = (_cast_blocks([ffn2_w_gate], FFN_BLOCK), _cast_blocks([ffn2_w_up], FFN_BLOCK), _cast_blocks([ffn2_w_down]))
    w_o = _cast_blocks([w_out])
    w_q = _cast_blocks([xattn_wq], PROJ_BLOCK, scale=(d // X_HEADS) ** -0.5 * LOG2E)
    w_kv = _cast_blocks([xattn_wk, xattn_wv], PROJ_BLOCK)
    w_xo = _cast_blocks([xattn_wo])

    for l in range(depth):
        lam_init = 0.8 - 0.6 * math.exp(-0.3 * l)
        xf = _ffn(xf, row(ffn1_norm_pre[l]), row(ffn1_norm_post[l]), *ffn1, l)

        z, gates = _normproj(xf, row(mix_norm_pre[l]), w_main, l, w_gate)
        gate_bias = jnp.pad(jnp.concatenate([b_igate[l], b_fgate[l]]), (0, LANES - n_gate)).reshape(1, LANES)
        y_m = _mlstm(z, gates, gate_bias.astype(F32), conv_w[l].astype(F32), row(conv_b[l]),
                     row(mlstm_norm[l]), batch=batch, seq=seq)
        y_d = _diffattn(z, tiles, diff_lambda[l].astype(F32), row(diff_subln[l]),
                        batch=batch, seq=seq, lam_init=lam_init, col0=gate0)
        xf = _outproj(xf, row(mix_norm_post[l]), [y_m, y_d], w_o, l)

        q = _normproj(xf, row(xattn_norm_pre[l]), w_q, l)
        kv = _normproj(memf, row(mem_norm[l]), w_kv, l)
        c = _xattn(q, kv, batch=batch, seq=seq, mem_len=mem_len)
        xf = _outproj(xf, row(xattn_norm_post[l]), [c], w_xo, l)

        xf = _ffn(xf, row(ffn2_norm_pre[l]), row(ffn2_norm_post[l]), *ffn2, l)
    return xf.reshape(batch, seq, d)
```

```python
import functools
import math

import numpy as np
import jax
import jax.numpy as jnp
from jax import lax
from jax.experimental import pallas as pl
from jax.experimental.pallas import tpu as pltpu

F32 = jnp.float32
BF16 = jnp.bfloat16
EPS = 1e-6
LOG2E = math.log2(math.e)

M_HEADS = 4
M_DQK = 128
M_DV = 256
CONV_K = 4
DA_HEADS = 8
DA_D = 64
DA_DV = 128
X_HEADS = 4
NUM_BUCKETS = 32
MAX_EXACT = NUM_BUCKETS // 2
MAX_DISTANCE = 128

LANES = 128
MXU_N = 256
V7X_VMEM_BYTES = 64 * 2 ** 20
MLSTM_CHUNK = 256
DA_BLOCK = 256
DA_LOOKAHEAD = 2
FFN_BLOCK = 512
PROJ_BLOCK = 1024
INPROJ_BLOCK = 1536


def _params(semantics, vmem_bytes):
    assert vmem_bytes <= V7X_VMEM_BYTES
    return pltpu.CompilerParams(dimension_semantics=semantics, vmem_limit_bytes=int(vmem_bytes))


def _rmsnorm(xf, g):
    return xf * lax.rsqrt(jnp.mean(xf * xf, axis=-1, keepdims=True) + EPS) * g


def _log_sigmoid(x):
    return jnp.minimum(x, 0.0) - jnp.log1p(jnp.exp(-jnp.abs(x)))


def _dot(a, b):
    return jnp.dot(a, b, preferred_element_type=F32)


def _dot_nt(a, b):
    return lax.dot_general(a, b, (((1,), (1,)), ((), ())), preferred_element_type=F32)


def _dot_tn(a, b):
    return lax.dot_general(a, b, (((0,), (0,)), ((), ())), preferred_element_type=F32)


def _cast_body(*refs, bn, scale, interleave):
    *in_refs, o_ref = refs
    cast = lambda blk: (blk if scale is None else blk * scale).astype(BF16)
    if interleave:
        for b in range(in_refs[0].shape[2] // bn):
            for t, x_ref in enumerate(in_refs):
                o_ref[0, b, :, t * bn:(t + 1) * bn] = cast(x_ref[0, :, b * bn:(b + 1) * bn])
    else:
        c = 0
        for x_ref in in_refs:
            for b in range(x_ref.shape[2] // bn):
                o_ref[0, c] = cast(x_ref[0, :, b * bn:(b + 1) * bn])
                c += 1


def _cast_blocks(ws, bn=None, *, scale=None, interleave=False, kb=256):
    depth, k, _ = ws[0].shape
    bn = bn or ws[0].shape[2]
    n_total = sum(w.shape[2] for w in ws)
    width = bn * len(ws) if interleave else bn
    kb = min(kb, k)
    vmem = 2 * kb * n_total * (4 + 2) + (4 << 20)
    return pl.pallas_call(
        functools.partial(_cast_body, bn=bn, scale=scale, interleave=interleave),
        out_shape=jax.ShapeDtypeStruct((depth, n_total // width, k, width), BF16),
        grid=(depth, k // kb),
        in_specs=[pl.BlockSpec((1, kb, w.shape[2]), lambda l, i: (l, i, 0)) for w in ws],
        out_specs=pl.BlockSpec((1, n_total // width, kb, width), lambda l, i: (l, 0, i, 0)),
        compiler_params=_params(("parallel", "parallel"), vmem),
        name="cast_blocks",
    )(*ws)


def _cast_w_in_body(x_ref, o_ref, gate_ref, *, gate0, n_gate, bn, scaled, scale):
    x = x_ref[0]
    left = x[:, :gate0]
    right = x[:, gate0 + n_gate:]
    right = jnp.concatenate([right[:, :scaled] * scale, right[:, scaled:]], axis=1)
    c = 0
    for part in (left, right):
        for b in range(part.shape[1] // bn):
            o_ref[0, c] = part[:, b * bn:(b + 1) * bn].astype(BF16)
            c += 1
    group = x[:, gate0:gate0 + LANES]
    lane = lax.broadcasted_iota(jnp.int32, group.shape, 1)
    gate_ref[0] = jnp.where(lane < n_gate, group, 0.0).astype(BF16)


def _cast_w_in(w_in, *, gate0, n_gate, scaled, scale, bn=INPROJ_BLOCK, kb=256):
    depth, k, n = w_in.shape
    n_main = n - n_gate
    kb = min(kb, k)
    vmem = 2 * kb * n * 4 + 2 * kb * n_main * 2 + 3 * kb * n * 4 + (4 << 20)
    return pl.pallas_call(
        functools.partial(_cast_w_in_body, gate0=gate0, n_gate=n_gate, bn=bn, scaled=scaled, scale=scale),
        out_shape=[jax.ShapeDtypeStruct((depth, n_main // bn, k, bn), BF16),
                   jax.ShapeDtypeStruct((depth, k, LANES), BF16)],
        grid=(depth, k // kb),
        in_specs=[pl.BlockSpec((1, kb, n), lambda l, i: (l, i, 0))],
        out_specs=[pl.BlockSpec((1, n_main // bn, kb, bn), lambda l, i: (l, 0, i, 0)),
                   pl.BlockSpec((1, kb, LANES), lambda l, i: (l, i, 0))],
        compiler_params=_params(("parallel", "parallel"), vmem),
        name="cast_w_in",
    )(w_in)


def _ffn_up_body(x_ref, g_ref, w_ref, h_ref, xn_ref):
    @pl.when(pl.program_id(1) == 0)
    def _():
        xn_ref[...] = _rmsnorm(x_ref[...], g_ref[...]).astype(BF16)

    bf = h_ref.shape[1]
    gu = _dot(xn_ref[...], w_ref[0, 0])
    g, u = gu[:, :bf], gu[:, bf:]
    h_ref[...] = (g * jax.nn.sigmoid(g) * u).astype(h_ref.dtype)


def _ffn(x, g_pre, g_post, w_gate_up, w_down, layer, *, bm=1024, bm_down=512):
    m, d = x.shape
    nf, bf = w_gate_up.shape[1], w_gate_up.shape[3] // 2
    f = nf * bf
    bm = min(bm, m)
    assert m % bm == 0
    vmem_up = 2 * bm * d * 4 + bm * d * 2 + 4 * d * bf * 2 + 2 * bm * bf * 2 + 3 * bm * bf * 4 + (4 << 20)
    hidden = pl.pallas_call(
        _ffn_up_body,
        out_shape=jax.ShapeDtypeStruct((m, f), BF16),
        grid=(m // bm, nf),
        in_specs=[
            pl.BlockSpec((bm, d), lambda i, j: (i, 0)),
            pl.BlockSpec((1, d), lambda i, j: (0, 0)),
            pl.BlockSpec((1, 1, d, 2 * bf), lambda i, j: (layer, j, 0, 0)),
        ],
        out_specs=pl.BlockSpec((bm, bf), lambda i, j: (i, j)),
        scratch_shapes=[pltpu.VMEM((bm, d), BF16)],
        compiler_params=_params(("parallel", "arbitrary"), vmem_up),
        name="ffn_up",
    )(x, g_pre, w_gate_up)
    return _outproj(x, g_post, [hidden], w_down, layer, res_scale=0.5, bm=bm_down)


def _normproj_body(x_ref, g_ref, w_ref, *rest, with_gate):
    if with_gate:
        wgate_ref, o_ref, gate_ref, xn_ref = rest
    else:
        o_ref, xn_ref = rest
    j = pl.program_id(1)

    @pl.when(j == 0)
    def _():
        xn = _rmsnorm(x_ref[...], g_ref[...]).astype(BF16)
        xn_ref[...] = xn
        if with_gate:
            gate_ref[...] = _dot(xn, wgate_ref[0])

    o_ref[...] = _dot(xn_ref[...], w_ref[0, 0]).astype(o_ref.dtype)


def _normproj(x, g, w, layer, w_gate=None, *, bm=1024):
    m, d = x.shape
    nb, bn = w.shape[1], w.shape[3]
    bm = min(bm, m)
    assert m % bm == 0
    with_gate = w_gate is not None
    in_specs = [
        pl.BlockSpec((bm, d), lambda i, j: (i, 0)),
        pl.BlockSpec((1, d), lambda i, j: (0, 0)),
        pl.BlockSpec((1, 1, d, bn), lambda i, j: (layer, j, 0, 0)),
    ]
    out_shape = [jax.ShapeDtypeStruct((m, nb * bn), BF16)]
    out_specs = [pl.BlockSpec((bm, bn), lambda i, j: (i, j))]
    args = [x, g, w]
    if with_gate:
        in_specs.append(pl.BlockSpec((1, d, LANES), lambda i, j: (layer, 0, 0)))
        out_shape.append(jax.ShapeDtypeStruct((m, LANES), F32))
        out_specs.append(pl.BlockSpec((bm, LANES), lambda i, j: (i, 0)))
        args.append(w_gate)
    vmem = 2 * bm * d * 4 + bm * d * 2 + 2 * d * bn * 2 + 2 * bm * bn * 2 + bm * bn * 4 + (6 << 20)
    out = pl.pallas_call(
        functools.partial(_normproj_body, with_gate=with_gate),
        out_shape=out_shape,
        grid=(m // bm, nb),
        in_specs=in_specs,
        out_specs=out_specs,
        scratch_shapes=[pltpu.VMEM((bm, d), BF16)],
        compiler_params=_params(("parallel", "arbitrary"), vmem),
        name="normproj_gate" if with_gate else "normproj",
    )(*args)
    return out if with_gate else out[0]


def _outproj_body(*refs, n_parts, res_scale):
    a_refs = refs[:n_parts]
    w_refs = refs[n_parts:2 * n_parts]
    x_ref, g_ref, o_ref = refs[2 * n_parts:]
    y = _dot(a_refs[0][...], w_refs[0][0, 0])
    for a_ref, w_ref in zip(a_refs[1:], w_refs[1:]):
        y += _dot(a_ref[...], w_ref[0, 0])
    o_ref[...] = x_ref[...] + res_scale * _rmsnorm(y, g_ref[...])


def _outproj(x, g, acts, w, layer, *, res_scale=1.0, bm=512):
    m, d = x.shape
    bm = min(bm, m)
    assert m % bm == 0
    k_total = w.shape[2]
    in_specs = [pl.BlockSpec((bm, a.shape[1]), lambda i: (i, 0)) for a in acts]
    row0 = 0
    for a in acts:
        k = a.shape[1]
        assert row0 % k == 0
        in_specs.append(pl.BlockSpec((1, 1, k, d), functools.partial(lambda i, rb: (layer, 0, rb, 0), rb=row0 // k),
                                     pipeline_mode=pl.Buffered(1)))
        row0 += k
    assert row0 == k_total
    in_specs += [pl.BlockSpec((bm, d), lambda i: (i, 0)), pl.BlockSpec((1, d), lambda i: (0, 0))]
    vmem = k_total * d * 2 + 2 * bm * k_total * 2 + 5 * bm * d * 4 + (4 << 20)
    return pl.pallas_call(
        functools.partial(_outproj_body, n_parts=len(acts), res_scale=res_scale),
        out_shape=jax.ShapeDtypeStruct((m, d), F32),
        grid=(m // bm,),
        in_specs=in_specs,
        out_specs=pl.BlockSpec((bm, d), lambda i: (i, 0)),
        compiler_params=_params(("parallel",), vmem),
        name="outproj",
    )(*acts, *([w] * len(acts)), x, g)


def _conv_silu(raw, w, b):
    rows = lax.broadcasted_iota(jnp.int32, raw.shape, 0)
    acc = raw * w[CONV_K - 1:CONV_K, :] + b
    for shift in range(1, CONV_K):
        shifted = jnp.where(rows >= shift, pltpu.roll(raw, shift, 0), 0.0)
        acc += shifted * w[CONV_K - 1 - shift:CONV_K - shift, :]
    return acc * jax.nn.sigmoid(acc)


GATE_ROWS = 16


def _split3(x):
    hi = x.astype(BF16)
    r1 = x - hi.astype(F32)
    mid = r1.astype(BF16)
    return hi, mid, (r1 - mid.astype(F32)).astype(BF16)


def _mlstm_body(q_ref, k_ref, v_ref, og_ref, gates_ref, gbias_ref, cwq_ref, cbq_ref, cwk_ref, cbk_ref,
                norm_ref, o_ref, qs_ref, ks_ref, vx_ref, gt_ref, bt_ref, row_ref, *, chunk):
    s = q_ref.shape[0]
    nc = s // chunk
    head = pl.program_id(1)
    ext = vx_ref.shape[1]
    qs_ref[...] = _conv_silu(q_ref[...].astype(F32), cwq_ref[...], cbq_ref[...]).astype(BF16)
    ks_ref[...] = _conv_silu(k_ref[...].astype(F32), cwk_ref[...], cbk_ref[...]) * (M_DQK ** -0.5)
    ones_col = (lax.broadcasted_iota(jnp.int32, (s, ext - M_DV), 1) == 0).astype(BF16)
    vx_ref[...] = jnp.concatenate([v_ref[...], ones_col], axis=1)

    upper = (lax.broadcasted_iota(jnp.int32, (chunk, chunk), 0)
             <= lax.broadcasted_iota(jnp.int32, (chunk, chunk), 1)).astype(BF16)
    for c in range(nc):
        cols = slice(c * chunk, (c + 1) * chunk)
        gt = (gates_ref[cols, :] + gbias_ref[...]).T[:GATE_ROWS]
        gt_ref[:, cols] = gt
        bt_ref[:, cols] = sum(_dot(piece, upper) for piece in _split3(_log_sigmoid(gt)))

    i_row = gt_ref[pl.ds(head, 1), :]
    b_row = bt_ref[pl.ds(M_HEADS + head, 1), :]
    r_row = i_row - b_row
    pos = lax.broadcasted_iota(jnp.int32, (1, s), 1) % chunk
    cmax = r_row
    shift = 1
    while shift < chunk:
        cmax = jnp.maximum(cmax, jnp.where(pos >= shift, pltpu.roll(cmax, shift, 1), -jnp.inf))
        shift *= 2

    row_ref[...] = jnp.zeros(row_ref.shape, F32)
    m = jnp.zeros((1, 1), F32)
    ws_rows, decays = [], []
    for c in range(nc):
        cols = slice(c * chunk, (c + 1) * chunk)
        a = jnp.maximum(m, cmax[:, cols])
        a_last = a[:, chunk - 1:]
        b_c = b_row[:, cols]
        row_ref[0:1, cols] = a
        row_ref[1:2, cols] = jnp.exp(m - a)
        row_ref[2:3, cols] = jnp.exp(-(b_c + a))
        ws_rows.append(jnp.exp(r_row[:, cols] - a_last))
        decays.append(jnp.exp(m - a_last))
        m = b_c[:, chunk - 1:] + a_last

    causal = (lax.broadcasted_iota(jnp.int32, (chunk, chunk), 0)
              >= lax.broadcasted_iota(jnp.int32, (chunk, chunk), 1))
    gain = norm_ref[...]

    def independent(c):
        rows = slice(c * chunk, (c + 1) * chunk)
        kf = ks_ref[rows, :]
        s_qk = _dot_nt(qs_ref[rows, :], kf.astype(BF16))
        kv = _dot((kf.T * ws_rows[c]).astype(BF16), vx_ref[rows, :])
        return s_qk, kv

    cmat = None
    nxt = independent(0)
    for c in range(nc):
        rows = slice(c * chunk, (c + 1) * chunk)
        s_qk, kv = nxt
        if c + 1 < nc:
            nxt = independent(c + 1)
        col = row_ref[:, rows].T
        a_col, wi_col, en_col = col[:, 0:1], col[:, 1:2], col[:, 2:3]
        p = jnp.exp(jnp.where(causal, r_row[:, rows] - a_col, -jnp.inf)) * s_qk
        hx = _dot(p.astype(BF16), vx_ref[rows, :])
        if cmat is not None:
            hx += wi_col * _dot(qs_ref[rows, :], cmat.astype(BF16))
        hc = hx[:, :M_DV] / jnp.maximum(jnp.abs(hx[:, M_DV:M_DV + 1]), en_col)
        cmat = kv if cmat is None else decays[c] * cmat + kv
        y = _rmsnorm(hc, gain) * jax.nn.sigmoid(og_ref[rows, :].astype(F32))
        o_ref[rows, :] = y.astype(o_ref.dtype)


def _mlstm(z, gates, gate_bias, conv_w, conv_b, norm, *, batch, seq):
    chunk = min(MLSTM_CHUNK, seq)
    qk_blocks = M_HEADS
    v_block0 = 2 * M_HEADS * M_DQK // M_DV
    og_block0 = v_block0 + M_HEADS
    vmem = seq * (2 * 2 * M_DQK * 2 + 3 * 2 * M_DV * 2 + 3 * LANES * 4 + M_DQK * 6) + (24 << 20)
    return pl.pallas_call(
        functools.partial(_mlstm_body, chunk=chunk),
        out_shape=jax.ShapeDtypeStruct((batch * seq, M_HEADS * M_DV), BF16),
        grid=(batch, M_HEADS),
        in_specs=[
            pl.BlockSpec((seq, M_DQK), lambda b, h: (b, h)),
            pl.BlockSpec((seq, M_DQK), lambda b, h: (b, qk_blocks + h)),
            pl.BlockSpec((seq, M_DV), lambda b, h: (b, v_block0 + h)),
            pl.BlockSpec((seq, M_DV), lambda b, h: (b, og_block0 + h)),
            pl.BlockSpec((seq, LANES), lambda b, h: (b, 0)),
            pl.BlockSpec((1, LANES), lambda b, h: (0, 0)),
            pl.BlockSpec((CONV_K, M_DQK), lambda b, h: (0, h)),
            pl.BlockSpec((1, M_DQK), lambda b, h: (0, h)),
            pl.BlockSpec((CONV_K, M_DQK), lambda b, h: (0, qk_blocks + h)),
            pl.BlockSpec((1, M_DQK), lambda b, h: (0, qk_blocks + h)),
            pl.BlockSpec((1, M_DV), lambda b, h: (0, h)),
        ],
        out_specs=pl.BlockSpec((seq, M_DV), lambda b, h: (b, h)),
        scratch_shapes=[pltpu.VMEM((seq, M_DQK), BF16), pltpu.VMEM((seq, M_DQK), F32),
                        pltpu.VMEM((seq, M_DV + LANES), BF16), pltpu.VMEM((GATE_ROWS, seq), F32),
                        pltpu.VMEM((GATE_ROWS, seq), F32), pltpu.VMEM((LANES, seq), F32)],
        compiler_params=_params(("parallel", "parallel"), vmem),
        name="mlstm",
    )(z, z, z, z, gates, gate_bias, conv_w, conv_b, conv_w, conv_b, norm)


def _t5_bucket_np(n):
    nf = np.maximum(n, 1).astype(np.float64)
    val = np.log(nf / MAX_EXACT) / math.log(MAX_DISTANCE / MAX_EXACT) * (NUM_BUCKETS - MAX_EXACT)
    frac = val[(n > MAX_EXACT) & (n < MAX_DISTANCE)]
    assert np.all(np.abs(frac - np.round(frac)) > 1e-3)
    large = np.minimum(MAX_EXACT + val.astype(np.int64), NUM_BUCKETS - 1)
    return np.where(n < MAX_EXACT, n, large)


def _bucket_tile(blk):
    rel = np.arange(blk)[None, :] + blk - np.arange(2 * blk)[:, None]
    return np.where(rel >= 0, _t5_bucket_np(np.maximum(rel, 0)), -1).astype(np.int32)


def _bias_body(rb_ref, bucket_ref, t_ref):
    h = pl.program_id(0)
    bucket = bucket_ref[...]
    acc = jnp.zeros(bucket.shape, F32)
    for b in range(NUM_BUCKETS):
        acc = jnp.where(bucket == b, rb_ref[h * NUM_BUCKETS + b], acc)
    far = rb_ref[h * NUM_BUCKETS + NUM_BUCKETS - 1]
    t_ref[0] = jnp.where(bucket < 0, -jnp.inf, (acc - far) * LOG2E)


def _bias_tiles(rel_bias, blk):
    assert blk >= MAX_DISTANCE
    bucket = jnp.asarray(_bucket_tile(blk))
    return pl.pallas_call(
        _bias_body,
        out_shape=jax.ShapeDtypeStruct((DA_HEADS, 2 * blk, blk), F32),
        grid=(DA_HEADS,),
        in_specs=[pl.BlockSpec(memory_space=pltpu.SMEM),
                  pl.BlockSpec((2 * blk, blk), lambda h: (0, 0))],
        out_specs=pl.BlockSpec((1, 2 * blk, blk), lambda h: (h, 0, 0)),
        compiler_params=_params(("arbitrary",), 16 << 20),
        name="t5_bias_tiles",
    )(rel_bias.T.reshape(-1), bucket)


VT_ROWS = DA_DV + 16


def _diffattn_body(lam_ref, q_ref, k_ref, v_ref, t_ref, g_ref, o_ref, vt_ref, *, blk, lam_init):
    nb = q_ref.shape[0] // blk
    ones_rows = (lax.broadcasted_iota(jnp.int32, (VT_ROWS - DA_DV, blk), 0) == 0).astype(F32)
    for c in range(nb):
        vt = v_ref[c * blk:(c + 1) * blk, :].astype(F32).T
        vt_ref[c] = jnp.concatenate([vt, ones_rows], axis=0).astype(BF16)

    lp = lam_ref[...]
    lam = (jnp.exp(jnp.sum(lp[0:1] * lp[1:2], axis=-1, keepdims=True))
           - jnp.exp(jnp.sum(lp[2:3] * lp[3:4], axis=-1, keepdims=True)) + lam_init)
    sub = lax.broadcasted_iota(jnp.int32, (LANES, blk), 0)
    gain = g_ref[...]

    def query_maps(qi):
        qt = q_ref[qi * blk:(qi + 1) * blk, :].astype(F32).T
        return (jnp.where(sub < DA_D, qt, 0.0).astype(BF16), jnp.where(sub >= DA_D, qt, 0.0).astype(BF16))

    def scores(qts, qi, kc):
        kb = k_ref[kc * blk:(kc + 1) * blk, :]
        s = [_dot(kb, qt_m) for qt_m in qts]
        if kc >= qi - 1:
            bias = t_ref[0, blk:, :] if kc == qi else t_ref[0, :blk, :]
            s = [s_m + bias for s_m in s]
        return s

    units = [(qi, kc) for qi in range(nb) for kc in range(qi, -1, -1)]
    qts_of = {}

    def unit_scores(u):
        qi, kc = units[u]
        if qi not in qts_of:
            qts_of.clear()
            qts_of[qi] = query_maps(qi)
        return scores(qts_of[qi], qi, kc)

    ahead = [unit_scores(u) for u in range(min(DA_LOOKAHEAD, len(units)))]
    state = [None, None]
    for u, (qi, kc) in enumerate(units):
        s_cur = ahead.pop(0)
        if u + DA_LOOKAHEAD < len(units):
            ahead.append(unit_scores(u + DA_LOOKAHEAD))
        for a in range(2):
            s = s_cur[a]
            blk_max = jnp.max(s, axis=0, keepdims=True)
            if state[a] is None:
                m_new = blk_max
                acc = _dot(vt_ref[kc], jnp.exp2(s - m_new).astype(BF16))
            else:
                m, acc = state[a]
                m_new = jnp.maximum(m, blk_max)
                acc = acc * jnp.exp2(m - m_new) + _dot(vt_ref[kc], jnp.exp2(s - m_new).astype(BF16))
            state[a] = (m_new, acc)
        if kc == 0:
            outs = [acc[:DA_DV] / acc[DA_DV:DA_DV + 1] for _, acc in state]
            o = (outs[0] - lam * outs[1]).T
            o_ref[qi * blk:(qi + 1) * blk, :] = (_rmsnorm(o, gain) * (1.0 - lam_init)).astype(o_ref.dtype)
            state = [None, None]


def _diffattn(z, tiles, lam_p, subln, *, batch, seq, lam_init, col0):
    blk = tiles.shape[2]
    q0 = col0 // LANES
    k0 = q0 + DA_HEADS
    v0 = k0 + DA_HEADS
    vmem = 8 * seq * LANES * 2 + seq * VT_ROWS * 2 + 4 * blk * blk * 4 + 24 * blk * blk * 4 + (8 << 20)
    return pl.pallas_call(
        functools.partial(_diffattn_body, blk=blk, lam_init=lam_init),
        out_shape=jax.ShapeDtypeStruct((batch * seq, DA_HEADS * DA_DV), BF16),
        grid=(batch, DA_HEADS),
        in_specs=[
            pl.BlockSpec((4, DA_D), lambda b, h: (0, 0)),
            pl.BlockSpec((seq, LANES), lambda b, h: (b, q0 + h)),
            pl.BlockSpec((seq, LANES), lambda b, h: (b, k0 + h)),
            pl.BlockSpec((seq, LANES), lambda b, h: (b, v0 + h)),
            pl.BlockSpec((1, 2 * blk, blk), lambda b, h: (h, 0, 0)),
            pl.BlockSpec((1, DA_DV), lambda b, h: (0, 0)),
        ],
        out_specs=pl.BlockSpec((seq, DA_DV), lambda b, h: (b, h)),
        scratch_shapes=[pltpu.VMEM((seq // blk, VT_ROWS, blk), BF16)],
        compiler_params=_params(("parallel", "parallel"), vmem),
        name="diffattn",
    )(lam_p, z, z, z, tiles, subln)


def _xattn_body(q_ref, k_ref, v_ref, o_ref, *, blk):
    nb = q_ref.shape[0] // blk
    kb = k_ref[...]
    vb = v_ref[...]
    s_next = _dot_nt(q_ref[0:blk, :], kb)
    for j in range(nb):
        s = s_next
        if j + 1 < nb:
            s_next = _dot_nt(q_ref[(j + 1) * blk:(j + 2) * blk, :], kb)
        p = jnp.exp2(s - jnp.max(s, axis=-1, keepdims=True))
        o = _dot(p.astype(BF16), vb) / jnp.sum(p, axis=-1, keepdims=True)
        o_ref[j * blk:(j + 1) * blk, :] = o.astype(o_ref.dtype)


def _xattn(q, kv, *, batch, seq, mem_len, blk=256):
    d = q.shape[1]
    dh = d // X_HEADS
    blk = min(blk, seq)
    vmem = 4 * seq * dh * 2 + 4 * mem_len * dh * 2 + 8 * blk * (mem_len + dh) * 4 + (8 << 20)
    return pl.pallas_call(
        functools.partial(_xattn_body, blk=blk),
        out_shape=jax.ShapeDtypeStruct((batch * seq, d), BF16),
        grid=(batch, X_HEADS),
        in_specs=[
            pl.BlockSpec((seq, dh), lambda b, h: (b, h)),
            pl.BlockSpec((mem_len, dh), lambda b, h: (b, h)),
            pl.BlockSpec((mem_len, dh), lambda b, h: (b, X_HEADS + h)),
        ],
        out_specs=pl.BlockSpec((seq, dh), lambda b, h: (b, h)),
        compiler_params=_params(("parallel", "parallel"), vmem),
        name="xattn",
    )(q, kv, kv)


def kernel(x, mem, rel_bias, ffn1_norm_pre, ffn1_norm_post, ffn1_w_gate, ffn1_w_up, ffn1_w_down, mix_norm_pre, mix_norm_post, w_in, conv_w, conv_b, b_igate, b_fgate, mlstm_norm, diff_lambda, diff_subln, w_out, xattn_norm_pre, xattn_norm_post, mem_norm, xattn_wq, xattn_wk, xattn_wv, xattn_wo, ffn2_norm_pre, ffn2_norm_post, ffn2_w_gate, ffn2_w_up, ffn2_w_down):
    batch, seq, d = x.shape
    mem_len = mem.shape[1]
    depth = w_in.shape[0]
    n_gate = 2 * M_HEADS
    da_qw = DA_HEADS * 2 * DA_D
    gate0 = 2 * M_HEADS * M_DQK + 2 * M_HEADS * M_DV
    xf = x.reshape(batch * seq, d)
    memf = mem.reshape(batch * mem_len, d)
    row = lambda v: v.reshape(1, -1).astype(F32)

    tiles = _bias_tiles(rel_bias, min(DA_BLOCK, seq))
    w_main, w_gate = _cast_w_in(w_in, gate0=gate0, n_gate=n_gate, scaled=da_qw, scale=DA_D ** -0.5 * LOG2E)
    ffn1 = (_cast_blocks([ffn1_w_gate, ffn1_w_up], FFN_BLOCK, interleave=True), _cast_blocks([ffn1_w_down]))
    ffn2 = (_cast_blocks([ffn2_w_gate, ffn2_w_up], FFN_BLOCK, interleave=True), _cast_blocks([ffn2_w_down]))
    w_o = _cast_blocks([w_out])
    w_q = _cast_blocks([xattn_wq], PROJ_BLOCK, scale=(d // X_HEADS) ** -0.5 * LOG2E)
    w_kv = _cast_blocks([xattn_wk, xattn_wv], PROJ_BLOCK)
    w_xo = _cast_blocks([xattn_wo])

    for l in range(depth):
        lam_init = 0.8 - 0.6 * math.exp(-0.3 * l)
        xf = _ffn(xf, row(ffn1_norm_pre[l]), row(ffn1_norm_post[l]), *ffn1, l)

        z, gates = _normproj(xf, row(mix_norm_pre[l]), w_main, l, w_gate)
        gate_bias = jnp.pad(jnp.concatenate([b_igate[l], b_fgate[l]]), (0, LANES - n_gate)).reshape(1, LANES)
        y_m = _mlstm(z, gates, gate_bias.astype(F32), conv_w[l].astype(F32), row(conv_b[l]),
                     row(mlstm_norm[l]), batch=batch, seq=seq)
        y_d = _diffattn(z, tiles, diff_lambda[l].astype(F32), row(diff_subln[l]),
                        batch=batch, seq=seq, lam_init=lam_init, col0=gate0)
        xf = _outproj(xf, row(mix_norm_post[l]), [y_m, y_d], w_o, l)

        q = _normproj(xf, row(xattn_norm_pre[l]), w_q, l)
        kv = _normproj(memf, row(mem_norm[l]), w_kv, l)
        c = _xattn(q, kv, batch=batch, seq=seq, mem_len=mem_len)
        xf = _outproj(xf, row(xattn_norm_post[l]), [c], w_xo, l)

        xf = _ffn(xf, row(ffn2_norm_pre[l]), row(ffn2_norm_post[l]), *ffn2, l)
    return xf.reshape(batch, seq, d)
```

```python
import functools
import math

import numpy as np
import jax
import jax.numpy as jnp
from jax import lax
from jax.experimental import pallas as pl
from jax.experimental.pallas import tpu as pltpu

F32 = jnp.float32
BF16 = jnp.bfloat16
EPS = 1e-6
LOG2E = math.log2(math.e)

M_HEADS = 4
M_DQK = 128
M_DV = 256
CONV_K = 4
DA_HEADS = 8
DA_D = 64
DA_DV = 128
X_HEADS = 4
NUM_BUCKETS = 32
MAX_EXACT = NUM_BUCKETS // 2
MAX_DISTANCE = 128

LANES = 128
MXU_N = 256
V7X_VMEM_BYTES = 64 * 2 ** 20
MLSTM_CHUNK = 256
DA_BLOCK = 256
DA_LOOKAHEAD = 2
FFN_BLOCK = 1408
PROJ_BLOCK = 1024
INPROJ_BLOCK = 1536


def _params(semantics, vmem_bytes):
    assert vmem_bytes <= V7X_VMEM_BYTES
    return pltpu.CompilerParams(dimension_semantics=semantics, vmem_limit_bytes=int(vmem_bytes))


def _rmsnorm(xf, g):
    return xf * lax.rsqrt(jnp.mean(xf * xf, axis=-1, keepdims=True) + EPS) * g


def _log_sigmoid(x):
    return jnp.minimum(x, 0.0) - jnp.log1p(jnp.exp(-jnp.abs(x)))


def _dot(a, b):
    return jnp.dot(a, b, preferred_element_type=F32)


def _dot_nt(a, b):
    return lax.dot_general(a, b, (((1,), (1,)), ((), ())), preferred_element_type=F32)


def _dot_tn(a, b):
    return lax.dot_general(a, b, (((0,), (0,)), ((), ())), preferred_element_type=F32)


def _cast_body(*refs, bn, scale, interleave):
    *in_refs, o_ref = refs
    cast = lambda blk: (blk if scale is None else blk * scale).astype(BF16)
    if interleave:
        for b in range(in_refs[0].shape[2] // bn):
            for t, x_ref in enumerate(in_refs):
                o_ref[0, b, :, t * bn:(t + 1) * bn] = cast(x_ref[0, :, b * bn:(b + 1) * bn])
    else:
        c = 0
        for x_ref in in_refs:
            for b in range(x_ref.shape[2] // bn):
                o_ref[0, c] = cast(x_ref[0, :, b * bn:(b + 1) * bn])
                c += 1


def _cast_blocks(ws, bn=None, *, scale=None, interleave=False, kb=256):
    depth, k, _ = ws[0].shape
    bn = bn or ws[0].shape[2]
    n_total = sum(w.shape[2] for w in ws)
    width = bn * len(ws) if interleave else bn
    kb = min(kb, k)
    vmem = 2 * kb * n_total * (4 + 2) + (4 << 20)
    return pl.pallas_call(
        functools.partial(_cast_body, bn=bn, scale=scale, interleave=interleave),
        out_shape=jax.ShapeDtypeStruct((depth, n_total // width, k, width), BF16),
        grid=(depth, k // kb),
        in_specs=[pl.BlockSpec((1, kb, w.shape[2]), lambda l, i: (l, i, 0)) for w in ws],
        out_specs=pl.BlockSpec((1, n_total // width, kb, width), lambda l, i: (l, 0, i, 0)),
        compiler_params=_params(("parallel", "parallel"), vmem),
        name="cast_blocks",
    )(*ws)


def _cast_w_in_body(x_ref, o_ref, gate_ref, *, gate0, n_gate, bn, scaled, scale):
    x = x_ref[0]
    left = x[:, :gate0]
    right = x[:, gate0 + n_gate:]
    right = jnp.concatenate([right[:, :scaled] * scale, right[:, scaled:]], axis=1)
    c = 0
    for part in (left, right):
        for b in range(part.shape[1] // bn):
            o_ref[0, c] = part[:, b * bn:(b + 1) * bn].astype(BF16)
            c += 1
    group = x[:, gate0:gate0 + LANES]
    lane = lax.broadcasted_iota(jnp.int32, group.shape, 1)
    gate_ref[0] = jnp.where(lane < n_gate, group, 0.0).astype(BF16)


def _cast_w_in(w_in, *, gate0, n_gate, scaled, scale, bn=INPROJ_BLOCK, kb=256):
    depth, k, n = w_in.shape
    n_main = n - n_gate
    kb = min(kb, k)
    vmem = 2 * kb * n * 4 + 2 * kb * n_main * 2 + 3 * kb * n * 4 + (4 << 20)
    return pl.pallas_call(
        functools.partial(_cast_w_in_body, gate0=gate0, n_gate=n_gate, bn=bn, scaled=scaled, scale=scale),
        out_shape=[jax.ShapeDtypeStruct((depth, n_main // bn, k, bn), BF16),
                   jax.ShapeDtypeStruct((depth, k, LANES), BF16)],
        grid=(depth, k // kb),
        in_specs=[pl.BlockSpec((1, kb, n), lambda l, i: (l, i, 0))],
        out_specs=[pl.BlockSpec((1, n_main // bn, kb, bn), lambda l, i: (l, 0, i, 0)),
                   pl.BlockSpec((1, kb, LANES), lambda l, i: (l, i, 0))],
        compiler_params=_params(("parallel", "parallel"), vmem),
        name="cast_w_in",
    )(w_in)


def _ffn_up_body(x_ref, g_ref, w_ref, h_ref, xn_ref):
    @pl.when(pl.program_id(1) == 0)
    def _():
        xn_ref[...] = _rmsnorm(x_ref[...], g_ref[...]).astype(BF16)

    bf = h_ref.shape[1]
    gu = _dot(xn_ref[...], w_ref[0, 0])
    g, u = gu[:, :bf], gu[:, bf:]
    h_ref[...] = (g * jax.nn.sigmoid(g) * u).astype(h_ref.dtype)


def _ffn(x, g_pre, g_post, w_gate_up, w_down, layer, *, bm=512, bm_down=512):
    m, d = x.shape
    nf, bf = w_gate_up.shape[1], w_gate_up.shape[3] // 2
    f = nf * bf
    bm = min(bm, m)
    assert m % bm == 0
    vmem_up = 2 * bm * d * 4 + bm * d * 2 + 4 * d * bf * 2 + 2 * bm * bf * 2 + 3 * bm * bf * 4 + (4 << 20)
    hidden = pl.pallas_call(
        _ffn_up_body,
        out_shape=jax.ShapeDtypeStruct((m, f), BF16),
        grid=(m // bm, nf),
        in_specs=[
            pl.BlockSpec((bm, d), lambda i, j: (i, 0)),
            pl.BlockSpec((1, d), lambda i, j: (0, 0)),
            pl.BlockSpec((1, 1, d, 2 * bf), lambda i, j: (layer, j, 0, 0)),
        ],
        out_specs=pl.BlockSpec((bm, bf), lambda i, j: (i, j)),
        scratch_shapes=[pltpu.VMEM((bm, d), BF16)],
        compiler_params=_params(("parallel", "arbitrary"), vmem_up),
        name="ffn_up",
    )(x, g_pre, w_gate_up)
    return _outproj(x, g_post, [hidden], w_down, layer, res_scale=0.5, bm=bm_down)


def _normproj_body(x_ref, g_ref, w_ref, *rest, with_gate):
    if with_gate:
        wgate_ref, o_ref, gate_ref, xn_ref = rest
    else:
        o_ref, xn_ref = rest
    j = pl.program_id(1)

    @pl.when(j == 0)
    def _():
        xn = _rmsnorm(x_ref[...], g_ref[...]).astype(BF16)
        xn_ref[...] = xn
        if with_gate:
            gate_ref[...] = _dot(xn, wgate_ref[0])

    o_ref[...] = _dot(xn_ref[...], w_ref[0, 0]).astype(o_ref.dtype)


def _normproj(x, g, w, layer, w_gate=None, *, bm=1024):
    m, d = x.shape
    nb, bn = w.shape[1], w.shape[3]
    bm = min(bm, m)
    assert m % bm == 0
    with_gate = w_gate is not None
    in_specs = [
        pl.BlockSpec((bm, d), lambda i, j: (i, 0)),
        pl.BlockSpec((1, d), lambda i, j: (0, 0)),
        pl.BlockSpec((1, 1, d, bn), lambda i, j: (layer, j, 0, 0)),
    ]
    out_shape = [jax.ShapeDtypeStruct((m, nb * bn), BF16)]
    out_specs = [pl.BlockSpec((bm, bn), lambda i, j: (i, j))]
    args = [x, g, w]
    if with_gate:
        in_specs.append(pl.BlockSpec((1, d, LANES), lambda i, j: (layer, 0, 0)))
        out_shape.append(jax.ShapeDtypeStruct((m, LANES), F32))
        out_specs.append(pl.BlockSpec((bm, LANES), lambda i, j: (i, 0)))
        args.append(w_gate)
    vmem = 2 * bm * d * 4 + bm * d * 2 + 2 * d * bn * 2 + 2 * bm * bn * 2 + bm * bn * 4 + (6 << 20)
    out = pl.pallas_call(
        functools.partial(_normproj_body, with_gate=with_gate),
        out_shape=out_shape,
        grid=(m // bm, nb),
        in_specs=in_specs,
        out_specs=out_specs,
        scratch_shapes=[pltpu.VMEM((bm, d), BF16)],
        compiler_params=_params(("parallel", "arbitrary"), vmem),
        name="normproj_gate" if with_gate else "normproj",
    )(*args)
    return out if with_gate else out[0]


def _outproj_body(*refs, n_parts, res_scale):
    a_refs = refs[:n_parts]
    w_refs = refs[n_parts:2 * n_parts]
    x_ref, g_ref, o_ref = refs[2 * n_parts:]
    y = _dot(a_refs[0][...], w_refs[0][0, 0])
    for a_ref, w_ref in zip(a_refs[1:], w_refs[1:]):
        y += _dot(a_ref[...], w_ref[0, 0])
    o_ref[...] = x_ref[...] + res_scale * _rmsnorm(y, g_ref[...])


def _outproj(x, g, acts, w, layer, *, res_scale=1.0, bm=512):
    m, d = x.shape
    bm = min(bm, m)
    assert m % bm == 0
    k_total = w.shape[2]
    in_specs = [pl.BlockSpec((bm, a.shape[1]), lambda i: (i, 0)) for a in acts]
    row0 = 0
    for a in acts:
        k = a.shape[1]
        assert row0 % k == 0
        in_specs.append(pl.BlockSpec((1, 1, k, d), functools.partial(lambda i, rb: (layer, 0, rb, 0), rb=row0 // k),
                                     pipeline_mode=pl.Buffered(1)))
        row0 += k
    assert row0 == k_total
    in_specs += [pl.BlockSpec((bm, d), lambda i: (i, 0)), pl.BlockSpec((1, d), lambda i: (0, 0))]
    vmem = k_total * d * 2 + 2 * bm * k_total * 2 + 5 * bm * d * 4 + (4 << 20)
    return pl.pallas_call(
        functools.partial(_outproj_body, n_parts=len(acts), res_scale=res_scale),
        out_shape=jax.ShapeDtypeStruct((m, d), F32),
        grid=(m // bm,),
        in_specs=in_specs,
        out_specs=pl.BlockSpec((bm, d), lambda i: (i, 0)),
        compiler_params=_params(("parallel",), vmem),
        name="outproj",
    )(*acts, *([w] * len(acts)), x, g)


def _conv_silu(raw, w, b):
    rows = lax.broadcasted_iota(jnp.int32, raw.shape, 0)
    acc = raw * w[CONV_K - 1:CONV_K, :] + b
    for shift in range(1, CONV_K):
        shifted = jnp.where(rows >= shift, pltpu.roll(raw, shift, 0), 0.0)
        acc += shifted * w[CONV_K - 1 - shift:CONV_K - shift, :]
    return acc * jax.nn.sigmoid(acc)


GATE_ROWS = 16


def _split3(x):
    hi = x.astype(BF16)
    r1 = x - hi.astype(F32)
    mid = r1.astype(BF16)
    return hi, mid, (r1 - mid.astype(F32)).astype(BF16)


def _mlstm_body(q_ref, k_ref, v_ref, og_ref, gates_ref, gbias_ref, cwq_ref, cbq_ref, cwk_ref, cbk_ref,
                norm_ref, o_ref, qs_ref, ks_ref, vx_ref, gt_ref, bt_ref, row_ref, *, chunk):
    s = q_ref.shape[0]
    nc = s // chunk
    head = pl.program_id(1)
    ext = vx_ref.shape[1]
    qs_ref[...] = _conv_silu(q_ref[...].astype(F32), cwq_ref[...], cbq_ref[...]).astype(BF16)
    ks_ref[...] = _conv_silu(k_ref[...].astype(F32), cwk_ref[...], cbk_ref[...]) * (M_DQK ** -0.5)
    ones_col = (lax.broadcasted_iota(jnp.int32, (s, ext - M_DV), 1) == 0).astype(BF16)
    vx_ref[...] = jnp.concatenate([v_ref[...], ones_col], axis=1)

    upper = (lax.broadcasted_iota(jnp.int32, (chunk, chunk), 0)
             <= lax.broadcasted_iota(jnp.int32, (chunk, chunk), 1)).astype(BF16)
    for c in range(nc):
        cols = slice(c * chunk, (c + 1) * chunk)
        gt = (gates_ref[cols, :] + gbias_ref[...]).T[:GATE_ROWS]
        gt_ref[:, cols] = gt
        bt_ref[:, cols] = sum(_dot(piece, upper) for piece in _split3(_log_sigmoid(gt)))

    i_row = gt_ref[pl.ds(head, 1), :]
    b_row = bt_ref[pl.ds(M_HEADS + head, 1), :]
    r_row = i_row - b_row
    pos = lax.broadcasted_iota(jnp.int32, (1, s), 1) % chunk
    cmax = r_row
    shift = 1
    while shift < chunk:
        cmax = jnp.maximum(cmax, jnp.where(pos >= shift, pltpu.roll(cmax, shift, 1), -jnp.inf))
        shift *= 2

    row_ref[...] = jnp.zeros(row_ref.shape, F32)
    m = jnp.zeros((1, 1), F32)
    ws_rows, decays = [], []
    for c in range(nc):
        cols = slice(c * chunk, (c + 1) * chunk)
        a = jnp.maximum(m, cmax[:, cols])
        a_last = a[:, chunk - 1:]
        b_c = b_row[:, cols]
        row_ref[0:1, cols] = a
        row_ref[1:2, cols] = jnp.exp(m - a)
        row_ref[2:3, cols] = jnp.exp(-(b_c + a))
        ws_rows.append(jnp.exp(r_row[:, cols] - a_last))
        decays.append(jnp.exp(m - a_last))
        m = b_c[:, chunk - 1:] + a_last

    causal = (lax.broadcasted_iota(jnp.int32, (chunk, chunk), 0)
              >= lax.broadcasted_iota(jnp.int32, (chunk, chunk), 1))
    gain = norm_ref[...]

    def independent(c):
        rows = slice(c * chunk, (c + 1) * chunk)
        kf = ks_ref[rows, :]
        s_qk = _dot_nt(qs_ref[rows, :], kf.astype(BF16))
        kv = _dot((kf.T * ws_rows[c]).astype(BF16), vx_ref[rows, :])
        return s_qk, kv

    cmat = None
    nxt = independent(0)
    for c in range(nc):
        rows = slice(c * chunk, (c + 1) * chunk)
        s_qk, kv = nxt
        if c + 1 < nc:
            nxt = independent(c + 1)
        col = row_ref[:, rows].T
        a_col, wi_col, en_col = col[:, 0:1], col[:, 1:2], col[:, 2:3]
        p = jnp.exp(jnp.where(causal, r_row[:, rows] - a_col, -jnp.inf)) * s_qk
        hx = _dot(p.astype(BF16), vx_ref[rows, :])
        if cmat is not None:
            hx += wi_col * _dot(qs_ref[rows, :], cmat.astype(BF16))
        hc = hx[:, :M_DV] / jnp.maximum(jnp.abs(hx[:, M_DV:M_DV + 1]), en_col)
        cmat = kv if cmat is None else decays[c] * cmat + kv
        y = _rmsnorm(hc, gain) * jax.nn.sigmoid(og_ref[rows, :].astype(F32))
        o_ref[rows, :] = y.astype(o_ref.dtype)


def _mlstm(z, gates, gate_bias, conv_w, conv_b, norm, *, batch, seq):
    chunk = min(MLSTM_CHUNK, seq)
    qk_blocks = M_HEADS
    v_block0 = 2 * M_HEADS * M_DQK // M_DV
    og_block0 = v_block0 + M_HEADS
    vmem = seq * (2 * 2 * M_DQK * 2 + 3 * 2 * M_DV * 2 + 3 * LANES * 4 + M_DQK * 6) + (24 << 20)
    return pl.pallas_call(
        functools.partial(_mlstm_body, chunk=chunk),
        out_shape=jax.ShapeDtypeStruct((batch * seq, M_HEADS * M_DV), BF16),
        grid=(batch, M_HEADS),
        in_specs=[
            pl.BlockSpec((seq, M_DQK), lambda b, h: (b, h)),
            pl.BlockSpec((seq, M_DQK), lambda b, h: (b, qk_blocks + h)),
            pl.BlockSpec((seq, M_DV), lambda b, h: (b, v_block0 + h)),
            pl.BlockSpec((seq, M_DV), lambda b, h: (b, og_block0 + h)),
            pl.BlockSpec((seq, LANES), lambda b, h: (b, 0)),
            pl.BlockSpec((1, LANES), lambda b, h: (0, 0)),
            pl.BlockSpec((CONV_K, M_DQK), lambda b, h: (0, h)),
            pl.BlockSpec((1, M_DQK), lambda b, h: (0, h)),
            pl.BlockSpec((CONV_K, M_DQK), lambda b, h: (0, qk_blocks + h)),
            pl.BlockSpec((1, M_DQK), lambda b, h: (0, qk_blocks + h)),
            pl.BlockSpec((1, M_DV), lambda b, h: (0, h)),
        ],
        out_specs=pl.BlockSpec((seq, M_DV), lambda b, h: (b, h)),
        scratch_shapes=[pltpu.VMEM((seq, M_DQK), BF16), pltpu.VMEM((seq, M_DQK), F32),
                        pltpu.VMEM((seq, M_DV + LANES), BF16), pltpu.VMEM((GATE_ROWS, seq), F32),
                        pltpu.VMEM((GATE_ROWS, seq), F32), pltpu.VMEM((LANES, seq), F32)],
        compiler_params=_params(("parallel", "parallel"), vmem),
        name="mlstm",
    )(z, z, z, z, gates, gate_bias, conv_w, conv_b, conv_w, conv_b, norm)


def _t5_bucket_np(n):
    nf = np.maximum(n, 1).astype(np.float64)
    val = np.log(nf / MAX_EXACT) / math.log(MAX_DISTANCE / MAX_EXACT) * (NUM_BUCKETS - MAX_EXACT)
    frac = val[(n > MAX_EXACT) & (n < MAX_DISTANCE)]
    assert np.all(np.abs(frac - np.round(frac)) > 1e-3)
    large = np.minimum(MAX_EXACT + val.astype(np.int64), NUM_BUCKETS - 1)
    return np.where(n < MAX_EXACT, n, large)


def _bucket_tile(blk):
    rel = np.arange(blk)[None, :] + blk - np.arange(2 * blk)[:, None]
    return np.where(rel >= 0, _t5_bucket_np(np.maximum(rel, 0)), -1).astype(np.int32)


def _bias_body(rb_ref, bucket_ref, t_ref):
    h = pl.program_id(0)
    bucket = bucket_ref[...]
    acc = jnp.zeros(bucket.shape, F32)
    for b in range(NUM_BUCKETS):
        acc = jnp.where(bucket == b, rb_ref[h * NUM_BUCKETS + b], acc)
    far = rb_ref[h * NUM_BUCKETS + NUM_BUCKETS - 1]
    t_ref[0] = jnp.where(bucket < 0, -jnp.inf, (acc - far) * LOG2E)


def _bias_tiles(rel_bias, blk):
    assert blk >= MAX_DISTANCE
    bucket = jnp.asarray(_bucket_tile(blk))
    return pl.pallas_call(
        _bias_body,
        out_shape=jax.ShapeDtypeStruct((DA_HEADS, 2 * blk, blk), F32),
        grid=(DA_HEADS,),
        in_specs=[pl.BlockSpec(memory_space=pltpu.SMEM),
                  pl.BlockSpec((2 * blk, blk), lambda h: (0, 0))],
        out_specs=pl.BlockSpec((1, 2 * blk, blk), lambda h: (h, 0, 0)),
        compiler_params=_params(("arbitrary",), 16 << 20),
        name="t5_bias_tiles",
    )(rel_bias.T.reshape(-1), bucket)


VT_ROWS = DA_DV + 16


def _diffattn_body(lam_ref, q_ref, k_ref, v_ref, t_ref, g_ref, o_ref, vt_ref, *, blk, lam_init):
    nb = q_ref.shape[0] // blk
    ones_rows = (lax.broadcasted_iota(jnp.int32, (VT_ROWS - DA_DV, blk), 0) == 0).astype(F32)
    for c in range(nb):
        vt = v_ref[c * blk:(c + 1) * blk, :].astype(F32).T
        vt_ref[c] = jnp.concatenate([vt, ones_rows], axis=0).astype(BF16)

    lp = lam_ref[...]
    lam = (jnp.exp(jnp.sum(lp[0:1] * lp[1:2], axis=-1, keepdims=True))
           - jnp.exp(jnp.sum(lp[2:3] * lp[3:4], axis=-1, keepdims=True)) + lam_init)
    sub = lax.broadcasted_iota(jnp.int32, (LANES, blk), 0)
    gain = g_ref[...]

    def query_maps(qi):
        qt = q_ref[qi * blk:(qi + 1) * blk, :].astype(F32).T
        return (jnp.where(sub < DA_D, qt, 0.0).astype(BF16), jnp.where(sub >= DA_D, qt, 0.0).astype(BF16))

    def scores(qts, qi, kc):
        kb = k_ref[kc * blk:(kc + 1) * blk, :]
        s = [_dot(kb, qt_m) for qt_m in qts]
        if kc >= qi - 1:
            bias = t_ref[0, blk:, :] if kc == qi else t_ref[0, :blk, :]
            s = [s_m + bias for s_m in s]
        return s

    units = [(qi, kc) for qi in range(nb) for kc in range(qi, -1, -1)]
    qts_of = {}

    def unit_scores(u):
        qi, kc = units[u]
        if qi not in qts_of:
            qts_of.clear()
            qts_of[qi] = query_maps(qi)
        return scores(qts_of[qi], qi, kc)

    ahead = [unit_scores(u) for u in range(min(DA_LOOKAHEAD, len(units)))]
    state = [None, None]
    for u, (qi, kc) in enumerate(units):
        s_cur = ahead.pop(0)
        if u + DA_LOOKAHEAD < len(units):
            ahead.append(unit_scores(u + DA_LOOKAHEAD))
        for a in range(2):
            s = s_cur[a]
            blk_max = jnp.max(s, axis=0, keepdims=True)
            if state[a] is None:
                m_new = blk_max
                acc = _dot(vt_ref[kc], jnp.exp2(s - m_new).astype(BF16))
            else:
                m, acc = state[a]
                m_new = jnp.maximum(m, blk_max)
                acc = acc * jnp.exp2(m - m_new) + _dot(vt_ref[kc], jnp.exp2(s - m_new).astype(BF16))
            state[a] = (m_new, acc)
        if kc == 0:
            outs = [acc[:DA_DV] / acc[DA_DV:DA_DV + 1] for _, acc in state]
            o = (outs[0] - lam * outs[1]).T
            o_ref[qi * blk:(qi + 1) * blk, :] = (_rmsnorm(o, gain) * (1.0 - lam_init)).astype(o_ref.dtype)
            state = [None, None]


def _diffattn(z, tiles, lam_p, subln, *, batch, seq, lam_init, col0):
    blk = tiles.shape[2]
    q0 = col0 // LANES
    k0 = q0 + DA_HEADS
    v0 = k0 + DA_HEADS
    vmem = 8 * seq * LANES * 2 + seq * VT_ROWS * 2 + 4 * blk * blk * 4 + 24 * blk * blk * 4 + (8 << 20)
    return pl.pallas_call(
        functools.partial(_diffattn_body, blk=blk, lam_init=lam_init),
        out_shape=jax.ShapeDtypeStruct((batch * seq, DA_HEADS * DA_DV), BF16),
        grid=(batch, DA_HEADS),
        in_specs=[
            pl.BlockSpec((4, DA_D), lambda b, h: (0, 0)),
            pl.BlockSpec((seq, LANES), lambda b, h: (b, q0 + h)),
            pl.BlockSpec((seq, LANES), lambda b, h: (b, k0 + h)),
            pl.BlockSpec((seq, LANES), lambda b, h: (b, v0 + h)),
            pl.BlockSpec((1, 2 * blk, blk), lambda b, h: (h, 0, 0)),
            pl.BlockSpec((1, DA_DV), lambda b, h: (0, 0)),
        ],
        out_specs=pl.BlockSpec((seq, DA_DV), lambda b, h: (b, h)),
        scratch_shapes=[pltpu.VMEM((seq // blk, VT_ROWS, blk), BF16)],
        compiler_params=_params(("parallel", "parallel"), vmem),
        name="diffattn",
    )(lam_p, z, z, z, tiles, subln)


def _xattn_body(q_ref, k_ref, v_ref, o_ref, *, blk):
    nb = q_ref.shape[0] // blk
    kb = k_ref[...]
    vb = v_ref[...]
    s_next = _dot_nt(q_ref[0:blk, :], kb)
    for j in range(nb):
        s = s_next
        if j + 1 < nb:
            s_next = _dot_nt(q_ref[(j + 1) * blk:(j + 2) * blk, :], kb)
        p = jnp.exp2(s - jnp.max(s, axis=-1, keepdims=True))
        o = _dot(p.astype(BF16), vb) / jnp.sum(p, axis=-1, keepdims=True)
        o_ref[j * blk:(j + 1) * blk, :] = o.astype(o_ref.dtype)


def _xattn(q, kv, *, batch, seq, mem_len, blk=256):
    d = q.shape[1]
    dh = d // X_HEADS
    blk = min(blk, seq)
    vmem = 4 * seq * dh * 2 + 4 * mem_len * dh * 2 + 8 * blk * (mem_len + dh) * 4 + (8 << 20)
    return pl.pallas_call(
        functools.partial(_xattn_body, blk=blk),
        out_shape=jax.ShapeDtypeStruct((batch * seq, d), BF16),
        grid=(batch, X_HEADS),
        in_specs=[
            pl.BlockSpec((seq, dh), lambda b, h: (b, h)),
            pl.BlockSpec((mem_len, dh), lambda b, h: (b, h)),
            pl.BlockSpec((mem_len, dh), lambda b, h: (b, X_HEADS + h)),
        ],
        out_specs=pl.BlockSpec((seq, dh), lambda b, h: (b, h)),
        compiler_params=_params(("parallel", "parallel"), vmem),
        name="xattn",
    )(q, kv, kv)


def kernel(x, mem, rel_bias, ffn1_norm_pre, ffn1_norm_post, ffn1_w_gate, ffn1_w_up, ffn1_w_down, mix_norm_pre, mix_norm_post, w_in, conv_w, conv_b, b_igate, b_fgate, mlstm_norm, diff_lambda, diff_subln, w_out, xattn_norm_pre, xattn_norm_post, mem_norm, xattn_wq, xattn_wk, xattn_wv, xattn_wo, ffn2_norm_pre, ffn2_norm_post, ffn2_w_gate, ffn2_w_up, ffn2_w_down):
    batch, seq, d = x.shape
    mem_len = mem.shape[1]
    depth = w_in.shape[0]
    n_gate = 2 * M_HEADS
    da_qw = DA_HEADS * 2 * DA_D
    gate0 = 2 * M_HEADS * M_DQK + 2 * M_HEADS * M_DV
    xf = x.reshape(batch * seq, d)
    memf = mem.reshape(batch * mem_len, d)
    row = lambda v: v.reshape(1, -1).astype(F32)

    tiles = _bias_tiles(rel_bias, min(DA_BLOCK, seq))
    w_main, w_gate = _cast_w_in(w_in, gate0=gate0, n_gate=n_gate, scaled=da_qw, scale=DA_D ** -0.5 * LOG2E)
    ffn1 = (_cast_blocks([ffn1_w_gate, ffn1_w_up], FFN_BLOCK, interleave=True), _cast_blocks([ffn1_w_down]))
    ffn2 = (_cast_blocks([ffn2_w_gate, ffn2_w_up], FFN_BLOCK, interleave=True), _cast_blocks([ffn2_w_down]))
    w_o = _cast_blocks([w_out])
    w_q = _cast_blocks([xattn_wq], PROJ_BLOCK, scale=(d // X_HEADS) ** -0.5 * LOG2E)
    w_kv = _cast_blocks([xattn_wk, xattn_wv], PROJ_BLOCK)
    w_xo = _cast_blocks([xattn_wo])

    for l in range(depth):
        lam_init = 0.8 - 0.6 * math.exp(-0.3 * l)
        xf = _ffn(xf, row(ffn1_norm_pre[l]), row(ffn1_norm_post[l]), *ffn1, l)

        z, gates = _normproj(xf, row(mix_norm_pre[l]), w_main, l, w_gate)
        gate_bias = jnp.pad(jnp.concatenate([b_igate[l], b_fgate[l]]), (0, LANES - n_gate)).reshape(1, LANES)
        y_m = _mlstm(z, gates, gate_bias.astype(F32), conv_w[l].astype(F32), row(conv_b[l]),
                     row(mlstm_norm[l]), batch=batch, seq=seq)
        y_d = _diffattn(z, tiles, diff_lambda[l].astype(F32), row(diff_subln[l]),
                        batch=batch, seq=seq, lam_init=lam_init, col0=gate0)
        xf = _outproj(xf, row(mix_norm_post[l]), [y_m, y_d], w_o, l)

        q = _normproj(xf, row(xattn_norm_pre[l]), w_q, l)
        kv = _normproj(memf, row(mem_norm[l]), w_kv, l)
        c = _xattn(q, kv, batch=batch, seq=seq, mem_len=mem_len)
        xf = _outproj(xf, row(xattn_norm_post[l]), [c], w_xo, l)

        xf = _ffn(xf, row(ffn2_norm_pre[l]), row(ffn2_norm_post[l]), *ffn2, l)
    return xf.reshape(batch, seq, d)
```

```python
import functools
import math

import numpy as np
import jax
import jax.numpy as jnp
from jax import lax
from jax.experimental import pallas as pl
from jax.experimental.pallas import tpu as pltpu

F32 = jnp.float32
BF16 = jnp.bfloat16
EPS = 1e-6
LOG2E = math.log2(math.e)

M_HEADS = 4
M_DQK = 128
M_DV = 256
CONV_K = 4
DA_HEADS = 8
DA_D = 64
DA_DV = 128
X_HEADS = 4
NUM_BUCKETS = 32
MAX_EXACT = NUM_BUCKETS // 2
MAX_DISTANCE = 128

LANES = 128
MXU_N = 256
V7X_VMEM_BYTES = 64 * 2 ** 20
MLSTM_CHUNK = 256
DA_BLOCK = 256
DA_LOOKAHEAD = 2
FFN_BLOCK = 1408
FFN_ROWS = 512
FFN_ROWS_PRENORMED = 1024
DOWN_ROWS = 512
DOWN_ROWS_HANDOFF = 256
PROJ_BLOCK = 1024
INPROJ_BLOCK = 1536


def _params(semantics, vmem_bytes):
    assert vmem_bytes <= V7X_VMEM_BYTES
    return pltpu.CompilerParams(dimension_semantics=semantics, vmem_limit_bytes=int(vmem_bytes))


def _rmsnorm(xf, g):
    return xf * lax.rsqrt(jnp.mean(xf * xf, axis=-1, keepdims=True) + EPS) * g


def _log_sigmoid(x):
    return jnp.minimum(x, 0.0) - jnp.log1p(jnp.exp(-jnp.abs(x)))


def _dot(a, b):
    return jnp.dot(a, b, preferred_element_type=F32)


def _dot_nt(a, b):
    return lax.dot_general(a, b, (((1,), (1,)), ((), ())), preferred_element_type=F32)


def _cast_body(*refs, bn, scale, interleave):
    *in_refs, o_ref = refs
    cast = lambda blk: (blk if scale is None else blk * scale).astype(BF16)
    if interleave:
        for b in range(in_refs[0].shape[2] // bn):
            for t, x_ref in enumerate(in_refs):
                o_ref[0, b, :, t * bn:(t + 1) * bn] = cast(x_ref[0, :, b * bn:(b + 1) * bn])
    else:
        c = 0
        for x_ref in in_refs:
            for b in range(x_ref.shape[2] // bn):
                o_ref[0, c] = cast(x_ref[0, :, b * bn:(b + 1) * bn])
                c += 1


def _cast_blocks(ws, bn=None, *, scale=None, interleave=False, kb=256):
    depth, k, _ = ws[0].shape
    bn = bn or ws[0].shape[2]
    n_total = sum(w.shape[2] for w in ws)
    width = bn * len(ws) if interleave else bn
    kb = min(kb, k)
    vmem = 2 * kb * n_total * (4 + 2) + (4 << 20)
    return pl.pallas_call(
        functools.partial(_cast_body, bn=bn, scale=scale, interleave=interleave),
        out_shape=jax.ShapeDtypeStruct((depth, n_total // width, k, width), BF16),
        grid=(depth, k // kb),
        in_specs=[pl.BlockSpec((1, kb, w.shape[2]), lambda l, i: (l, i, 0)) for w in ws],
        out_specs=pl.BlockSpec((1, n_total // width, kb, width), lambda l, i: (l, 0, i, 0)),
        compiler_params=_params(("parallel", "parallel"), vmem),
        name="cast_blocks",
    )(*ws)


def _cast_w_in_body(x_ref, o_ref, gate_ref, *, gate0, n_gate, bn, scaled, scale):
    x = x_ref[0]
    left = x[:, :gate0]
    right = x[:, gate0 + n_gate:]
    right = jnp.concatenate([right[:, :scaled] * scale, right[:, scaled:]], axis=1)
    c = 0
    for part in (left, right):
        for b in range(part.shape[1] // bn):
            o_ref[0, c] = part[:, b * bn:(b + 1) * bn].astype(BF16)
            c += 1
    group = x[:, gate0:gate0 + LANES]
    lane = lax.broadcasted_iota(jnp.int32, group.shape, 1)
    gate_ref[0] = jnp.where(lane < n_gate, group, 0.0).astype(BF16)


def _cast_w_in(w_in, *, gate0, n_gate, scaled, scale, bn=INPROJ_BLOCK, kb=256):
    depth, k, n = w_in.shape
    n_main = n - n_gate
    kb = min(kb, k)
    vmem = 2 * kb * n * 4 + 2 * kb * n_main * 2 + 3 * kb * n * 4 + (4 << 20)
    return pl.pallas_call(
        functools.partial(_cast_w_in_body, gate0=gate0, n_gate=n_gate, bn=bn, scaled=scaled, scale=scale),
        out_shape=[jax.ShapeDtypeStruct((depth, n_main // bn, k, bn), BF16),
                   jax.ShapeDtypeStruct((depth, k, LANES), BF16)],
        grid=(depth, k // kb),
        in_specs=[pl.BlockSpec((1, kb, n), lambda l, i: (l, i, 0))],
        out_specs=[pl.BlockSpec((1, n_main // bn, kb, bn), lambda l, i: (l, 0, i, 0)),
                   pl.BlockSpec((1, kb, LANES), lambda l, i: (l, i, 0))],
        compiler_params=_params(("parallel", "parallel"), vmem),
        name="cast_w_in",
    )(w_in)


def _ffn_up_body(*refs, prenormed):
    if prenormed:
        xn_ref, w_ref, h_ref = refs
    else:
        x_ref, g_ref, w_ref, h_ref, xn_ref = refs

        @pl.when(pl.program_id(1) == 0)
        def _():
            xn_ref[...] = _rmsnorm(x_ref[...], g_ref[...]).astype(BF16)

    bf = h_ref.shape[1]
    gu = _dot(xn_ref[...], w_ref[0, 0])
    g, u = gu[:, :bf], gu[:, bf:]
    h_ref[...] = (g * jax.nn.sigmoid(g) * u).astype(h_ref.dtype)


def _ffn(x, g_pre, g_post, w_gate_up, w_down, layer, *, xn=None, next_gain=None):
    m, d = x.shape
    nf, bf = w_gate_up.shape[1], w_gate_up.shape[3] // 2
    f = nf * bf
    prenormed = xn is not None
    bm = min(FFN_ROWS_PRENORMED if prenormed else FFN_ROWS, m)
    assert m % bm == 0
    w_spec = pl.BlockSpec((1, 1, d, 2 * bf), lambda i, j: (layer, j, 0, 0))
    row_spec = pl.BlockSpec((bm, d), lambda i, j: (i, 0))
    if prenormed:
        in_specs, args, scratch = [row_spec, w_spec], (xn, w_gate_up), []
        vmem_up = 2 * bm * d * 2 + 4 * d * bf * 2 + 2 * bm * bf * 2 + 3 * bm * bf * 4 + (4 << 20)
    else:
        in_specs = [row_spec, pl.BlockSpec((1, d), lambda i, j: (0, 0)), w_spec]
        args, scratch = (x, g_pre, w_gate_up), [pltpu.VMEM((bm, d), BF16)]
        vmem_up = 2 * bm * d * 4 + bm * d * 2 + 4 * d * bf * 2 + 2 * bm * bf * 2 + 3 * bm * bf * 4 + (4 << 20)
    hidden = pl.pallas_call(
        functools.partial(_ffn_up_body, prenormed=prenormed),
        out_shape=jax.ShapeDtypeStruct((m, f), BF16),
        grid=(m // bm, nf),
        in_specs=in_specs,
        out_specs=pl.BlockSpec((bm, bf), lambda i, j: (i, j)),
        scratch_shapes=scratch,
        compiler_params=_params(("parallel", "arbitrary"), vmem_up),
        name="ffn_up",
    )(*args)
    return _outproj(x, g_post, [hidden], w_down, layer, res_scale=0.5, next_gain=next_gain,
                    bm=DOWN_ROWS if next_gain is None else DOWN_ROWS_HANDOFF)


def _normproj_body(x_ref, g_ref, w_ref, *rest, with_gate):
    if with_gate:
        wgate_ref, o_ref, gate_ref, xn_ref = rest
    else:
        o_ref, xn_ref = rest
    j = pl.program_id(1)

    @pl.when(j == 0)
    def _():
        xn = _rmsnorm(x_ref[...], g_ref[...]).astype(BF16)
        xn_ref[...] = xn
        if with_gate:
            gate_ref[...] = _dot(xn, wgate_ref[0])

    o_ref[...] = _dot(xn_ref[...], w_ref[0, 0]).astype(o_ref.dtype)


def _normproj(x, g, w, layer, w_gate=None, *, bm=1024):
    m, d = x.shape
    nb, bn = w.shape[1], w.shape[3]
    bm = min(bm, m)
    assert m % bm == 0
    with_gate = w_gate is not None
    in_specs = [
        pl.BlockSpec((bm, d), lambda i, j: (i, 0)),
        pl.BlockSpec((1, d), lambda i, j: (0, 0)),
        pl.BlockSpec((1, 1, d, bn), lambda i, j: (layer, j, 0, 0)),
    ]
    out_shape = [jax.ShapeDtypeStruct((m, nb * bn), BF16)]
    out_specs = [pl.BlockSpec((bm, bn), lambda i, j: (i, j))]
    args = [x, g, w]
    if with_gate:
        in_specs.append(pl.BlockSpec((1, d, LANES), lambda i, j: (layer, 0, 0)))
        out_shape.append(jax.ShapeDtypeStruct((m, LANES), F32))
        out_specs.append(pl.BlockSpec((bm, LANES), lambda i, j: (i, 0)))
        args.append(w_gate)
    vmem = 2 * bm * d * 4 + bm * d * 2 + 2 * d * bn * 2 + 2 * bm * bn * 2 + bm * bn * 4 + (6 << 20)
    out = pl.pallas_call(
        functools.partial(_normproj_body, with_gate=with_gate),
        out_shape=out_shape,
        grid=(m // bm, nb),
        in_specs=in_specs,
        out_specs=out_specs,
        scratch_shapes=[pltpu.VMEM((bm, d), BF16)],
        compiler_params=_params(("parallel", "arbitrary"), vmem),
        name="normproj_gate" if with_gate else "normproj",
    )(*args)
    return out if with_gate else out[0]


def _outproj_body(*refs, n_parts, res_scale, handoff):
    a_refs = refs[:n_parts]
    w_refs = refs[n_parts:2 * n_parts]
    if handoff:
        x_ref, g_ref, gnext_ref, o_ref, xn_ref = refs[2 * n_parts:]
    else:
        x_ref, g_ref, o_ref = refs[2 * n_parts:]
    y = _dot(a_refs[0][...], w_refs[0][0, 0])
    for a_ref, w_ref in zip(a_refs[1:], w_refs[1:]):
        y += _dot(a_ref[...], w_ref[0, 0])
    out = x_ref[...] + res_scale * _rmsnorm(y, g_ref[...])
    o_ref[...] = out
    if handoff:
        xn_ref[...] = _rmsnorm(out, gnext_ref[...]).astype(xn_ref.dtype)


def _outproj(x, g, acts, w, layer, *, res_scale=1.0, next_gain=None, bm=512):
    m, d = x.shape
    bm = min(bm, m)
    assert m % bm == 0
    handoff = next_gain is not None
    k_total = w.shape[2]
    in_specs = [pl.BlockSpec((bm, a.shape[1]), lambda i: (i, 0)) for a in acts]
    row0 = 0
    for a in acts:
        k = a.shape[1]
        assert row0 % k == 0
        in_specs.append(pl.BlockSpec((1, 1, k, d), functools.partial(lambda i, rb: (layer, 0, rb, 0), rb=row0 // k),
                                     pipeline_mode=pl.Buffered(1)))
        row0 += k
    assert row0 == k_total
    gain_spec = pl.BlockSpec((1, d), lambda i: (0, 0))
    row_spec = pl.BlockSpec((bm, d), lambda i: (i, 0))
    in_specs += [row_spec, gain_spec] + ([gain_spec] if handoff else [])
    out_shape = [jax.ShapeDtypeStruct((m, d), F32)] + ([jax.ShapeDtypeStruct((m, d), BF16)] if handoff else [])
    vmem = k_total * d * 2 + 2 * bm * k_total * 2 + 5 * bm * d * 4 + (2 * bm * d * 2 if handoff else 0) + (4 << 20)
    out = pl.pallas_call(
        functools.partial(_outproj_body, n_parts=len(acts), res_scale=res_scale, handoff=handoff),
        out_shape=out_shape,
        grid=(m // bm,),
        in_specs=in_specs,
        out_specs=[row_spec] * len(out_shape),
        compiler_params=_params(("parallel",), vmem),
        name="outproj",
    )(*acts, *([w] * len(acts)), x, g, *([next_gain] if handoff else []))
    return (out[0], out[1]) if handoff else (out[0], None)


def _conv_silu(raw, w, b):
    rows = lax.broadcasted_iota(jnp.int32, raw.shape, 0)
    acc = raw * w[CONV_K - 1:CONV_K, :] + b
    for shift in range(1, CONV_K):
        shifted = jnp.where(rows >= shift, pltpu.roll(raw, shift, 0), 0.0)
        acc += shifted * w[CONV_K - 1 - shift:CONV_K - shift, :]
    return acc * jax.nn.sigmoid(acc)


GATE_ROWS = 16


def _split3(x):
    hi = x.astype(BF16)
    r1 = x - hi.astype(F32)
    mid = r1.astype(BF16)
    return hi, mid, (r1 - mid.astype(F32)).astype(BF16)


def _mlstm_body(q_ref, k_ref, v_ref, og_ref, gates_ref, gbias_ref, cwq_ref, cbq_ref, cwk_ref, cbk_ref,
                norm_ref, o_ref, qs_ref, ks_ref, vx_ref, gt_ref, bt_ref, row_ref, *, chunk):
    s = q_ref.shape[0]
    nc = s // chunk
    head = pl.program_id(1)
    ext = vx_ref.shape[1]
    qs_ref[...] = _conv_silu(q_ref[...].astype(F32), cwq_ref[...], cbq_ref[...]).astype(BF16)
    ks_ref[...] = _conv_silu(k_ref[...].astype(F32), cwk_ref[...], cbk_ref[...]) * (M_DQK ** -0.5)
    ones_col = (lax.broadcasted_iota(jnp.int32, (s, ext - M_DV), 1) == 0).astype(BF16)
    vx_ref[...] = jnp.concatenate([v_ref[...], ones_col], axis=1)

    upper = (lax.broadcasted_iota(jnp.int32, (chunk, chunk), 0)
             <= lax.broadcasted_iota(jnp.int32, (chunk, chunk), 1)).astype(BF16)
    for c in range(nc):
        cols = slice(c * chunk, (c + 1) * chunk)
        gt = (gates_ref[cols, :] + gbias_ref[...]).T[:GATE_ROWS]
        gt_ref[:, cols] = gt
        bt_ref[:, cols] = sum(_dot(piece, upper) for piece in _split3(_log_sigmoid(gt)))

    i_row = gt_ref[pl.ds(head, 1), :]
    b_row = bt_ref[pl.ds(M_HEADS + head, 1), :]
    r_row = i_row - b_row
    pos = lax.broadcasted_iota(jnp.int32, (1, s), 1) % chunk
    cmax = r_row
    shift = 1
    while shift < chunk:
        cmax = jnp.maximum(cmax, jnp.where(pos >= shift, pltpu.roll(cmax, shift, 1), -jnp.inf))
        shift *= 2

    row_ref[...] = jnp.zeros(row_ref.shape, F32)
    m = jnp.zeros((1, 1), F32)
    ws_rows, decays = [], []
    for c in range(nc):
        cols = slice(c * chunk, (c + 1) * chunk)
        a = jnp.maximum(m, cmax[:, cols])
        a_last = a[:, chunk - 1:]
        b_c = b_row[:, cols]
        row_ref[0:1, cols] = a
        row_ref[1:2, cols] = jnp.exp(m - a)
        row_ref[2:3, cols] = jnp.exp(-(b_c + a))
        ws_rows.append(jnp.exp(r_row[:, cols] - a_last))
        decays.append(jnp.exp(m - a_last))
        m = b_c[:, chunk - 1:] + a_last

    causal = (lax.broadcasted_iota(jnp.int32, (chunk, chunk), 0)
              >= lax.broadcasted_iota(jnp.int32, (chunk, chunk), 1))
    gain = norm_ref[...]

    def independent(c):
        rows = slice(c * chunk, (c + 1) * chunk)
        kf = ks_ref[rows, :]
        s_qk = _dot_nt(qs_ref[rows, :], kf.astype(BF16))
        kv = _dot((kf.T * ws_rows[c]).astype(BF16), vx_ref[rows, :])
        return s_qk, kv

    cmat = None
    nxt = independent(0)
    for c in range(nc):
        rows = slice(c * chunk, (c + 1) * chunk)
        s_qk, kv = nxt
        if c + 1 < nc:
            nxt = independent(c + 1)
        col = row_ref[:, rows].T
        a_col, wi_col, en_col = col[:, 0:1], col[:, 1:2], col[:, 2:3]
        p = jnp.exp(jnp.where(causal, r_row[:, rows] - a_col, -jnp.inf)) * s_qk
        hx = _dot(p.astype(BF16), vx_ref[rows, :])
        if cmat is not None:
            hx += wi_col * _dot(qs_ref[rows, :], cmat.astype(BF16))
        hc = hx[:, :M_DV] / jnp.maximum(jnp.abs(hx[:, M_DV:M_DV + 1]), en_col)
        cmat = kv if cmat is None else decays[c] * cmat + kv
        y = _rmsnorm(hc, gain) * jax.nn.sigmoid(og_ref[rows, :].astype(F32))
        o_ref[rows, :] = y.astype(o_ref.dtype)


def _mlstm(z, gates, gate_bias, conv_w, conv_b, norm, *, batch, seq):
    chunk = min(MLSTM_CHUNK, seq)
    qk_blocks = M_HEADS
    v_block0 = 2 * M_HEADS * M_DQK // M_DV
    og_block0 = v_block0 + M_HEADS
    vmem = seq * (2 * 2 * M_DQK * 2 + 3 * 2 * M_DV * 2 + 3 * LANES * 4 + M_DQK * 6) + (24 << 20)
    return pl.pallas_call(
        functools.partial(_mlstm_body, chunk=chunk),
        out_shape=jax.ShapeDtypeStruct((batch * seq, M_HEADS * M_DV), BF16),
        grid=(batch, M_HEADS),
        in_specs=[
            pl.BlockSpec((seq, M_DQK), lambda b, h: (b, h)),
            pl.BlockSpec((seq, M_DQK), lambda b, h: (b, qk_blocks + h)),
            pl.BlockSpec((seq, M_DV), lambda b, h: (b, v_block0 + h)),
            pl.BlockSpec((seq, M_DV), lambda b, h: (b, og_block0 + h)),
            pl.BlockSpec((seq, LANES), lambda b, h: (b, 0)),
            pl.BlockSpec((1, LANES), lambda b, h: (0, 0)),
            pl.BlockSpec((CONV_K, M_DQK), lambda b, h: (0, h)),
            pl.BlockSpec((1, M_DQK), lambda b, h: (0, h)),
            pl.BlockSpec((CONV_K, M_DQK), lambda b, h: (0, qk_blocks + h)),
            pl.BlockSpec((1, M_DQK), lambda b, h: (0, qk_blocks + h)),
            pl.BlockSpec((1, M_DV), lambda b, h: (0, h)),
        ],
        out_specs=pl.BlockSpec((seq, M_DV), lambda b, h: (b, h)),
        scratch_shapes=[pltpu.VMEM((seq, M_DQK), BF16), pltpu.VMEM((seq, M_DQK), F32),
                        pltpu.VMEM((seq, M_DV + LANES), BF16), pltpu.VMEM((GATE_ROWS, seq), F32),
                        pltpu.VMEM((GATE_ROWS, seq), F32), pltpu.VMEM((LANES, seq), F32)],
        compiler_params=_params(("parallel", "parallel"), vmem),
        name="mlstm",
    )(z, z, z, z, gates, gate_bias, conv_w, conv_b, conv_w, conv_b, norm)


def _t5_bucket_np(n):
    nf = np.maximum(n, 1).astype(np.float64)
    val = np.log(nf / MAX_EXACT) / math.log(MAX_DISTANCE / MAX_EXACT) * (NUM_BUCKETS - MAX_EXACT)
    frac = val[(n > MAX_EXACT) & (n < MAX_DISTANCE)]
    assert np.all(np.abs(frac - np.round(frac)) > 1e-3)
    large = np.minimum(MAX_EXACT + val.astype(np.int64), NUM_BUCKETS - 1)
    return np.where(n < MAX_EXACT, n, large)


def _bucket_tile(blk):
    rel = np.arange(blk)[None, :] + blk - np.arange(2 * blk)[:, None]
    return np.where(rel >= 0, _t5_bucket_np(np.maximum(rel, 0)), -1).astype(np.int32)


def _bias_body(rb_ref, bucket_ref, t_ref):
    h = pl.program_id(0)
    bucket = bucket_ref[...]
    acc = jnp.zeros(bucket.shape, F32)
    for b in range(NUM_BUCKETS):
        acc = jnp.where(bucket == b, rb_ref[h * NUM_BUCKETS + b], acc)
    far = rb_ref[h * NUM_BUCKETS + NUM_BUCKETS - 1]
    t_ref[0] = jnp.where(bucket < 0, -jnp.inf, (acc - far) * LOG2E)


def _bias_tiles(rel_bias, blk):
    assert blk >= MAX_DISTANCE
    bucket = jnp.asarray(_bucket_tile(blk))
    return pl.pallas_call(
        _bias_body,
        out_shape=jax.ShapeDtypeStruct((DA_HEADS, 2 * blk, blk), F32),
        grid=(DA_HEADS,),
        in_specs=[pl.BlockSpec(memory_space=pltpu.SMEM),
                  pl.BlockSpec((2 * blk, blk), lambda h: (0, 0))],
        out_specs=pl.BlockSpec((1, 2 * blk, blk), lambda h: (h, 0, 0)),
        compiler_params=_params(("arbitrary",), 16 << 20),
        name="t5_bias_tiles",
    )(rel_bias.T.reshape(-1), bucket)


VT_ROWS = DA_DV + 16


def _diffattn_body(lam_ref, q_ref, k_ref, v_ref, t_ref, g_ref, o_ref, vt_ref, *, blk, lam_init):
    nb = q_ref.shape[0] // blk
    ones_rows = (lax.broadcasted_iota(jnp.int32, (VT_ROWS - DA_DV, blk), 0) == 0).astype(F32)
    for c in range(nb):
        vt = v_ref[c * blk:(c + 1) * blk, :].astype(F32).T
        vt_ref[c] = jnp.concatenate([vt, ones_rows], axis=0).astype(BF16)

    lp = lam_ref[...]
    lam = (jnp.exp(jnp.sum(lp[0:1] * lp[1:2], axis=-1, keepdims=True))
           - jnp.exp(jnp.sum(lp[2:3] * lp[3:4], axis=-1, keepdims=True)) + lam_init)
    sub = lax.broadcasted_iota(jnp.int32, (LANES, blk), 0)
    gain = g_ref[...]

    def query_maps(qi):
        qt = q_ref[qi * blk:(qi + 1) * blk, :].astype(F32).T
        return (jnp.where(sub < DA_D, qt, 0.0).astype(BF16), jnp.where(sub >= DA_D, qt, 0.0).astype(BF16))

    def scores(qts, qi, kc):
        kb = k_ref[kc * blk:(kc + 1) * blk, :]
        s = [_dot(kb, qt_m) for qt_m in qts]
        if kc >= qi - 1:
            bias = t_ref[0, blk:, :] if kc == qi else t_ref[0, :blk, :]
            s = [s_m + bias for s_m in s]
        return s

    units = [(qi, kc) for qi in range(nb) for kc in range(qi, -1, -1)]
    qts_of = {}

    def unit_scores(u):
        qi, kc = units[u]
        if qi not in qts_of:
            qts_of.clear()
            qts_of[qi] = query_maps(qi)
        return scores(qts_of[qi], qi, kc)

    ahead = [unit_scores(u) for u in range(min(DA_LOOKAHEAD, len(units)))]
    state = [None, None]
    for u, (qi, kc) in enumerate(units):
        s_cur = ahead.pop(0)
        if u + DA_LOOKAHEAD < len(units):
            ahead.append(unit_scores(u + DA_LOOKAHEAD))
        for a in range(2):
            s = s_cur[a]
            blk_max = jnp.max(s, axis=0, keepdims=True)
            if state[a] is None:
                m_new = blk_max
                acc = _dot(vt_ref[kc], jnp.exp2(s - m_new).astype(BF16))
            else:
                m, acc = state[a]
                m_new = jnp.maximum(m, blk_max)
                acc = acc * jnp.exp2(m - m_new) + _dot(vt_ref[kc], jnp.exp2(s - m_new).astype(BF16))
            state[a] = (m_new, acc)
        if kc == 0:
            outs = [acc[:DA_DV] / acc[DA_DV:DA_DV + 1] for _, acc in state]
            o = (outs[0] - lam * outs[1]).T
            o_ref[qi * blk:(qi + 1) * blk, :] = (_rmsnorm(o, gain) * (1.0 - lam_init)).astype(o_ref.dtype)
            state = [None, None]


def _diffattn(z, tiles, lam_p, subln, *, batch, seq, lam_init, col0):
    blk = tiles.shape[2]
    q0 = col0 // LANES
    k0 = q0 + DA_HEADS
    v0 = k0 + DA_HEADS
    vmem = 8 * seq * LANES * 2 + seq * VT_ROWS * 2 + 4 * blk * blk * 4 + 24 * blk * blk * 4 + (8 << 20)
    return pl.pallas_call(
        functools.partial(_diffattn_body, blk=blk, lam_init=lam_init),
        out_shape=jax.ShapeDtypeStruct((batch * seq, DA_HEADS * DA_DV), BF16),
        grid=(batch, DA_HEADS),
        in_specs=[
            pl.BlockSpec((4, DA_D), lambda b, h: (0, 0)),
            pl.BlockSpec((seq, LANES), lambda b, h: (b, q0 + h)),
            pl.BlockSpec((seq, LANES), lambda b, h: (b, k0 + h)),
            pl.BlockSpec((seq, LANES), lambda b, h: (b, v0 + h)),
            pl.BlockSpec((1, 2 * blk, blk), lambda b, h: (h, 0, 0)),
            pl.BlockSpec((1, DA_DV), lambda b, h: (0, 0)),
        ],
        out_specs=pl.BlockSpec((seq, DA_DV), lambda b, h: (b, h)),
        scratch_shapes=[pltpu.VMEM((seq // blk, VT_ROWS, blk), BF16)],
        compiler_params=_params(("parallel", "parallel"), vmem),
        name="diffattn",
    )(lam_p, z, z, z, tiles, subln)


def _xattn_body(q_ref, k_ref, v_ref, o_ref, *, blk):
    nb = q_ref.shape[0] // blk
    kb = k_ref[...]
    vb = v_ref[...]
    s_next = _dot_nt(q_ref[0:blk, :], kb)
    for j in range(nb):
        s = s_next
        if j + 1 < nb:
            s_next = _dot_nt(q_ref[(j + 1) * blk:(j + 2) * blk, :], kb)
        p = jnp.exp2(s - jnp.max(s, axis=-1, keepdims=True))
        o = _dot(p.astype(BF16), vb) / jnp.sum(p, axis=-1, keepdims=True)
        o_ref[j * blk:(j + 1) * blk, :] = o.astype(o_ref.dtype)


def _xattn(q, kv, *, batch, seq, mem_len, blk=256):
    d = q.shape[1]
    dh = d // X_HEADS
    blk = min(blk, seq)
    vmem = 4 * seq * dh * 2 + 4 * mem_len * dh * 2 + 8 * blk * (mem_len + dh) * 4 + (8 << 20)
    return pl.pallas_call(
        functools.partial(_xattn_body, blk=blk),
        out_shape=jax.ShapeDtypeStruct((batch * seq, d), BF16),
        grid=(batch, X_HEADS),
        in_specs=[
            pl.BlockSpec((seq, dh), lambda b, h: (b, h)),
            pl.BlockSpec((mem_len, dh), lambda b, h: (b, h)),
            pl.BlockSpec((mem_len, dh), lambda b, h: (b, X_HEADS + h)),
        ],
        out_specs=pl.BlockSpec((seq, dh), lambda b, h: (b, h)),
        compiler_params=_params(("parallel", "parallel"), vmem),
        name="xattn",
    )(q, kv, kv)


def kernel(x, mem, rel_bias, ffn1_norm_pre, ffn1_norm_post, ffn1_w_gate, ffn1_w_up, ffn1_w_down, mix_norm_pre, mix_norm_post, w_in, conv_w, conv_b, b_igate, b_fgate, mlstm_norm, diff_lambda, diff_subln, w_out, xattn_norm_pre, xattn_norm_post, mem_norm, xattn_wq, xattn_wk, xattn_wv, xattn_wo, ffn2_norm_pre, ffn2_norm_post, ffn2_w_gate, ffn2_w_up, ffn2_w_down):
    batch, seq, d = x.shape
    mem_len = mem.shape[1]
    depth = w_in.shape[0]
    n_gate = 2 * M_HEADS
    da_qw = DA_HEADS * 2 * DA_D
    gate0 = 2 * M_HEADS * M_DQK + 2 * M_HEADS * M_DV
    xf = x.reshape(batch * seq, d)
    memf = mem.reshape(batch * mem_len, d)
    row = lambda v: v.reshape(1, -1).astype(F32)

    tiles = _bias_tiles(rel_bias, min(DA_BLOCK, seq))
    w_main, w_gate = _cast_w_in(w_in, gate0=gate0, n_gate=n_gate, scaled=da_qw, scale=DA_D ** -0.5 * LOG2E)
    ffn1 = (_cast_blocks([ffn1_w_gate, ffn1_w_up], FFN_BLOCK, interleave=True), _cast_blocks([ffn1_w_down]))
    ffn2 = (_cast_blocks([ffn2_w_gate, ffn2_w_up], FFN_BLOCK, interleave=True), _cast_blocks([ffn2_w_down]))
    w_o = _cast_blocks([w_out])
    w_q = _cast_blocks([xattn_wq], PROJ_BLOCK, scale=(d // X_HEADS) ** -0.5 * LOG2E)
    w_kv = _cast_blocks([xattn_wk, xattn_wv], PROJ_BLOCK)
    w_xo = _cast_blocks([xattn_wo])

    xn = None
    for l in range(depth):
        lam_init = 0.8 - 0.6 * math.exp(-0.3 * l)
        xf, _ = _ffn(xf, row(ffn1_norm_pre[l]), row(ffn1_norm_post[l]), *ffn1, l, xn=xn)

        z, gates = _normproj(xf, row(mix_norm_pre[l]), w_main, l, w_gate)
        gate_bias = jnp.pad(jnp.concatenate([b_igate[l], b_fgate[l]]), (0, LANES - n_gate)).reshape(1, LANES)
        y_m = _mlstm(z, gates, gate_bias.astype(F32), conv_w[l].astype(F32), row(conv_b[l]),
                     row(mlstm_norm[l]), batch=batch, seq=seq)
        y_d = _diffattn(z, tiles, diff_lambda[l].astype(F32), row(diff_subln[l]),
                        batch=batch, seq=seq, lam_init=lam_init, col0=gate0)
        xf, _ = _outproj(xf, row(mix_norm_post[l]), [y_m, y_d], w_o, l)

        q = _normproj(xf, row(xattn_norm_pre[l]), w_q, l)
        kv = _normproj(memf, row(mem_norm[l]), w_kv, l)
        c = _xattn(q, kv, batch=batch, seq=seq, mem_len=mem_len)
        xf, xn2 = _outproj(xf, row(xattn_norm_post[l]), [c], w_xo, l, next_gain=row(ffn2_norm_pre[l]))

        next_gain = row(ffn1_norm_pre[l + 1]) if l + 1 < depth else None
        xf, xn = _ffn(xf, row(ffn2_norm_pre[l]), row(ffn2_norm_post[l]), *ffn2, l, xn=xn2, next_gain=next_gain)
    return xf.reshape(batch, seq, d)
```

```python
import functools
import math

import numpy as np
import jax
import jax.numpy as jnp
from jax import lax
from jax.experimental import pallas as pl
from jax.experimental.pallas import tpu as pltpu

F32 = jnp.float32
BF16 = jnp.bfloat16
EPS = 1e-6
LOG2E = math.log2(math.e)

M_HEADS = 4
M_DQK = 128
M_DV = 256
CONV_K = 4
DA_HEADS = 8
DA_D = 64
DA_DV = 128
X_HEADS = 4
NUM_BUCKETS = 32
MAX_EXACT = NUM_BUCKETS // 2
MAX_DISTANCE = 128

LANES = 128
MXU_N = 256
V7X_VMEM_BYTES = 64 * 2 ** 20
MLSTM_CHUNK = 256
DA_BLOCK = 256
DA_LOOKAHEAD = 2
FFN_BLOCK = 1408
FFN_ROWS = 512
FFN_ROWS_PRENORMED = 1024
DOWN_ROWS = 512
DOWN_ROWS_HANDOFF = 256
PROJ_BLOCK = 1024
INPROJ_BLOCK = 1536


def _params(semantics, vmem_bytes):
    assert vmem_bytes <= V7X_VMEM_BYTES
    return pltpu.CompilerParams(dimension_semantics=semantics, vmem_limit_bytes=int(vmem_bytes))


def _rmsnorm(xf, g):
    return xf * lax.rsqrt(jnp.mean(xf * xf, axis=-1, keepdims=True) + EPS) * g


def _log_sigmoid(x):
    return jnp.minimum(x, 0.0) - jnp.log1p(jnp.exp(-jnp.abs(x)))


def _dot(a, b):
    return jnp.dot(a, b, preferred_element_type=F32)


def _dot_nt(a, b):
    return lax.dot_general(a, b, (((1,), (1,)), ((), ())), preferred_element_type=F32)


def _cast_body(*refs, bn, scale, interleave):
    *in_refs, o_ref = refs
    cast = lambda blk: (blk if scale is None else blk * scale).astype(BF16)
    if interleave:
        for b in range(in_refs[0].shape[2] // bn):
            for t, x_ref in enumerate(in_refs):
                o_ref[0, b, :, t * bn:(t + 1) * bn] = cast(x_ref[0, :, b * bn:(b + 1) * bn])
    else:
        c = 0
        for x_ref in in_refs:
            for b in range(x_ref.shape[2] // bn):
                o_ref[0, c] = cast(x_ref[0, :, b * bn:(b + 1) * bn])
                c += 1


def _cast_blocks(ws, bn=None, *, scale=None, interleave=False, kb=256):
    depth, k, _ = ws[0].shape
    bn = bn or ws[0].shape[2]
    n_total = sum(w.shape[2] for w in ws)
    width = bn * len(ws) if interleave else bn
    kb = min(kb, k)
    vmem = 2 * kb * n_total * (4 + 2) + (4 << 20)
    return pl.pallas_call(
        functools.partial(_cast_body, bn=bn, scale=scale, interleave=interleave),
        out_shape=jax.ShapeDtypeStruct((depth, n_total // width, k, width), BF16),
        grid=(depth, k // kb),
        in_specs=[pl.BlockSpec((1, kb, w.shape[2]), lambda l, i: (l, i, 0)) for w in ws],
        out_specs=pl.BlockSpec((1, n_total // width, kb, width), lambda l, i: (l, 0, i, 0)),
        compiler_params=_params(("parallel", "parallel"), vmem),
        name="cast_blocks",
    )(*ws)


def _cast_w_in_body(x_ref, o_ref, gate_ref, *, gate0, n_gate, bn, scaled, scale):
    n, kb = x_ref.shape[1:]
    starts = list(range(0, gate0, bn)) + list(range(gate0 + n_gate, n, bn))
    q0 = gate0 + n_gate
    for c, r0 in enumerate(starts):
        blk = x_ref[0, r0:r0 + bn, :]
        lo, hi = max(q0, r0) - r0, min(q0 + scaled, r0 + bn) - r0
        if lo < hi:
            pieces = [(0, lo, None), (lo, hi, scale), (hi, bn, None)]
            blk = jnp.concatenate([blk[a:b] if s is None else blk[a:b] * s for a, b, s in pieces if a < b], axis=0)
        o_ref[0, c] = blk.T.astype(BF16)
    gates = jnp.concatenate([x_ref[0, gate0:gate0 + n_gate, :], jnp.zeros((LANES - n_gate, kb), F32)], axis=0)
    gate_ref[0] = gates.T.astype(BF16)


def _cast_w_in(w_in, *, gate0, n_gate, scaled, scale, bn=INPROJ_BLOCK, kb=256):
    depth, k, n = w_in.shape
    n_main = n - n_gate
    kb = min(kb, k)
    assert gate0 % bn == 0 and (n - gate0 - n_gate) % bn == 0 and (gate0 + n_gate) % 8 == 0
    vmem = 2 * kb * n * 4 + 2 * kb * n_main * 2 + 3 * kb * n * 4 + (4 << 20)
    return pl.pallas_call(
        functools.partial(_cast_w_in_body, gate0=gate0, n_gate=n_gate, bn=bn, scaled=scaled, scale=scale),
        out_shape=[jax.ShapeDtypeStruct((depth, n_main // bn, k, bn), BF16),
                   jax.ShapeDtypeStruct((depth, k, LANES), BF16)],
        grid=(depth, k // kb),
        in_specs=[pl.BlockSpec((1, n, kb), lambda l, i: (l, 0, i))],
        out_specs=[pl.BlockSpec((1, n_main // bn, kb, bn), lambda l, i: (l, 0, i, 0)),
                   pl.BlockSpec((1, kb, LANES), lambda l, i: (l, i, 0))],
        compiler_params=_params(("parallel", "parallel"), vmem),
        name="cast_w_in",
    )(jnp.swapaxes(w_in, 1, 2))


def _ffn_up_body(*refs, prenormed):
    if prenormed:
        xn_ref, w_ref, h_ref = refs
    else:
        x_ref, g_ref, w_ref, h_ref, xn_ref = refs

        @pl.when(pl.program_id(1) == 0)
        def _():
            xn_ref[...] = _rmsnorm(x_ref[...], g_ref[...]).astype(BF16)

    bf = h_ref.shape[1]
    gu = _dot(xn_ref[...], w_ref[0, 0])
    g, u = gu[:, :bf], gu[:, bf:]
    h_ref[...] = (g * jax.nn.sigmoid(g) * u).astype(h_ref.dtype)


def _ffn(x, g_pre, g_post, w_gate_up, w_down, layer, *, xn=None, next_gain=None):
    m, d = x.shape
    nf, bf = w_gate_up.shape[1], w_gate_up.shape[3] // 2
    f = nf * bf
    prenormed = xn is not None
    bm = min(FFN_ROWS_PRENORMED if prenormed else FFN_ROWS, m)
    assert m % bm == 0
    w_spec = pl.BlockSpec((1, 1, d, 2 * bf), lambda i, j: (layer, j, 0, 0))
    row_spec = pl.BlockSpec((bm, d), lambda i, j: (i, 0))
    if prenormed:
        in_specs, args, scratch = [row_spec, w_spec], (xn, w_gate_up), []
        vmem_up = 2 * bm * d * 2 + 4 * d * bf * 2 + 2 * bm * bf * 2 + 3 * bm * bf * 4 + (4 << 20)
    else:
        in_specs = [row_spec, pl.BlockSpec((1, d), lambda i, j: (0, 0)), w_spec]
        args, scratch = (x, g_pre, w_gate_up), [pltpu.VMEM((bm, d), BF16)]
        vmem_up = 2 * bm * d * 4 + bm * d * 2 + 4 * d * bf * 2 + 2 * bm * bf * 2 + 3 * bm * bf * 4 + (4 << 20)
    hidden = pl.pallas_call(
        functools.partial(_ffn_up_body, prenormed=prenormed),
        out_shape=jax.ShapeDtypeStruct((m, f), BF16),
        grid=(m // bm, nf),
        in_specs=in_specs,
        out_specs=pl.BlockSpec((bm, bf), lambda i, j: (i, j)),
        scratch_shapes=scratch,
        compiler_params=_params(("parallel", "arbitrary"), vmem_up),
        name="ffn_up",
    )(*args)
    return _outproj(x, g_post, [hidden], w_down, layer, res_scale=0.5, next_gain=next_gain,
                    bm=DOWN_ROWS if next_gain is None else DOWN_ROWS_HANDOFF)


def _normproj_body(x_ref, g_ref, w_ref, *rest, with_gate):
    if with_gate:
        wgate_ref, o_ref, gate_ref, xn_ref = rest
    else:
        o_ref, xn_ref = rest
    j = pl.program_id(1)

    @pl.when(j == 0)
    def _():
        xn = _rmsnorm(x_ref[...], g_ref[...]).astype(BF16)
        xn_ref[...] = xn
        if with_gate:
            gate_ref[...] = _dot(xn, wgate_ref[0])

    o_ref[...] = _dot(xn_ref[...], w_ref[0, 0]).astype(o_ref.dtype)


def _normproj(x, g, w, layer, w_gate=None, *, bm=1024):
    m, d = x.shape
    nb, bn = w.shape[1], w.shape[3]
    bm = min(bm, m)
    assert m % bm == 0
    with_gate = w_gate is not None
    in_specs = [
        pl.BlockSpec((bm, d), lambda i, j: (i, 0)),
        pl.BlockSpec((1, d), lambda i, j: (0, 0)),
        pl.BlockSpec((1, 1, d, bn), lambda i, j: (layer, j, 0, 0)),
    ]
    out_shape = [jax.ShapeDtypeStruct((m, nb * bn), BF16)]
    out_specs = [pl.BlockSpec((bm, bn), lambda i, j: (i, j))]
    args = [x, g, w]
    if with_gate:
        in_specs.append(pl.BlockSpec((1, d, LANES), lambda i, j: (layer, 0, 0)))
        out_shape.append(jax.ShapeDtypeStruct((m, LANES), F32))
        out_specs.append(pl.BlockSpec((bm, LANES), lambda i, j: (i, 0)))
        args.append(w_gate)
    vmem = 2 * bm * d * 4 + bm * d * 2 + 2 * d * bn * 2 + 2 * bm * bn * 2 + bm * bn * 4 + (6 << 20)
    out = pl.pallas_call(
        functools.partial(_normproj_body, with_gate=with_gate),
        out_shape=out_shape,
        grid=(m // bm, nb),
        in_specs=in_specs,
        out_specs=out_specs,
        scratch_shapes=[pltpu.VMEM((bm, d), BF16)],
        compiler_params=_params(("parallel", "arbitrary"), vmem),
        name="normproj_gate" if with_gate else "normproj",
    )(*args)
    return out if with_gate else out[0]


def _outproj_body(*refs, n_parts, res_scale, handoff):
    a_refs = refs[:n_parts]
    w_refs = refs[n_parts:2 * n_parts]
    if handoff:
        x_ref, g_ref, gnext_ref, o_ref, xn_ref = refs[2 * n_parts:]
    else:
        x_ref, g_ref, o_ref = refs[2 * n_parts:]
    y = _dot(a_refs[0][...], w_refs[0][0, 0])
    for a_ref, w_ref in zip(a_refs[1:], w_refs[1:]):
        y += _dot(a_ref[...], w_ref[0, 0])
    out = x_ref[...] + res_scale * _rmsnorm(y, g_ref[...])
    o_ref[...] = out
    if handoff:
        xn_ref[...] = _rmsnorm(out, gnext_ref[...]).astype(xn_ref.dtype)


def _outproj(x, g, acts, w, layer, *, res_scale=1.0, next_gain=None, bm=512):
    m, d = x.shape
    bm = min(bm, m)
    assert m % bm == 0
    handoff = next_gain is not None
    k_total = w.shape[2]
    in_specs = [pl.BlockSpec((bm, a.shape[1]), lambda i: (i, 0)) for a in acts]
    row0 = 0
    for a in acts:
        k = a.shape[1]
        assert row0 % k == 0
        in_specs.append(pl.BlockSpec((1, 1, k, d), functools.partial(lambda i, rb: (layer, 0, rb, 0), rb=row0 // k),
                                     pipeline_mode=pl.Buffered(1)))
        row0 += k
    assert row0 == k_total
    gain_spec = pl.BlockSpec((1, d), lambda i: (0, 0))
    row_spec = pl.BlockSpec((bm, d), lambda i: (i, 0))
    in_specs += [row_spec, gain_spec] + ([gain_spec] if handoff else [])
    out_shape = [jax.ShapeDtypeStruct((m, d), F32)] + ([jax.ShapeDtypeStruct((m, d), BF16)] if handoff else [])
    vmem = k_total * d * 2 + 2 * bm * k_total * 2 + 5 * bm * d * 4 + (2 * bm * d * 2 if handoff else 0) + (4 << 20)
    out = pl.pallas_call(
        functools.partial(_outproj_body, n_parts=len(acts), res_scale=res_scale, handoff=handoff),
        out_shape=out_shape,
        grid=(m // bm,),
        in_specs=in_specs,
        out_specs=[row_spec] * len(out_shape),
        compiler_params=_params(("parallel",), vmem),
        name="outproj",
    )(*acts, *([w] * len(acts)), x, g, *([next_gain] if handoff else []))
    return (out[0], out[1]) if handoff else (out[0], None)


def _conv_silu(raw, w, b):
    rows = lax.broadcasted_iota(jnp.int32, raw.shape, 0)
    acc = raw * w[CONV_K - 1:CONV_K, :] + b
    for shift in range(1, CONV_K):
        shifted = jnp.where(rows >= shift, pltpu.roll(raw, shift, 0), 0.0)
        acc += shifted * w[CONV_K - 1 - shift:CONV_K - shift, :]
    return acc * jax.nn.sigmoid(acc)


GATE_ROWS = 16


def _split3(x):
    hi = x.astype(BF16)
    r1 = x - hi.astype(F32)
    mid = r1.astype(BF16)
    return hi, mid, (r1 - mid.astype(F32)).astype(BF16)


def _mlstm_body(q_ref, k_ref, v_ref, og_ref, gates_ref, gbias_ref, cwq_ref, cbq_ref, cwk_ref, cbk_ref,
                norm_ref, o_ref, qs_ref, ks_ref, vx_ref, gt_ref, bt_ref, row_ref, *, chunk):
    s = q_ref.shape[0]
    nc = s // chunk
    head = pl.program_id(1)
    ext = vx_ref.shape[1]
    qs_ref[...] = _conv_silu(q_ref[...].astype(F32), cwq_ref[...], cbq_ref[...]).astype(BF16)
    ks_ref[...] = _conv_silu(k_ref[...].astype(F32), cwk_ref[...], cbk_ref[...]) * (M_DQK ** -0.5)
    ones_col = (lax.broadcasted_iota(jnp.int32, (s, ext - M_DV), 1) == 0).astype(BF16)
    vx_ref[...] = jnp.concatenate([v_ref[...], ones_col], axis=1)

    upper = (lax.broadcasted_iota(jnp.int32, (chunk, chunk), 0)
             <= lax.broadcasted_iota(jnp.int32, (chunk, chunk), 1)).astype(BF16)
    for c in range(nc):
        cols = slice(c * chunk, (c + 1) * chunk)
        gt = (gates_ref[cols, :] + gbias_ref[...]).T[:GATE_ROWS]
        gt_ref[:, cols] = gt
        bt_ref[:, cols] = sum(_dot(piece, upper) for piece in _split3(_log_sigmoid(gt)))

    i_row = gt_ref[pl.ds(head, 1), :]
    b_row = bt_ref[pl.ds(M_HEADS + head, 1), :]
    r_row = i_row - b_row
    pos = lax.broadcasted_iota(jnp.int32, (1, s), 1) % chunk
    cmax = r_row
    shift = 1
    while shift < chunk:
        cmax = jnp.maximum(cmax, jnp.where(pos >= shift, pltpu.roll(cmax, shift, 1), -jnp.inf))
        shift *= 2

    row_ref[...] = jnp.zeros(row_ref.shape, F32)
    m = jnp.zeros((1, 1), F32)
    ws_rows, decays = [], []
    for c in range(nc):
        cols = slice(c * chunk, (c + 1) * chunk)
        a = jnp.maximum(m, cmax[:, cols])
        a_last = a[:, chunk - 1:]
        b_c = b_row[:, cols]
        row_ref[0:1, cols] = a
        row_ref[1:2, cols] = jnp.exp(m - a)
        row_ref[2:3, cols] = jnp.exp(-(b_c + a))
        ws_rows.append(jnp.exp(r_row[:, cols] - a_last))
        decays.append(jnp.exp(m - a_last))
        m = b_c[:, chunk - 1:] + a_last

    causal = (lax.broadcasted_iota(jnp.int32, (chunk, chunk), 0)
              >= lax.broadcasted_iota(jnp.int32, (chunk, chunk), 1))
    gain = norm_ref[...]

    def independent(c):
        rows = slice(c * chunk, (c + 1) * chunk)
        kf = ks_ref[rows, :]
        s_qk = _dot_nt(qs_ref[rows, :], kf.astype(BF16))
        kv = _dot((kf.T * ws_rows[c]).astype(BF16), vx_ref[rows, :])
        return s_qk, kv

    cmat = None
    nxt = independent(0)
    for c in range(nc):
        rows = slice(c * chunk, (c + 1) * chunk)
        s_qk, kv = nxt
        if c + 1 < nc:
            nxt = independent(c + 1)
        col = row_ref[:, rows].T
        a_col, wi_col, en_col = col[:, 0:1], col[:, 1:2], col[:, 2:3]
        p = jnp.exp(jnp.where(causal, r_row[:, rows] - a_col, -jnp.inf)) * s_qk
        hx = _dot(p.astype(BF16), vx_ref[rows, :])
        if cmat is not None:
            hx += wi_col * _dot(qs_ref[rows, :], cmat.astype(BF16))
        hc = hx[:, :M_DV] / jnp.maximum(jnp.abs(hx[:, M_DV:M_DV + 1]), en_col)
        cmat = kv if cmat is None else decays[c] * cmat + kv
        y = _rmsnorm(hc, gain) * jax.nn.sigmoid(og_ref[rows, :].astype(F32))
        o_ref[rows, :] = y.astype(o_ref.dtype)


def _mlstm(z, gates, gate_bias, conv_w, conv_b, norm, *, batch, seq):
    chunk = min(MLSTM_CHUNK, seq)
    qk_blocks = M_HEADS
    v_block0 = 2 * M_HEADS * M_DQK // M_DV
    og_block0 = v_block0 + M_HEADS
    vmem = seq * (2 * 2 * M_DQK * 2 + 3 * 2 * M_DV * 2 + 3 * LANES * 4 + M_DQK * 6) + (24 << 20)
    return pl.pallas_call(
        functools.partial(_mlstm_body, chunk=chunk),
        out_shape=jax.ShapeDtypeStruct((batch * seq, M_HEADS * M_DV), BF16),
        grid=(batch, M_HEADS),
        in_specs=[
            pl.BlockSpec((seq, M_DQK), lambda b, h: (b, h)),
            pl.BlockSpec((seq, M_DQK), lambda b, h: (b, qk_blocks + h)),
            pl.BlockSpec((seq, M_DV), lambda b, h: (b, v_block0 + h)),
            pl.BlockSpec((seq, M_DV), lambda b, h: (b, og_block0 + h)),
            pl.BlockSpec((seq, LANES), lambda b, h: (b, 0)),
            pl.BlockSpec((1, LANES), lambda b, h: (0, 0)),
            pl.BlockSpec((CONV_K, M_DQK), lambda b, h: (0, h)),
            pl.BlockSpec((1, M_DQK), lambda b, h: (0, h)),
            pl.BlockSpec((CONV_K, M_DQK), lambda b, h: (0, qk_blocks + h)),
            pl.BlockSpec((1, M_DQK), lambda b, h: (0, qk_blocks + h)),
            pl.BlockSpec((1, M_DV), lambda b, h: (0, h)),
        ],
        out_specs=pl.BlockSpec((seq, M_DV), lambda b, h: (b, h)),
        scratch_shapes=[pltpu.VMEM((seq, M_DQK), BF16), pltpu.VMEM((seq, M_DQK), F32),
                        pltpu.VMEM((seq, M_DV + LANES), BF16), pltpu.VMEM((GATE_ROWS, seq), F32),
                        pltpu.VMEM((GATE_ROWS, seq), F32), pltpu.VMEM((LANES, seq), F32)],
        compiler_params=_params(("parallel", "parallel"), vmem),
        name="mlstm",
    )(z, z, z, z, gates, gate_bias, conv_w, conv_b, conv_w, conv_b, norm)


def _t5_bucket_np(n):
    nf = np.maximum(n, 1).astype(np.float64)
    val = np.log(nf / MAX_EXACT) / math.log(MAX_DISTANCE / MAX_EXACT) * (NUM_BUCKETS - MAX_EXACT)
    frac = val[(n > MAX_EXACT) & (n < MAX_DISTANCE)]
    assert np.all(np.abs(frac - np.round(frac)) > 1e-3)
    large = np.minimum(MAX_EXACT + val.astype(np.int64), NUM_BUCKETS - 1)
    return np.where(n < MAX_EXACT, n, large)


def _bucket_tile(blk):
    rel = np.arange(blk)[None, :] + blk - np.arange(2 * blk)[:, None]
    return np.where(rel >= 0, _t5_bucket_np(np.maximum(rel, 0)), -1).astype(np.int32)


def _bias_body(rb_ref, bucket_ref, t_ref):
    h = pl.program_id(0)
    bucket = bucket_ref[...]
    acc = jnp.zeros(bucket.shape, F32)
    for b in range(NUM_BUCKETS):
        acc = jnp.where(bucket == b, rb_ref[h * NUM_BUCKETS + b], acc)
    far = rb_ref[h * NUM_BUCKETS + NUM_BUCKETS - 1]
    t_ref[0] = jnp.where(bucket < 0, -jnp.inf, (acc - far) * LOG2E)


def _bias_tiles(rel_bias, blk):
    assert blk >= MAX_DISTANCE
    bucket = jnp.asarray(_bucket_tile(blk))
    return pl.pallas_call(
        _bias_body,
        out_shape=jax.ShapeDtypeStruct((DA_HEADS, 2 * blk, blk), F32),
        grid=(DA_HEADS,),
        in_specs=[pl.BlockSpec(memory_space=pltpu.SMEM),
                  pl.BlockSpec((2 * blk, blk), lambda h: (0, 0))],
        out_specs=pl.BlockSpec((1, 2 * blk, blk), lambda h: (h, 0, 0)),
        compiler_params=_params(("arbitrary",), 16 << 20),
        name="t5_bias_tiles",
    )(rel_bias.T.reshape(-1), bucket)


VT_ROWS = DA_DV + 16


def _diffattn_body(lam_ref, q_ref, k_ref, v_ref, t_ref, g_ref, o_ref, vt_ref, *, blk, lam_init):
    nb = q_ref.shape[0] // blk
    ones_rows = (lax.broadcasted_iota(jnp.int32, (VT_ROWS - DA_DV, blk), 0) == 0).astype(F32)
    for c in range(nb):
        vt = v_ref[c * blk:(c + 1) * blk, :].astype(F32).T
        vt_ref[c] = jnp.concatenate([vt, ones_rows], axis=0).astype(BF16)

    lp = lam_ref[...]
    lam = (jnp.exp(jnp.sum(lp[0:1] * lp[1:2], axis=-1, keepdims=True))
           - jnp.exp(jnp.sum(lp[2:3] * lp[3:4], axis=-1, keepdims=True)) + lam_init)
    sub = lax.broadcasted_iota(jnp.int32, (LANES, blk), 0)
    gain = g_ref[...]

    def query_maps(qi):
        qt = q_ref[qi * blk:(qi + 1) * blk, :].astype(F32).T
        return (jnp.where(sub < DA_D, qt, 0.0).astype(BF16), jnp.where(sub >= DA_D, qt, 0.0).astype(BF16))

    def scores(qts, qi, kc):
        kb = k_ref[kc * blk:(kc + 1) * blk, :]
        s = [_dot(kb, qt_m) for qt_m in qts]
        if kc >= qi - 1:
            bias = t_ref[0, blk:, :] if kc == qi else t_ref[0, :blk, :]
            s = [s_m + bias for s_m in s]
        return s

    units = [(qi, kc) for qi in range(nb) for kc in range(qi, -1, -1)]
    qts_of = {}

    def unit_scores(u):
        qi, kc = units[u]
        if qi not in qts_of:
            qts_of.clear()
            qts_of[qi] = query_maps(qi)
        return scores(qts_of[qi], qi, kc)

    ahead = [unit_scores(u) for u in range(min(DA_LOOKAHEAD, len(units)))]
    state = [None, None]
    for u, (qi, kc) in enumerate(units):
        s_cur = ahead.pop(0)
        if u + DA_LOOKAHEAD < len(units):
            ahead.append(unit_scores(u + DA_LOOKAHEAD))
        for a in range(2):
            s = s_cur[a]
            blk_max = jnp.max(s, axis=0, keepdims=True)
            if state[a] is None:
                m_new = blk_max
                acc = _dot(vt_ref[kc], jnp.exp2(s - m_new).astype(BF16))
            else:
                m, acc = state[a]
                m_new = jnp.maximum(m, blk_max)
                acc = acc * jnp.exp2(m - m_new) + _dot(vt_ref[kc], jnp.exp2(s - m_new).astype(BF16))
            state[a] = (m_new, acc)
        if kc == 0:
            outs = [acc[:DA_DV] / acc[DA_DV:DA_DV + 1] for _, acc in state]
            o = (outs[0] - lam * outs[1]).T
            o_ref[qi * blk:(qi + 1) * blk, :] = (_rmsnorm(o, gain) * (1.0 - lam_init)).astype(o_ref.dtype)
            state = [None, None]


def _diffattn(z, tiles, lam_p, subln, *, batch, seq, lam_init, col0):
    blk = tiles.shape[2]
    q0 = col0 // LANES
    k0 = q0 + DA_HEADS
    v0 = k0 + DA_HEADS
    vmem = 8 * seq * LANES * 2 + seq * VT_ROWS * 2 + 4 * blk * blk * 4 + 24 * blk * blk * 4 + (8 << 20)
    return pl.pallas_call(
        functools.partial(_diffattn_body, blk=blk, lam_init=lam_init),
        out_shape=jax.ShapeDtypeStruct((batch * seq, DA_HEADS * DA_DV), BF16),
        grid=(batch, DA_HEADS),
        in_specs=[
            pl.BlockSpec((4, DA_D), lambda b, h: (0, 0)),
            pl.BlockSpec((seq, LANES), lambda b, h: (b, q0 + h)),
            pl.BlockSpec((seq, LANES), lambda b, h: (b, k0 + h)),
            pl.BlockSpec((seq, LANES), lambda b, h: (b, v0 + h)),
            pl.BlockSpec((1, 2 * blk, blk), lambda b, h: (h, 0, 0)),
            pl.BlockSpec((1, DA_DV), lambda b, h: (0, 0)),
        ],
        out_specs=pl.BlockSpec((seq, DA_DV), lambda b, h: (b, h)),
        scratch_shapes=[pltpu.VMEM((seq // blk, VT_ROWS, blk), BF16)],
        compiler_params=_params(("parallel", "parallel"), vmem),
        name="diffattn",
    )(lam_p, z, z, z, tiles, subln)


def _xattn_body(q_ref, k_ref, v_ref, o_ref, *, blk):
    nb = q_ref.shape[0] // blk
    kb = k_ref[...]
    vb = v_ref[...]
    s_next = _dot_nt(q_ref[0:blk, :], kb)
    for j in range(nb):
        s = s_next
        if j + 1 < nb:
            s_next = _dot_nt(q_ref[(j + 1) * blk:(j + 2) * blk, :], kb)
        p = jnp.exp2(s - jnp.max(s, axis=-1, keepdims=True))
        o = _dot(p.astype(BF16), vb) / jnp.sum(p, axis=-1, keepdims=True)
        o_ref[j * blk:(j + 1) * blk, :] = o.astype(o_ref.dtype)


def _xattn(q, kv, *, batch, seq, mem_len, blk=256):
    d = q.shape[1]
    dh = d // X_HEADS
    blk = min(blk, seq)
    vmem = 4 * seq * dh * 2 + 4 * mem_len * dh * 2 + 8 * blk * (mem_len + dh) * 4 + (8 << 20)
    return pl.pallas_call(
        functools.partial(_xattn_body, blk=blk),
        out_shape=jax.ShapeDtypeStruct((batch * seq, d), BF16),
        grid=(batch, X_HEADS),
        in_specs=[
            pl.BlockSpec((seq, dh), lambda b, h: (b, h)),
            pl.BlockSpec((mem_len, dh), lambda b, h: (b, h)),
            pl.BlockSpec((mem_len, dh), lambda b, h: (b, X_HEADS + h)),
        ],
        out_specs=pl.BlockSpec((seq, dh), lambda b, h: (b, h)),
        compiler_params=_params(("parallel", "parallel"), vmem),
        name="xattn",
    )(q, kv, kv)


def kernel(x, mem, rel_bias, ffn1_norm_pre, ffn1_norm_post, ffn1_w_gate, ffn1_w_up, ffn1_w_down, mix_norm_pre, mix_norm_post, w_in, conv_w, conv_b, b_igate, b_fgate, mlstm_norm, diff_lambda, diff_subln, w_out, xattn_norm_pre, xattn_norm_post, mem_norm, xattn_wq, xattn_wk, xattn_wv, xattn_wo, ffn2_norm_pre, ffn2_norm_post, ffn2_w_gate, ffn2_w_up, ffn2_w_down):
    batch, seq, d = x.shape
    mem_len = mem.shape[1]
    depth = w_in.shape[0]
    n_gate = 2 * M_HEADS
    da_qw = DA_HEADS * 2 * DA_D
    gate0 = 2 * M_HEADS * M_DQK + 2 * M_HEADS * M_DV
    xf = x.reshape(batch * seq, d)
    memf = mem.reshape(batch * mem_len, d)
    row = lambda v: v.reshape(1, -1).astype(F32)

    tiles = _bias_tiles(rel_bias, min(DA_BLOCK, seq))
    w_main, w_gate = _cast_w_in(w_in, gate0=gate0, n_gate=n_gate, scaled=da_qw, scale=DA_D ** -0.5 * LOG2E)
    ffn1 = (_cast_blocks([ffn1_w_gate, ffn1_w_up], FFN_BLOCK, interleave=True), _cast_blocks([ffn1_w_down]))
    ffn2 = (_cast_blocks([ffn2_w_gate, ffn2_w_up], FFN_BLOCK, interleave=True), _cast_blocks([ffn2_w_down]))
    w_o = _cast_blocks([w_out])
    w_q = _cast_blocks([xattn_wq], PROJ_BLOCK, scale=(d // X_HEADS) ** -0.5 * LOG2E)
    w_kv = _cast_blocks([xattn_wk, xattn_wv], PROJ_BLOCK)
    w_xo = _cast_blocks([xattn_wo])

    xn = None
    for l in range(depth):
        lam_init = 0.8 - 0.6 * math.exp(-0.3 * l)
        xf, _ = _ffn(xf, row(ffn1_norm_pre[l]), row(ffn1_norm_post[l]), *ffn1, l, xn=xn)

        z, gates = _normproj(xf, row(mix_norm_pre[l]), w_main, l, w_gate)
        gate_bias = jnp.pad(jnp.concatenate([b_igate[l], b_fgate[l]]), (0, LANES - n_gate)).reshape(1, LANES)
        y_m = _mlstm(z, gates, gate_bias.astype(F32), conv_w[l].astype(F32), row(conv_b[l]),
                     row(mlstm_norm[l]), batch=batch, seq=seq)
        y_d = _diffattn(z, tiles, diff_lambda[l].astype(F32), row(diff_subln[l]),
                        batch=batch, seq=seq, lam_init=lam_init, col0=gate0)
        xf, _ = _outproj(xf, row(mix_norm_post[l]), [y_m, y_d], w_o, l)

        q = _normproj(xf, row(xattn_norm_pre[l]), w_q, l)
        kv = _normproj(memf, row(mem_norm[l]), w_kv, l)
        c = _xattn(q, kv, batch=batch, seq=seq, mem_len=mem_len)
        xf, xn2 = _outproj(xf, row(xattn_norm_post[l]), [c], w_xo, l, next_gain=row(ffn2_norm_pre[l]))

        next_gain = row(ffn1_norm_pre[l + 1]) if l + 1 < depth else None
        xf, xn = _ffn(xf, row(ffn2_norm_pre[l]), row(ffn2_norm_post[l]), *ffn2, l, xn=xn2, next_gain=next_gain)
    return xf.reshape(batch, seq, d)
```

```python
import functools
import math

import numpy as np
import jax
import jax.numpy as jnp
from jax import lax
from jax.experimental import pallas as pl
from jax.experimental.pallas import tpu as pltpu

F32 = jnp.float32
BF16 = jnp.bfloat16
EPS = 1e-6
LOG2E = math.log2(math.e)

M_HEADS = 4
M_DQK = 128
M_DV = 256
CONV_K = 4
DA_HEADS = 8
DA_D = 64
DA_DV = 128
X_HEADS = 4
NUM_BUCKETS = 32
MAX_EXACT = NUM_BUCKETS // 2
MAX_DISTANCE = 128

LANES = 128
MXU_N = 256
V7X_VMEM_BYTES = 64 * 2 ** 20
MLSTM_CHUNK = 256
DA_BLOCK = 256
DA_LOOKAHEAD = 3
FFN_BLOCK = 1408
FFN_ROWS = 512
FFN_ROWS_PRENORMED = 1024
DOWN_ROWS = 512
ATT_PROJ_ROWS = 512
INPROJ_BLOCK = 3072
INPROJ_ROWS = 512


def _params(semantics, vmem_bytes):
    assert vmem_bytes <= V7X_VMEM_BYTES
    return pltpu.CompilerParams(dimension_semantics=semantics, vmem_limit_bytes=int(vmem_bytes))


def _rmsnorm(xf, g):
    return xf * lax.rsqrt(jnp.mean(xf * xf, axis=-1, keepdims=True) + EPS) * g


def _log_sigmoid(x):
    return jnp.minimum(x, 0.0) - jnp.log1p(jnp.exp(-jnp.abs(x)))


def _dot(a, b):
    return jnp.dot(a, b, preferred_element_type=F32)


def _dot_nt(a, b):
    return lax.dot_general(a, b, (((1,), (1,)), ((), ())), preferred_element_type=F32)


def _cast_body(*refs, bn, scales, interleave):
    *in_refs, o_ref = refs
    cast = lambda blk, scale: (blk if scale is None else blk * scale).astype(BF16)
    if interleave:
        for b in range(in_refs[0].shape[2] // bn):
            for t, x_ref in enumerate(in_refs):
                o_ref[0, b, :, t * bn:(t + 1) * bn] = cast(x_ref[0, :, b * bn:(b + 1) * bn], scales[t])
    else:
        c = 0
        for t, x_ref in enumerate(in_refs):
            for b in range(x_ref.shape[2] // bn):
                o_ref[0, c] = cast(x_ref[0, :, b * bn:(b + 1) * bn], scales[t])
                c += 1


def _cast_blocks(ws, bn=None, *, scales=None, interleave=False, kb=256):
    depth, k, _ = ws[0].shape
    bn = bn or ws[0].shape[2]
    n_total = sum(w.shape[2] for w in ws)
    width = bn * len(ws) if interleave else bn
    kb = min(kb, k)
    vmem = 2 * kb * n_total * (4 + 2) + (4 << 20)
    return pl.pallas_call(
        functools.partial(_cast_body, bn=bn, scales=scales or [None] * len(ws), interleave=interleave),
        out_shape=jax.ShapeDtypeStruct((depth, n_total // width, k, width), BF16),
        grid=(depth, k // kb),
        in_specs=[pl.BlockSpec((1, kb, w.shape[2]), lambda l, i: (l, i, 0)) for w in ws],
        out_specs=pl.BlockSpec((1, n_total // width, kb, width), lambda l, i: (l, 0, i, 0)),
        compiler_params=_params(("parallel", "parallel"), vmem),
        name="cast_blocks",
    )(*ws)


def _cast_w_in_body(x_ref, o_ref, gate_ref, *, gate0, n_gate, bn, scaled, scale):
    n, kb = x_ref.shape[1:]
    starts = list(range(0, gate0, bn)) + list(range(gate0 + n_gate, n, bn))
    q0 = gate0 + n_gate
    for c, r0 in enumerate(starts):
        blk = x_ref[0, r0:r0 + bn, :]
        lo, hi = max(q0, r0) - r0, min(q0 + scaled, r0 + bn) - r0
        if lo < hi:
            pieces = [(0, lo, None), (lo, hi, scale), (hi, bn, None)]
            blk = jnp.concatenate([blk[a:b] if s is None else blk[a:b] * s for a, b, s in pieces if a < b], axis=0)
        o_ref[0, c] = blk.T.astype(BF16)
    gates = jnp.concatenate([x_ref[0, gate0:gate0 + n_gate, :], jnp.zeros((LANES - n_gate, kb), F32)], axis=0)
    gate_ref[0] = gates.T.astype(BF16)


def _cast_w_in(w_in, *, gate0, n_gate, scaled, scale, bn=INPROJ_BLOCK, kb=256):
    depth, k, n = w_in.shape
    n_main = n - n_gate
    kb = min(kb, k)
    assert gate0 % bn == 0 and (n - gate0 - n_gate) % bn == 0 and (gate0 + n_gate) % 8 == 0
    vmem = 2 * kb * n * 4 + 2 * kb * n_main * 2 + 3 * kb * n * 4 + (4 << 20)
    return pl.pallas_call(
        functools.partial(_cast_w_in_body, gate0=gate0, n_gate=n_gate, bn=bn, scaled=scaled, scale=scale),
        out_shape=[jax.ShapeDtypeStruct((depth, n_main // bn, k, bn), BF16),
                   jax.ShapeDtypeStruct((depth, k, LANES), BF16)],
        grid=(depth, k // kb),
        in_specs=[pl.BlockSpec((1, n, kb), lambda l, i: (l, 0, i))],
        out_specs=[pl.BlockSpec((1, n_main // bn, kb, bn), lambda l, i: (l, 0, i, 0)),
                   pl.BlockSpec((1, kb, LANES), lambda l, i: (l, i, 0))],
        compiler_params=_params(("parallel", "parallel"), vmem),
        name="cast_w_in",
    )(jnp.swapaxes(w_in, 1, 2))


def _ffn_up_body(*refs, prenormed):
    if prenormed:
        xn_ref, w_ref, h_ref = refs
    else:
        x_ref, g_ref, w_ref, h_ref, xn_ref = refs

        @pl.when(pl.program_id(1) == 0)
        def _():
            xn_ref[...] = _rmsnorm(x_ref[...], g_ref[...]).astype(BF16)

    bf = h_ref.shape[1]
    gu = _dot(xn_ref[...], w_ref[0, 0])
    g, u = gu[:, :bf], gu[:, bf:]
    h_ref[...] = (g * jax.nn.sigmoid(g) * u).astype(h_ref.dtype)


def _ffn(x, g_pre, g_post, w_gate_up, w_down, down_index, layer, *, xn=None, next_gain=None):
    m, d = x.shape
    nf, bf = w_gate_up.shape[1], w_gate_up.shape[3] // 2
    f = nf * bf
    prenormed = xn is not None
    bm = min(FFN_ROWS_PRENORMED if prenormed else FFN_ROWS, m)
    assert m % bm == 0
    w_spec = pl.BlockSpec((1, 1, d, 2 * bf), lambda i, j: (layer, j, 0, 0))
    row_spec = pl.BlockSpec((bm, d), lambda i, j: (i, 0))
    if prenormed:
        in_specs, args, scratch = [row_spec, w_spec], (xn, w_gate_up), []
        vmem_up = 2 * bm * d * 2 + 4 * d * bf * 2 + 2 * bm * bf * 2 + 3 * bm * bf * 4 + (4 << 20)
    else:
        in_specs = [row_spec, pl.BlockSpec((1, d), lambda i, j: (0, 0)), w_spec]
        args, scratch = (x, g_pre, w_gate_up), [pltpu.VMEM((bm, d), BF16)]
        vmem_up = 2 * bm * d * 4 + bm * d * 2 + 4 * d * bf * 2 + 2 * bm * bf * 2 + 3 * bm * bf * 4 + (4 << 20)
    hidden = pl.pallas_call(
        functools.partial(_ffn_up_body, prenormed=prenormed),
        out_shape=jax.ShapeDtypeStruct((m, f), BF16),
        grid=(m // bm, nf),
        in_specs=in_specs,
        out_specs=pl.BlockSpec((bm, bf), lambda i, j: (i, j)),
        scratch_shapes=scratch,
        compiler_params=_params(("parallel", "arbitrary"), vmem_up),
        name="ffn_up",
    )(*args)
    return _outproj(x, g_post, [hidden], w_down, layer, w_index=down_index, res_scale=0.5, next_gain=next_gain,
                    bm=DOWN_ROWS)


def _normproj_body(x_ref, g_ref, w_ref, *rest, with_gate):
    if with_gate:
        wgate_ref, o_ref, gate_ref, xn_ref = rest
    else:
        o_ref, xn_ref = rest
    j = pl.program_id(1)

    @pl.when(j == 0)
    def _():
        xn = _rmsnorm(x_ref[...], g_ref[...]).astype(BF16)
        xn_ref[...] = xn
        if with_gate:
            gate_ref[...] = _dot(xn, wgate_ref[0])

    o_ref[...] = _dot(xn_ref[...], w_ref[0, 0]).astype(o_ref.dtype)


def _normproj(x, g, w, layer, w_gate=None, *, block0=0, nb=None, bm=1024):
    m, d = x.shape
    nb, bn = nb or w.shape[1], w.shape[3]
    bm = min(bm, m)
    assert m % bm == 0
    with_gate = w_gate is not None
    in_specs = [
        pl.BlockSpec((bm, d), lambda i, j: (i, 0)),
        pl.BlockSpec((1, d), lambda i, j: (0, 0)),
        pl.BlockSpec((1, 1, d, bn), lambda i, j: (layer, block0 + j, 0, 0)),
    ]
    out_shape = [jax.ShapeDtypeStruct((m, nb * bn), BF16)]
    out_specs = [pl.BlockSpec((bm, bn), lambda i, j: (i, j))]
    args = [x, g, w]
    if with_gate:
        in_specs.append(pl.BlockSpec((1, d, LANES), lambda i, j: (layer, 0, 0)))
        out_shape.append(jax.ShapeDtypeStruct((m, LANES), F32))
        out_specs.append(pl.BlockSpec((bm, LANES), lambda i, j: (i, 0)))
        args.append(w_gate)
    vmem = 2 * bm * d * 4 + bm * d * 2 + 2 * d * bn * 2 + 2 * bm * bn * 2 + bm * bn * 4 + (6 << 20)
    out = pl.pallas_call(
        functools.partial(_normproj_body, with_gate=with_gate),
        out_shape=out_shape,
        grid=(m // bm, nb),
        in_specs=in_specs,
        out_specs=out_specs,
        scratch_shapes=[pltpu.VMEM((bm, d), BF16)],
        compiler_params=_params(("parallel", "arbitrary"), vmem),
        name="normproj_gate" if with_gate else "normproj",
    )(*args)
    return out if with_gate else out[0]


def _outproj_body(*refs, n_parts, res_scale, handoff):
    a_refs = refs[:n_parts]
    w_refs = refs[n_parts:2 * n_parts]
    if handoff:
        x_ref, g_ref, gnext_ref, o_ref, xn_ref = refs[2 * n_parts:]
    else:
        x_ref, g_ref, o_ref = refs[2 * n_parts:]
    y = _dot(a_refs[0][...], w_refs[0][0, 0])
    for a_ref, w_ref in zip(a_refs[1:], w_refs[1:]):
        y += _dot(a_ref[...], w_ref[0, 0])
    out = x_ref[...] + res_scale * _rmsnorm(y, g_ref[...])
    o_ref[...] = out
    if handoff:
        xn_ref[...] = _rmsnorm(out, gnext_ref[...]).astype(xn_ref.dtype)


def _outproj(x, g, acts, w, layer, *, w_index=0, res_scale=1.0, next_gain=None, bm=512):
    m, d = x.shape
    bm = min(bm, m)
    assert m % bm == 0
    handoff = next_gain is not None
    k_total = w.shape[2]
    in_specs = [pl.BlockSpec((bm, a.shape[1]), lambda i: (i, 0)) for a in acts]
    row0 = 0
    for a in acts:
        k = a.shape[1]
        assert row0 % k == 0
        in_specs.append(pl.BlockSpec((1, 1, k, d), functools.partial(lambda i, rb: (layer, w_index, rb, 0), rb=row0 // k),
                                     pipeline_mode=pl.Buffered(1)))
        row0 += k
    assert row0 == k_total
    gain_spec = pl.BlockSpec((1, d), lambda i: (0, 0))
    row_spec = pl.BlockSpec((bm, d), lambda i: (i, 0))
    in_specs += [row_spec, gain_spec] + ([gain_spec] if handoff else [])
    out_shape = [jax.ShapeDtypeStruct((m, d), F32)] + ([jax.ShapeDtypeStruct((m, d), BF16)] if handoff else [])
    vmem = k_total * d * 2 + 2 * bm * k_total * 2 + 5 * bm * d * 4 + (2 * bm * d * 2 if handoff else 0) + (4 << 20)
    out = pl.pallas_call(
        functools.partial(_outproj_body, n_parts=len(acts), res_scale=res_scale, handoff=handoff),
        out_shape=out_shape,
        grid=(m // bm,),
        in_specs=in_specs,
        out_specs=[row_spec] * len(out_shape),
        compiler_params=_params(("parallel",), vmem),
        name="outproj",
    )(*acts, *([w] * len(acts)), x, g, *([next_gain] if handoff else []))
    return (out[0], out[1]) if handoff else (out[0], None)


CONV_PAD = 8


def _conv_silu(pad_ref, raw, w, b):
    s = raw.shape[0]
    pad_ref[0:CONV_PAD, :] = jnp.zeros((CONV_PAD, raw.shape[1]), F32)
    pad_ref[CONV_PAD:CONV_PAD + s, :] = raw
    acc = raw * w[CONV_K - 1:CONV_K, :] + b
    for shift in range(1, CONV_K):
        acc += pad_ref[CONV_PAD - shift:CONV_PAD - shift + s, :] * w[CONV_K - 1 - shift:CONV_K - shift, :]
    return acc * jax.nn.sigmoid(acc)


GATE_ROWS = 16


def _split3(x):
    hi = x.astype(BF16)
    r1 = x - hi.astype(F32)
    mid = r1.astype(BF16)
    return hi, mid, (r1 - mid.astype(F32)).astype(BF16)


def _mlstm_body(q_ref, k_ref, v_ref, og_ref, gates_ref, gbias_ref, cwq_ref, cbq_ref, cwk_ref, cbk_ref,
                norm_ref, o_ref, qs_ref, ks_ref, vx_ref, gt_ref, bt_ref, row_ref, pad_ref, *, chunk):
    s = q_ref.shape[0]
    nc = s // chunk
    head = pl.program_id(1)
    ext = vx_ref.shape[1]
    qs_ref[...] = _conv_silu(pad_ref, q_ref[...].astype(F32), cwq_ref[...], cbq_ref[...]).astype(BF16)
    ks_ref[...] = _conv_silu(pad_ref, k_ref[...].astype(F32), cwk_ref[...], cbk_ref[...]) * (M_DQK ** -0.5)
    ones_col = (lax.broadcasted_iota(jnp.int32, (s, ext - M_DV), 1) == 0).astype(BF16)
    vx_ref[...] = jnp.concatenate([v_ref[...], ones_col], axis=1)

    upper = (lax.broadcasted_iota(jnp.int32, (chunk, chunk), 0)
             <= lax.broadcasted_iota(jnp.int32, (chunk, chunk), 1)).astype(BF16)
    for c in range(nc):
        cols = slice(c * chunk, (c + 1) * chunk)
        gt = (gates_ref[cols, :] + gbias_ref[...]).T[:GATE_ROWS]
        gt_ref[:, cols] = gt
        bt_ref[:, cols] = sum(_dot(piece, upper) for piece in _split3(_log_sigmoid(gt)))

    i_row = gt_ref[pl.ds(head, 1), :]
    b_row = bt_ref[pl.ds(M_HEADS + head, 1), :]
    r_row = i_row - b_row
    pos = lax.broadcasted_iota(jnp.int32, (1, s), 1) % chunk
    cmax = r_row
    shift = 1
    while shift < chunk:
        cmax = jnp.maximum(cmax, jnp.where(pos >= shift, pltpu.roll(cmax, shift, 1), -jnp.inf))
        shift *= 2

    row_ref[...] = jnp.zeros(row_ref.shape, F32)
    m = jnp.zeros((1, 1), F32)
    ws_rows, decays = [], []
    for c in range(nc):
        cols = slice(c * chunk, (c + 1) * chunk)
        a = jnp.maximum(m, cmax[:, cols])
        a_last = a[:, chunk - 1:]
        b_c = b_row[:, cols]
        row_ref[0:1, cols] = a
        row_ref[1:2, cols] = jnp.exp(m - a)
        row_ref[2:3, cols] = jnp.exp(-(b_c + a))
        ws_rows.append(jnp.exp(r_row[:, cols] - a_last))
        decays.append(jnp.exp(m - a_last))
        m = b_c[:, chunk - 1:] + a_last

    causal = (lax.broadcasted_iota(jnp.int32, (chunk, chunk), 0)
              >= lax.broadcasted_iota(jnp.int32, (chunk, chunk), 1))
    gain = norm_ref[...]

    def independent(c):
        rows = slice(c * chunk, (c + 1) * chunk)
        kf = ks_ref[rows, :]
        s_qk = _dot_nt(qs_ref[rows, :], kf.astype(BF16))
        kv = _dot((kf.T * ws_rows[c]).astype(BF16), vx_ref[rows, :])
        return s_qk, kv

    cmat = None
    nxt = independent(0)
    for c in range(nc):
        rows = slice(c * chunk, (c + 1) * chunk)
        s_qk, kv = nxt
        if c + 1 < nc:
            nxt = independent(c + 1)
        col = row_ref[:, rows].T
        a_col, wi_col, en_col = col[:, 0:1], col[:, 1:2], col[:, 2:3]
        p = jnp.exp(jnp.where(causal, r_row[:, rows] - a_col, -jnp.inf)) * s_qk
        hx = _dot(p.astype(BF16), vx_ref[rows, :])
        if cmat is not None:
            hx += wi_col * _dot(qs_ref[rows, :], cmat.astype(BF16))
        hc = hx[:, :M_DV] / jnp.maximum(jnp.abs(hx[:, M_DV:M_DV + 1]), en_col)
        cmat = kv if cmat is None else decays[c] * cmat + kv
        y = _rmsnorm(hc, gain) * jax.nn.sigmoid(og_ref[rows, :].astype(F32))
        o_ref[rows, :] = y.astype(o_ref.dtype)


def _mlstm(z, gates, gate_bias, conv_w, conv_b, norm, *, batch, seq):
    chunk = min(MLSTM_CHUNK, seq)
    qk_blocks = M_HEADS
    v_block0 = 2 * M_HEADS * M_DQK // M_DV
    og_block0 = v_block0 + M_HEADS
    vmem = seq * (2 * 2 * M_DQK * 2 + 3 * 2 * M_DV * 2 + 3 * LANES * 4 + M_DQK * 6) + (24 << 20)
    return pl.pallas_call(
        functools.partial(_mlstm_body, chunk=chunk),
        out_shape=jax.ShapeDtypeStruct((batch * seq, M_HEADS * M_DV), BF16),
        grid=(batch, M_HEADS),
        in_specs=[
            pl.BlockSpec((seq, M_DQK), lambda b, h: (b, h)),
            pl.BlockSpec((seq, M_DQK), lambda b, h: (b, qk_blocks + h)),
            pl.BlockSpec((seq, M_DV), lambda b, h: (b, v_block0 + h)),
            pl.BlockSpec((seq, M_DV), lambda b, h: (b, og_block0 + h)),
            pl.BlockSpec((seq, LANES), lambda b, h: (b, 0)),
            pl.BlockSpec((1, LANES), lambda b, h: (0, 0)),
            pl.BlockSpec((CONV_K, M_DQK), lambda b, h: (0, h)),
            pl.BlockSpec((1, M_DQK), lambda b, h: (0, h)),
            pl.BlockSpec((CONV_K, M_DQK), lambda b, h: (0, qk_blocks + h)),
            pl.BlockSpec((1, M_DQK), lambda b, h: (0, qk_blocks + h)),
            pl.BlockSpec((1, M_DV), lambda b, h: (0, h)),
        ],
        out_specs=pl.BlockSpec((seq, M_DV), lambda b, h: (b, h)),
        scratch_shapes=[pltpu.VMEM((seq, M_DQK), BF16), pltpu.VMEM((seq, M_DQK), F32),
                        pltpu.VMEM((seq, M_DV + LANES), BF16), pltpu.VMEM((GATE_ROWS, seq), F32),
                        pltpu.VMEM((GATE_ROWS, seq), F32), pltpu.VMEM((LANES, seq), F32),
                        pltpu.VMEM((CONV_PAD + seq, M_DQK), F32)],
        compiler_params=_params(("parallel", "parallel"), vmem),
        name="mlstm",
    )(z, z, z, z, gates, gate_bias, conv_w, conv_b, conv_w, conv_b, norm)


def _t5_bucket_np(n):
    nf = np.maximum(n, 1).astype(np.float64)
    val = np.log(nf / MAX_EXACT) / math.log(MAX_DISTANCE / MAX_EXACT) * (NUM_BUCKETS - MAX_EXACT)
    frac = val[(n > MAX_EXACT) & (n < MAX_DISTANCE)]
    assert np.all(np.abs(frac - np.round(frac)) > 1e-3)
    large = np.minimum(MAX_EXACT + val.astype(np.int64), NUM_BUCKETS - 1)
    return np.where(n < MAX_EXACT, n, large)


def _bucket_tile(blk):
    rel = np.arange(blk)[None, :] + blk - np.arange(2 * blk)[:, None]
    return np.where(rel >= 0, _t5_bucket_np(np.maximum(rel, 0)), -1).astype(np.int32)


def _bias_body(rb_ref, bucket_ref, t_ref):
    h = pl.program_id(0)
    bucket = bucket_ref[...]
    acc = jnp.zeros(bucket.shape, F32)
    for b in range(NUM_BUCKETS):
        acc = jnp.where(bucket == b, rb_ref[h * NUM_BUCKETS + b], acc)
    far = rb_ref[h * NUM_BUCKETS + NUM_BUCKETS - 1]
    t_ref[0] = jnp.where(bucket < 0, -jnp.inf, (acc - far) * LOG2E)


def _bias_tiles(rel_bias, blk):
    assert blk >= MAX_DISTANCE
    bucket = jnp.asarray(_bucket_tile(blk))
    return pl.pallas_call(
        _bias_body,
        out_shape=jax.ShapeDtypeStruct((DA_HEADS, 2 * blk, blk), F32),
        grid=(DA_HEADS,),
        in_specs=[pl.BlockSpec(memory_space=pltpu.SMEM),
                  pl.BlockSpec((2 * blk, blk), lambda h: (0, 0))],
        out_specs=pl.BlockSpec((1, 2 * blk, blk), lambda h: (h, 0, 0)),
        compiler_params=_params(("arbitrary",), 16 << 20),
        name="t5_bias_tiles",
    )(rel_bias.T.reshape(-1), bucket)


VT_ROWS = DA_DV + 16


def _diffattn_body(lam_ref, q_ref, k_ref, v_ref, t_ref, g_ref, o_ref, vt_ref, *, blk, lam_init):
    nb = q_ref.shape[0] // blk
    ones_rows = (lax.broadcasted_iota(jnp.int32, (VT_ROWS - DA_DV, blk), 0) == 0).astype(F32)
    for c in range(nb):
        vt = v_ref[c * blk:(c + 1) * blk, :].astype(F32).T
        vt_ref[c] = jnp.concatenate([vt, ones_rows], axis=0).astype(BF16)

    lp = lam_ref[...]
    lam = (jnp.exp(jnp.sum(lp[0:1] * lp[1:2], axis=-1, keepdims=True))
           - jnp.exp(jnp.sum(lp[2:3] * lp[3:4], axis=-1, keepdims=True)) + lam_init)
    sub = lax.broadcasted_iota(jnp.int32, (LANES, blk), 0)
    gain = g_ref[...]

    def query_maps(qi):
        qt = q_ref[qi * blk:(qi + 1) * blk, :].astype(F32).T
        return (jnp.where(sub < DA_D, qt, 0.0).astype(BF16), jnp.where(sub >= DA_D, qt, 0.0).astype(BF16))

    def scores(qts, qi, kc):
        kb = k_ref[kc * blk:(kc + 1) * blk, :]
        s = [_dot(kb, qt_m) for qt_m in qts]
        if kc >= qi - 1:
            bias = t_ref[0, blk:, :] if kc == qi else t_ref[0, :blk, :]
            s = [s_m + bias for s_m in s]
        return s

    units = [(qi, kc) for qi in range(nb) for kc in range(qi, -1, -1)]
    qts_of = {}

    def unit_scores(u):
        qi, kc = units[u]
        if qi not in qts_of:
            qts_of.clear()
            qts_of[qi] = query_maps(qi)
        return scores(qts_of[qi], qi, kc)

    ahead = [unit_scores(u) for u in range(min(DA_LOOKAHEAD, len(units)))]
    state = [None, None]
    for u, (qi, kc) in enumerate(units):
        s_cur = ahead.pop(0)
        if u + DA_LOOKAHEAD < len(units):
            ahead.append(unit_scores(u + DA_LOOKAHEAD))
        for a in range(2):
            s = s_cur[a]
            blk_max = jnp.max(s, axis=0, keepdims=True)
            if state[a] is None:
                m_new = blk_max
                acc = _dot(vt_ref[kc], jnp.exp2(s - m_new).astype(BF16))
            else:
                m, acc = state[a]
                m_new = jnp.maximum(m, blk_max)
                acc = acc * jnp.exp2(m - m_new) + _dot(vt_ref[kc], jnp.exp2(s - m_new).astype(BF16))
            state[a] = (m_new, acc)
        if kc == 0:
            outs = [acc[:DA_DV] / acc[DA_DV:DA_DV + 1] for _, acc in state]
            o = (outs[0] - lam * outs[1]).T
            o_ref[qi * blk:(qi + 1) * blk, :] = (_rmsnorm(o, gain) * (1.0 - lam_init)).astype(o_ref.dtype)
            state = [None, None]


def _diffattn(z, tiles, lam_p, subln, *, batch, seq, lam_init, col0):
    blk = tiles.shape[2]
    q0 = col0 // LANES
    k0 = q0 + DA_HEADS
    v0 = k0 + DA_HEADS
    vmem = 8 * seq * LANES * 2 + seq * VT_ROWS * 2 + 4 * blk * blk * 4 + 24 * blk * blk * 4 + (8 << 20)
    return pl.pallas_call(
        functools.partial(_diffattn_body, blk=blk, lam_init=lam_init),
        out_shape=jax.ShapeDtypeStruct((batch * seq, DA_HEADS * DA_DV), BF16),
        grid=(batch, DA_HEADS),
        in_specs=[
            pl.BlockSpec((4, DA_D), lambda b, h: (0, 0)),
            pl.BlockSpec((seq, LANES), lambda b, h: (b, q0 + h)),
            pl.BlockSpec((seq, LANES), lambda b, h: (b, k0 + h)),
            pl.BlockSpec((seq, LANES), lambda b, h: (b, v0 + h)),
            pl.BlockSpec((1, 2 * blk, blk), lambda b, h: (h, 0, 0)),
            pl.BlockSpec((1, DA_DV), lambda b, h: (0, 0)),
        ],
        out_specs=pl.BlockSpec((seq, DA_DV), lambda b, h: (b, h)),
        scratch_shapes=[pltpu.VMEM((seq // blk, VT_ROWS, blk), BF16)],
        compiler_params=_params(("parallel", "parallel"), vmem),
        name="diffattn",
    )(lam_p, z, z, z, tiles, subln)


def _xattn_body(q_ref, k_ref, v_ref, o_ref, *, blk):
    nb = q_ref.shape[0] // blk
    kb = k_ref[...]
    vb = v_ref[...]
    s_next = _dot_nt(q_ref[0:blk, :], kb)
    for j in range(nb):
        s = s_next
        if j + 1 < nb:
            s_next = _dot_nt(q_ref[(j + 1) * blk:(j + 2) * blk, :], kb)
        p = jnp.exp2(s - jnp.max(s, axis=-1, keepdims=True))
        o = _dot(p.astype(BF16), vb) / jnp.sum(p, axis=-1, keepdims=True)
        o_ref[j * blk:(j + 1) * blk, :] = o.astype(o_ref.dtype)


def _xattn(q, kv, *, batch, seq, mem_len, blk=256):
    d = q.shape[1]
    dh = d // X_HEADS
    blk = min(blk, seq)
    vmem = 4 * seq * dh * 2 + 4 * mem_len * dh * 2 + 8 * blk * (mem_len + dh) * 4 + (8 << 20)
    return pl.pallas_call(
        functools.partial(_xattn_body, blk=blk),
        out_shape=jax.ShapeDtypeStruct((batch * seq, d), BF16),
        grid=(batch, X_HEADS),
        in_specs=[
            pl.BlockSpec((seq, dh), lambda b, h: (b, h)),
            pl.BlockSpec((mem_len, dh), lambda b, h: (b, h)),
            pl.BlockSpec((mem_len, dh), lambda b, h: (b, X_HEADS + h)),
        ],
        out_specs=pl.BlockSpec((seq, dh), lambda b, h: (b, h)),
        compiler_params=_params(("parallel", "parallel"), vmem),
        name="xattn",
    )(q, kv, kv)


def kernel(x, mem, rel_bias, ffn1_norm_pre, ffn1_norm_post, ffn1_w_gate, ffn1_w_up, ffn1_w_down, mix_norm_pre, mix_norm_post, w_in, conv_w, conv_b, b_igate, b_fgate, mlstm_norm, diff_lambda, diff_subln, w_out, xattn_norm_pre, xattn_norm_post, mem_norm, xattn_wq, xattn_wk, xattn_wv, xattn_wo, ffn2_norm_pre, ffn2_norm_post, ffn2_w_gate, ffn2_w_up, ffn2_w_down):
    batch, seq, d = x.shape
    mem_len = mem.shape[1]
    depth = w_in.shape[0]
    n_gate = 2 * M_HEADS
    da_qw = DA_HEADS * 2 * DA_D
    gate0 = 2 * M_HEADS * M_DQK + 2 * M_HEADS * M_DV
    xf = x.reshape(batch * seq, d)
    memf = mem.reshape(batch * mem_len, d)
    row = lambda v: v.reshape(1, -1).astype(F32)

    tiles = _bias_tiles(rel_bias, min(DA_BLOCK, seq))
    w_main, w_gate = _cast_w_in(w_in, gate0=gate0, n_gate=n_gate, scaled=da_qw, scale=DA_D ** -0.5 * LOG2E)
    w_down = _cast_blocks([ffn1_w_down, ffn2_w_down])
    ffn1 = (_cast_blocks([ffn1_w_gate, ffn1_w_up], FFN_BLOCK, interleave=True), w_down, 0)
    ffn2 = (_cast_blocks([ffn2_w_gate, ffn2_w_up], FFN_BLOCK, interleave=True), w_down, 1)
    w_att = _cast_blocks([w_out, xattn_wo, xattn_wq, xattn_wk, xattn_wv],
                         scales=[None, None, (d // X_HEADS) ** -0.5 * LOG2E, None, None])

    xn = None
    for l in range(depth):
        lam_init = 0.8 - 0.6 * math.exp(-0.3 * l)
        xf, _ = _ffn(xf, row(ffn1_norm_pre[l]), row(ffn1_norm_post[l]), *ffn1, l, xn=xn)

        z, gates = _normproj(xf, row(mix_norm_pre[l]), w_main, l, w_gate, bm=INPROJ_ROWS)
        gate_bias = jnp.pad(jnp.concatenate([b_igate[l], b_fgate[l]]), (0, LANES - n_gate)).reshape(1, LANES)
        y_m = _mlstm(z, gates, gate_bias.astype(F32), conv_w[l].astype(F32), row(conv_b[l]),
                     row(mlstm_norm[l]), batch=batch, seq=seq)
        y_d = _diffattn(z, tiles, diff_lambda[l].astype(F32), row(diff_subln[l]),
                        batch=batch, seq=seq, lam_init=lam_init, col0=gate0)
        xf, _ = _outproj(xf, row(mix_norm_post[l]), [y_m, y_d], w_att, l, w_index=0)

        q = _normproj(xf, row(xattn_norm_pre[l]), w_att, l, block0=2, nb=1, bm=ATT_PROJ_ROWS)
        kv = _normproj(memf, row(mem_norm[l]), w_att, l, block0=3, nb=2, bm=ATT_PROJ_ROWS)
        c = _xattn(q, kv, batch=batch, seq=seq, mem_len=mem_len)
        xf, xn2 = _outproj(xf, row(xattn_norm_post[l]), [c], w_att, l, w_index=1, next_gain=row(ffn2_norm_pre[l]))

        next_gain = row(ffn1_norm_pre[l + 1]) if l + 1 < depth else None
        xf, xn = _ffn(xf, row(ffn2_norm_pre[l]), row(ffn2_norm_post[l]), *ffn2, l, xn=xn2, next_gain=next_gain)
    return xf.reshape(batch, seq, d)
```

```python
import functools
import math

import numpy as np
import jax
import jax.numpy as jnp
from jax import lax
from jax.experimental import pallas as pl
from jax.experimental.pallas import tpu as pltpu

F32 = jnp.float32
BF16 = jnp.bfloat16
EPS = 1e-6
LOG2E = math.log2(math.e)

M_HEADS = 4
M_DQK = 128
M_DV = 256
CONV_K = 4
DA_HEADS = 8
DA_D = 64
DA_DV = 128
X_HEADS = 4
NUM_BUCKETS = 32
MAX_EXACT = NUM_BUCKETS // 2
MAX_DISTANCE = 128

LANES = 128
MXU_N = 256
V7X_VMEM_BYTES = 64 * 2 ** 20
MLSTM_CHUNK = 128
DA_BLOCK = 256
DA_LOOKAHEAD = 3
FFN_BLOCK = 1408
FFN_ROWS = 512
FFN_ROWS_PRENORMED = 1024
DOWN_ROWS = 512
ATT_PROJ_ROWS = 512
INPROJ_BLOCK = 3072
INPROJ_ROWS = 512


def _params(semantics, vmem_bytes):
    assert vmem_bytes <= V7X_VMEM_BYTES
    return pltpu.CompilerParams(dimension_semantics=semantics, vmem_limit_bytes=int(vmem_bytes))


def _rmsnorm(xf, g):
    return xf * lax.rsqrt(jnp.mean(xf * xf, axis=-1, keepdims=True) + EPS) * g


def _log_sigmoid(x):
    return jnp.minimum(x, 0.0) - jnp.log1p(jnp.exp(-jnp.abs(x)))


def _dot(a, b):
    return jnp.dot(a, b, preferred_element_type=F32)


def _dot_nt(a, b):
    return lax.dot_general(a, b, (((1,), (1,)), ((), ())), preferred_element_type=F32)


def _cast_body(*refs, bn, scales, interleave):
    *in_refs, o_ref = refs
    cast = lambda blk, scale: (blk if scale is None else blk * scale).astype(BF16)
    if interleave:
        for b in range(in_refs[0].shape[2] // bn):
            for t, x_ref in enumerate(in_refs):
                o_ref[0, b, :, t * bn:(t + 1) * bn] = cast(x_ref[0, :, b * bn:(b + 1) * bn], scales[t])
    else:
        c = 0
        for t, x_ref in enumerate(in_refs):
            for b in range(x_ref.shape[2] // bn):
                o_ref[0, c] = cast(x_ref[0, :, b * bn:(b + 1) * bn], scales[t])
                c += 1


def _cast_blocks(ws, bn=None, *, scales=None, interleave=False, kb=256):
    depth, k, _ = ws[0].shape
    bn = bn or ws[0].shape[2]
    n_total = sum(w.shape[2] for w in ws)
    width = bn * len(ws) if interleave else bn
    kb = min(kb, k)
    vmem = 2 * kb * n_total * (4 + 2) + (4 << 20)
    return pl.pallas_call(
        functools.partial(_cast_body, bn=bn, scales=scales or [None] * len(ws), interleave=interleave),
        out_shape=jax.ShapeDtypeStruct((depth, n_total // width, k, width), BF16),
        grid=(depth, k // kb),
        in_specs=[pl.BlockSpec((1, kb, w.shape[2]), lambda l, i: (l, i, 0)) for w in ws],
        out_specs=pl.BlockSpec((1, n_total // width, kb, width), lambda l, i: (l, 0, i, 0)),
        compiler_params=_params(("parallel", "parallel"), vmem),
        name="cast_blocks",
    )(*ws)


def _cast_w_in_body(x_ref, o_ref, gate_ref, *, gate0, n_gate, bn, scaled, scale):
    n, kb = x_ref.shape[1:]
    starts = list(range(0, gate0, bn)) + list(range(gate0 + n_gate, n, bn))
    q0 = gate0 + n_gate
    for c, r0 in enumerate(starts):
        blk = x_ref[0, r0:r0 + bn, :]
        lo, hi = max(q0, r0) - r0, min(q0 + scaled, r0 + bn) - r0
        if lo < hi:
            pieces = [(0, lo, None), (lo, hi, scale), (hi, bn, None)]
            blk = jnp.concatenate([blk[a:b] if s is None else blk[a:b] * s for a, b, s in pieces if a < b], axis=0)
        o_ref[0, c] = blk.T.astype(BF16)
    gates = jnp.concatenate([x_ref[0, gate0:gate0 + n_gate, :], jnp.zeros((LANES - n_gate, kb), F32)], axis=0)
    gate_ref[0] = gates.T.astype(BF16)


def _cast_w_in(w_in, *, gate0, n_gate, scaled, scale, bn=INPROJ_BLOCK, kb=256):
    depth, k, n = w_in.shape
    n_main = n - n_gate
    kb = min(kb, k)
    assert gate0 % bn == 0 and (n - gate0 - n_gate) % bn == 0 and (gate0 + n_gate) % 8 == 0
    vmem = 2 * kb * n * 4 + 2 * kb * n_main * 2 + 3 * kb * n * 4 + (4 << 20)
    return pl.pallas_call(
        functools.partial(_cast_w_in_body, gate0=gate0, n_gate=n_gate, bn=bn, scaled=scaled, scale=scale),
        out_shape=[jax.ShapeDtypeStruct((depth, n_main // bn, k, bn), BF16),
                   jax.ShapeDtypeStruct((depth, k, LANES), BF16)],
        grid=(depth, k // kb),
        in_specs=[pl.BlockSpec((1, n, kb), lambda l, i: (l, 0, i))],
        out_specs=[pl.BlockSpec((1, n_main // bn, kb, bn), lambda l, i: (l, 0, i, 0)),
                   pl.BlockSpec((1, kb, LANES), lambda l, i: (l, i, 0))],
        compiler_params=_params(("parallel", "parallel"), vmem),
        name="cast_w_in",
    )(jnp.swapaxes(w_in, 1, 2))


def _ffn_up_body(*refs, prenormed):
    if prenormed:
        xn_ref, w_ref, h_ref = refs
    else:
        x_ref, g_ref, w_ref, h_ref, xn_ref = refs

        @pl.when(pl.program_id(1) == 0)
        def _():
            xn_ref[...] = _rmsnorm(x_ref[...], g_ref[...]).astype(BF16)

    bf = h_ref.shape[1]
    gu = _dot(xn_ref[...], w_ref[0, 0])
    g, u = gu[:, :bf], gu[:, bf:]
    h_ref[...] = (g * jax.nn.sigmoid(g) * u).astype(h_ref.dtype)


def _ffn(x, g_pre, g_post, w_gate_up, w_down, down_index, layer, *, xn=None, next_gain=None):
    m, d = x.shape
    nf, bf = w_gate_up.shape[1], w_gate_up.shape[3] // 2
    f = nf * bf
    prenormed = xn is not None
    bm = min(FFN_ROWS_PRENORMED if prenormed else FFN_ROWS, m)
    assert m % bm == 0
    w_spec = pl.BlockSpec((1, 1, d, 2 * bf), lambda i, j: (layer, j, 0, 0))
    row_spec = pl.BlockSpec((bm, d), lambda i, j: (i, 0))
    if prenormed:
        in_specs, args, scratch = [row_spec, w_spec], (xn, w_gate_up), []
        vmem_up = 2 * bm * d * 2 + 4 * d * bf * 2 + 2 * bm * bf * 2 + 3 * bm * bf * 4 + (4 << 20)
    else:
        in_specs = [row_spec, pl.BlockSpec((1, d), lambda i, j: (0, 0)), w_spec]
        args, scratch = (x, g_pre, w_gate_up), [pltpu.VMEM((bm, d), BF16)]
        vmem_up = 2 * bm * d * 4 + bm * d * 2 + 4 * d * bf * 2 + 2 * bm * bf * 2 + 3 * bm * bf * 4 + (4 << 20)
    hidden = pl.pallas_call(
        functools.partial(_ffn_up_body, prenormed=prenormed),
        out_shape=jax.ShapeDtypeStruct((m, f), BF16),
        grid=(m // bm, nf),
        in_specs=in_specs,
        out_specs=pl.BlockSpec((bm, bf), lambda i, j: (i, j)),
        scratch_shapes=scratch,
        compiler_params=_params(("parallel", "arbitrary"), vmem_up),
        name="ffn_up",
    )(*args)
    return _outproj(x, g_post, [hidden], w_down, layer, w_index=down_index, res_scale=0.5, next_gain=next_gain,
                    bm=DOWN_ROWS)


def _normproj_body(x_ref, g_ref, w_ref, *rest, with_gate):
    if with_gate:
        wgate_ref, o_ref, gate_ref, xn_ref = rest
    else:
        o_ref, xn_ref = rest
    j = pl.program_id(1)

    @pl.when(j == 0)
    def _():
        xn = _rmsnorm(x_ref[...], g_ref[...]).astype(BF16)
        xn_ref[...] = xn
        if with_gate:
            gate_ref[...] = _dot(xn, wgate_ref[0])

    o_ref[...] = _dot(xn_ref[...], w_ref[0, 0]).astype(o_ref.dtype)


def _normproj(x, g, w, layer, w_gate=None, *, block0=0, nb=None, bm=1024):
    m, d = x.shape
    nb, bn = nb or w.shape[1], w.shape[3]
    bm = min(bm, m)
    assert m % bm == 0
    with_gate = w_gate is not None
    in_specs = [
        pl.BlockSpec((bm, d), lambda i, j: (i, 0)),
        pl.BlockSpec((1, d), lambda i, j: (0, 0)),
        pl.BlockSpec((1, 1, d, bn), lambda i, j: (layer, block0 + j, 0, 0)),
    ]
    out_shape = [jax.ShapeDtypeStruct((m, nb * bn), BF16)]
    out_specs = [pl.BlockSpec((bm, bn), lambda i, j: (i, j))]
    args = [x, g, w]
    if with_gate:
        in_specs.append(pl.BlockSpec((1, d, LANES), lambda i, j: (layer, 0, 0)))
        out_shape.append(jax.ShapeDtypeStruct((m, LANES), F32))
        out_specs.append(pl.BlockSpec((bm, LANES), lambda i, j: (i, 0)))
        args.append(w_gate)
    vmem = 2 * bm * d * 4 + bm * d * 2 + 2 * d * bn * 2 + 2 * bm * bn * 2 + bm * bn * 4 + (6 << 20)
    out = pl.pallas_call(
        functools.partial(_normproj_body, with_gate=with_gate),
        out_shape=out_shape,
        grid=(m // bm, nb),
        in_specs=in_specs,
        out_specs=out_specs,
        scratch_shapes=[pltpu.VMEM((bm, d), BF16)],
        compiler_params=_params(("parallel", "arbitrary"), vmem),
        name="normproj_gate" if with_gate else "normproj",
    )(*args)
    return out if with_gate else out[0]


def _outproj_body(*refs, n_parts, res_scale, handoff):
    a_refs = refs[:n_parts]
    w_refs = refs[n_parts:2 * n_parts]
    if handoff:
        x_ref, g_ref, gnext_ref, o_ref, xn_ref = refs[2 * n_parts:]
    else:
        x_ref, g_ref, o_ref = refs[2 * n_parts:]
    y = _dot(a_refs[0][...], w_refs[0][0, 0])
    for a_ref, w_ref in zip(a_refs[1:], w_refs[1:]):
        y += _dot(a_ref[...], w_ref[0, 0])
    out = x_ref[...] + res_scale * _rmsnorm(y, g_ref[...])
    o_ref[...] = out
    if handoff:
        xn_ref[...] = _rmsnorm(out, gnext_ref[...]).astype(xn_ref.dtype)


def _outproj(x, g, acts, w, layer, *, w_index=0, res_scale=1.0, next_gain=None, bm=512):
    m, d = x.shape
    bm = min(bm, m)
    assert m % bm == 0
    handoff = next_gain is not None
    k_total = w.shape[2]
    in_specs = [pl.BlockSpec((bm, a.shape[1]), lambda i: (i, 0)) for a in acts]
    row0 = 0
    for a in acts:
        k = a.shape[1]
        assert row0 % k == 0
        in_specs.append(pl.BlockSpec((1, 1, k, d), functools.partial(lambda i, rb: (layer, w_index, rb, 0), rb=row0 // k),
                                     pipeline_mode=pl.Buffered(1)))
        row0 += k
    assert row0 == k_total
    gain_spec = pl.BlockSpec((1, d), lambda i: (0, 0))
    row_spec = pl.BlockSpec((bm, d), lambda i: (i, 0))
    in_specs += [row_spec, gain_spec] + ([gain_spec] if handoff else [])
    out_shape = [jax.ShapeDtypeStruct((m, d), F32)] + ([jax.ShapeDtypeStruct((m, d), BF16)] if handoff else [])
    vmem = k_total * d * 2 + 2 * bm * k_total * 2 + 5 * bm * d * 4 + (2 * bm * d * 2 if handoff else 0) + (4 << 20)
    out = pl.pallas_call(
        functools.partial(_outproj_body, n_parts=len(acts), res_scale=res_scale, handoff=handoff),
        out_shape=out_shape,
        grid=(m // bm,),
        in_specs=in_specs,
        out_specs=[row_spec] * len(out_shape),
        compiler_params=_params(("parallel",), vmem),
        name="outproj",
    )(*acts, *([w] * len(acts)), x, g, *([next_gain] if handoff else []))
    return (out[0], out[1]) if handoff else (out[0], None)


CONV_PAD = 8


def _conv_silu(pad_ref, raw, w, b):
    s = raw.shape[0]
    pad_ref[0:CONV_PAD, :] = jnp.zeros((CONV_PAD, raw.shape[1]), F32)
    pad_ref[CONV_PAD:CONV_PAD + s, :] = raw
    acc = raw * w[CONV_K - 1:CONV_K, :] + b
    for shift in range(1, CONV_K):
        acc += pad_ref[CONV_PAD - shift:CONV_PAD - shift + s, :] * w[CONV_K - 1 - shift:CONV_K - shift, :]
    return acc * jax.nn.sigmoid(acc)


GATE_ROWS = 16


def _split3(x):
    hi = x.astype(BF16)
    r1 = x - hi.astype(F32)
    mid = r1.astype(BF16)
    return hi, mid, (r1 - mid.astype(F32)).astype(BF16)


def _mlstm_body(q_ref, k_ref, v_ref, og_ref, gates_ref, gbias_ref, cwq_ref, cbq_ref, cwk_ref, cbk_ref,
                norm_ref, o_ref, qs_ref, ks_ref, vx_ref, gt_ref, bt_ref, row_ref, pad_ref, *, chunk):
    s = q_ref.shape[0]
    nc = s // chunk
    head = pl.program_id(1)
    ext = vx_ref.shape[1]
    qs_ref[...] = _conv_silu(pad_ref, q_ref[...].astype(F32), cwq_ref[...], cbq_ref[...]).astype(BF16)
    ks_ref[...] = _conv_silu(pad_ref, k_ref[...].astype(F32), cwk_ref[...], cbk_ref[...]) * (M_DQK ** -0.5)
    ones_col = (lax.broadcasted_iota(jnp.int32, (s, ext - M_DV), 1) == 0).astype(BF16)
    vx_ref[...] = jnp.concatenate([v_ref[...], ones_col], axis=1)

    upper = (lax.broadcasted_iota(jnp.int32, (chunk, chunk), 0)
             <= lax.broadcasted_iota(jnp.int32, (chunk, chunk), 1)).astype(BF16)
    for c in range(nc):
        cols = slice(c * chunk, (c + 1) * chunk)
        gt = (gates_ref[cols, :] + gbias_ref[...]).T[:GATE_ROWS]
        gt_ref[:, cols] = gt
        bt_ref[:, cols] = sum(_dot(piece, upper) for piece in _split3(_log_sigmoid(gt)))

    i_row = gt_ref[pl.ds(head, 1), :]
    b_row = bt_ref[pl.ds(M_HEADS + head, 1), :]
    r_row = i_row - b_row
    pos = lax.broadcasted_iota(jnp.int32, (1, s), 1) % chunk
    cmax = r_row
    shift = 1
    while shift < chunk:
        cmax = jnp.maximum(cmax, jnp.where(pos >= shift, pltpu.roll(cmax, shift, 1), -jnp.inf))
        shift *= 2

    row_ref[...] = jnp.zeros(row_ref.shape, F32)
    m = jnp.zeros((1, 1), F32)
    ws_rows, decays = [], []
    for c in range(nc):
        cols = slice(c * chunk, (c + 1) * chunk)
        a = jnp.maximum(m, cmax[:, cols])
        a_last = a[:, chunk - 1:]
        b_c = b_row[:, cols]
        row_ref[0:1, cols] = a
        row_ref[1:2, cols] = jnp.exp(m - a)
        row_ref[2:3, cols] = jnp.exp(-(b_c + a))
        ws_rows.append(jnp.exp(r_row[:, cols] - a_last))
        decays.append(jnp.exp(m - a_last))
        m = b_c[:, chunk - 1:] + a_last

    causal = (lax.broadcasted_iota(jnp.int32, (chunk, chunk), 0)
              >= lax.broadcasted_iota(jnp.int32, (chunk, chunk), 1))
    gain = norm_ref[...]

    def independent(c):
        rows = slice(c * chunk, (c + 1) * chunk)
        kf = ks_ref[rows, :]
        s_qk = _dot_nt(qs_ref[rows, :], kf.astype(BF16))
        kv = _dot((kf.T * ws_rows[c]).astype(BF16), vx_ref[rows, :])
        return s_qk, kv

    cmat = None
    nxt = independent(0)
    for c in range(nc):
        rows = slice(c * chunk, (c + 1) * chunk)
        s_qk, kv = nxt
        if c + 1 < nc:
            nxt = independent(c + 1)
        col = row_ref[:, rows].T
        a_col, wi_col, en_col = col[:, 0:1], col[:, 1:2], col[:, 2:3]
        p = jnp.exp(jnp.where(causal, r_row[:, rows] - a_col, -jnp.inf)) * s_qk
        hx = _dot(p.astype(BF16), vx_ref[rows, :])
        if cmat is not None:
            hx += wi_col * _dot(qs_ref[rows, :], cmat.astype(BF16))
        hc = hx[:, :M_DV] / jnp.maximum(jnp.abs(hx[:, M_DV:M_DV + 1]), en_col)
        cmat = kv if cmat is None else decays[c] * cmat + kv
        y = _rmsnorm(hc, gain) * jax.nn.sigmoid(og_ref[rows, :].astype(F32))
        o_ref[rows, :] = y.astype(o_ref.dtype)


def _mlstm(z, gates, gate_bias, conv_w, conv_b, norm, *, batch, seq):
    chunk = min(MLSTM_CHUNK, seq)
    qk_blocks = M_HEADS
    v_block0 = 2 * M_HEADS * M_DQK // M_DV
    og_block0 = v_block0 + M_HEADS
    vmem = seq * (2 * 2 * M_DQK * 2 + 3 * 2 * M_DV * 2 + 3 * LANES * 4 + M_DQK * 6) + (24 << 20)
    return pl.pallas_call(
        functools.partial(_mlstm_body, chunk=chunk),
        out_shape=jax.ShapeDtypeStruct((batch * seq, M_HEADS * M_DV), BF16),
        grid=(batch, M_HEADS),
        in_specs=[
            pl.BlockSpec((seq, M_DQK), lambda b, h: (b, h)),
            pl.BlockSpec((seq, M_DQK), lambda b, h: (b, qk_blocks + h)),
            pl.BlockSpec((seq, M_DV), lambda b, h: (b, v_block0 + h)),
            pl.BlockSpec((seq, M_DV), lambda b, h: (b, og_block0 + h)),
            pl.BlockSpec((seq, LANES), lambda b, h: (b, 0)),
            pl.BlockSpec((1, LANES), lambda b, h: (0, 0)),
            pl.BlockSpec((CONV_K, M_DQK), lambda b, h: (0, h)),
            pl.BlockSpec((1, M_DQK), lambda b, h: (0, h)),
            pl.BlockSpec((CONV_K, M_DQK), lambda b, h: (0, qk_blocks + h)),
            pl.BlockSpec((1, M_DQK), lambda b, h: (0, qk_blocks + h)),
            pl.BlockSpec((1, M_DV), lambda b, h: (0, h)),
        ],
        out_specs=pl.BlockSpec((seq, M_DV), lambda b, h: (b, h)),
        scratch_shapes=[pltpu.VMEM((seq, M_DQK), BF16), pltpu.VMEM((seq, M_DQK), F32),
                        pltpu.VMEM((seq, M_DV + LANES), BF16), pltpu.VMEM((GATE_ROWS, seq), F32),
                        pltpu.VMEM((GATE_ROWS, seq), F32), pltpu.VMEM((LANES, seq), F32),
                        pltpu.VMEM((CONV_PAD + seq, M_DQK), F32)],
        compiler_params=_params(("parallel", "parallel"), vmem),
        name="mlstm",
    )(z, z, z, z, gates, gate_bias, conv_w, conv_b, conv_w, conv_b, norm)


def _t5_bucket_np(n):
    nf = np.maximum(n, 1).astype(np.float64)
    val = np.log(nf / MAX_EXACT) / math.log(MAX_DISTANCE / MAX_EXACT) * (NUM_BUCKETS - MAX_EXACT)
    frac = val[(n > MAX_EXACT) & (n < MAX_DISTANCE)]
    assert np.all(np.abs(frac - np.round(frac)) > 1e-3)
    large = np.minimum(MAX_EXACT + val.astype(np.int64), NUM_BUCKETS - 1)
    return np.where(n < MAX_EXACT, n, large)


def _bucket_tile(blk):
    rel = np.arange(blk)[None, :] + blk - np.arange(2 * blk)[:, None]
    return np.where(rel >= 0, _t5_bucket_np(np.maximum(rel, 0)), -1).astype(np.int32)


def _bias_body(rb_ref, bucket_ref, t_ref):
    h = pl.program_id(0)
    bucket = bucket_ref[...]
    acc = jnp.zeros(bucket.shape, F32)
    for b in range(NUM_BUCKETS):
        acc = jnp.where(bucket == b, rb_ref[h * NUM_BUCKETS + b], acc)
    far = rb_ref[h * NUM_BUCKETS + NUM_BUCKETS - 1]
    t_ref[0] = jnp.where(bucket < 0, -jnp.inf, (acc - far) * LOG2E)


def _bias_tiles(rel_bias, blk):
    assert blk >= MAX_DISTANCE
    bucket = jnp.asarray(_bucket_tile(blk))
    return pl.pallas_call(
        _bias_body,
        out_shape=jax.ShapeDtypeStruct((DA_HEADS, 2 * blk, blk), F32),
        grid=(DA_HEADS,),
        in_specs=[pl.BlockSpec(memory_space=pltpu.SMEM),
                  pl.BlockSpec((2 * blk, blk), lambda h: (0, 0))],
        out_specs=pl.BlockSpec((1, 2 * blk, blk), lambda h: (h, 0, 0)),
        compiler_params=_params(("arbitrary",), 16 << 20),
        name="t5_bias_tiles",
    )(rel_bias.T.reshape(-1), bucket)


VT_ROWS = DA_DV + 16


def _diffattn_body(lam_ref, q_ref, k_ref, v_ref, t_ref, g_ref, o_ref, vt_ref, *, blk, lam_init):
    nb = q_ref.shape[0] // blk
    ones_rows = (lax.broadcasted_iota(jnp.int32, (VT_ROWS - DA_DV, blk), 0) == 0).astype(F32)
    for c in range(nb):
        vt = v_ref[c * blk:(c + 1) * blk, :].astype(F32).T
        vt_ref[c] = jnp.concatenate([vt, ones_rows], axis=0).astype(BF16)

    lp = lam_ref[...]
    lam = (jnp.exp(jnp.sum(lp[0:1] * lp[1:2], axis=-1, keepdims=True))
           - jnp.exp(jnp.sum(lp[2:3] * lp[3:4], axis=-1, keepdims=True)) + lam_init)
    sub = lax.broadcasted_iota(jnp.int32, (LANES, blk), 0)
    gain = g_ref[...]

    def query_maps(qi):
        qt = q_ref[qi * blk:(qi + 1) * blk, :].astype(F32).T
        return (jnp.where(sub < DA_D, qt, 0.0).astype(BF16), jnp.where(sub >= DA_D, qt, 0.0).astype(BF16))

    def scores(qts, qi, kc):
        kb = k_ref[kc * blk:(kc + 1) * blk, :]
        s = [_dot(kb, qt_m) for qt_m in qts]
        if kc >= qi - 1:
            bias = t_ref[0, blk:, :] if kc == qi else t_ref[0, :blk, :]
            s = [s_m + bias for s_m in s]
        return s

    units = [(qi, kc) for qi in range(nb) for kc in range(qi, -1, -1)]
    qts_of = {}

    def unit_scores(u):
        qi, kc = units[u]
        if qi not in qts_of:
            qts_of.clear()
            qts_of[qi] = query_maps(qi)
        return scores(qts_of[qi], qi, kc)

    ahead = [unit_scores(u) for u in range(min(DA_LOOKAHEAD, len(units)))]
    state = [None, None]
    for u, (qi, kc) in enumerate(units):
        s_cur = ahead.pop(0)
        if u + DA_LOOKAHEAD < len(units):
            ahead.append(unit_scores(u + DA_LOOKAHEAD))
        for a in range(2):
            s = s_cur[a]
            blk_max = jnp.max(s, axis=0, keepdims=True)
            if state[a] is None:
                m_new = blk_max
                acc = _dot(vt_ref[kc], jnp.exp2(s - m_new).astype(BF16))
            else:
                m, acc = state[a]
                m_new = jnp.maximum(m, blk_max)
                acc = acc * jnp.exp2(m - m_new) + _dot(vt_ref[kc], jnp.exp2(s - m_new).astype(BF16))
            state[a] = (m_new, acc)
        if kc == 0:
            outs = [acc[:DA_DV] / acc[DA_DV:DA_DV + 1] for _, acc in state]
            o = (outs[0] - lam * outs[1]).T
            o_ref[qi * blk:(qi + 1) * blk, :] = (_rmsnorm(o, gain) * (1.0 - lam_init)).astype(o_ref.dtype)
            state = [None, None]


def _diffattn(z, tiles, lam_p, subln, *, batch, seq, lam_init, col0):
    blk = tiles.shape[2]
    q0 = col0 // LANES
    k0 = q0 + DA_HEADS
    v0 = k0 + DA_HEADS
    vmem = 8 * seq * LANES * 2 + seq * VT_ROWS * 2 + 4 * blk * blk * 4 + 24 * blk * blk * 4 + (8 << 20)
    return pl.pallas_call(
        functools.partial(_diffattn_body, blk=blk, lam_init=lam_init),
        out_shape=jax.ShapeDtypeStruct((batch * seq, DA_HEADS * DA_DV), BF16),
        grid=(batch, DA_HEADS),
        in_specs=[
            pl.BlockSpec((4, DA_D), lambda b, h: (0, 0)),
            pl.BlockSpec((seq, LANES), lambda b, h: (b, q0 + h)),
            pl.BlockSpec((seq, LANES), lambda b, h: (b, k0 + h)),
            pl.BlockSpec((seq, LANES), lambda b, h: (b, v0 + h)),
            pl.BlockSpec((1, 2 * blk, blk), lambda b, h: (h, 0, 0)),
            pl.BlockSpec((1, DA_DV), lambda b, h: (0, 0)),
        ],
        out_specs=pl.BlockSpec((seq, DA_DV), lambda b, h: (b, h)),
        scratch_shapes=[pltpu.VMEM((seq // blk, VT_ROWS, blk), BF16)],
        compiler_params=_params(("parallel", "parallel"), vmem),
        name="diffattn",
    )(lam_p, z, z, z, tiles, subln)


def _xattn_body(q_ref, k_ref, v_ref, o_ref, *, blk):
    nb = q_ref.shape[0] // blk
    kb = k_ref[...]
    vb = v_ref[...]
    s_next = _dot_nt(q_ref[0:blk, :], kb)
    for j in range(nb):
        s = s_next
        if j + 1 < nb:
            s_next = _dot_nt(q_ref[(j + 1) * blk:(j + 2) * blk, :], kb)
        p = jnp.exp2(s - jnp.max(s, axis=-1, keepdims=True))
        o = _dot(p.astype(BF16), vb) / jnp.sum(p, axis=-1, keepdims=True)
        o_ref[j * blk:(j + 1) * blk, :] = o.astype(o_ref.dtype)


def _xattn(q, kv, *, batch, seq, mem_len, blk=256):
    d = q.shape[1]
    dh = d // X_HEADS
    blk = min(blk, seq)
    vmem = 4 * seq * dh * 2 + 4 * mem_len * dh * 2 + 8 * blk * (mem_len + dh) * 4 + (8 << 20)
    return pl.pallas_call(
        functools.partial(_xattn_body, blk=blk),
        out_shape=jax.ShapeDtypeStruct((batch * seq, d), BF16),
        grid=(batch, X_HEADS),
        in_specs=[
            pl.BlockSpec((seq, dh), lambda b, h: (b, h)),
            pl.BlockSpec((mem_len, dh), lambda b, h: (b, h)),
            pl.BlockSpec((mem_len, dh), lambda b, h: (b, X_HEADS + h)),
        ],
        out_specs=pl.BlockSpec((seq, dh), lambda b, h: (b, h)),
        compiler_params=_params(("parallel", "parallel"), vmem),
        name="xattn",
    )(q, kv, kv)


def kernel(x, mem, rel_bias, ffn1_norm_pre, ffn1_norm_post, ffn1_w_gate, ffn1_w_up, ffn1_w_down, mix_norm_pre, mix_norm_post, w_in, conv_w, conv_b, b_igate, b_fgate, mlstm_norm, diff_lambda, diff_subln, w_out, xattn_norm_pre, xattn_norm_post, mem_norm, xattn_wq, xattn_wk, xattn_wv, xattn_wo, ffn2_norm_pre, ffn2_norm_post, ffn2_w_gate, ffn2_w_up, ffn2_w_down):
    batch, seq, d = x.shape
    mem_len = mem.shape[1]
    depth = w_in.shape[0]
    n_gate = 2 * M_HEADS
    da_qw = DA_HEADS * 2 * DA_D
    gate0 = 2 * M_HEADS * M_DQK + 2 * M_HEADS * M_DV
    xf = x.reshape(batch * seq, d)
    memf = mem.reshape(batch * mem_len, d)
    row = lambda v: v.reshape(1, -1).astype(F32)

    tiles = _bias_tiles(rel_bias, min(DA_BLOCK, seq))
    w_main, w_gate = _cast_w_in(w_in, gate0=gate0, n_gate=n_gate, scaled=da_qw, scale=DA_D ** -0.5 * LOG2E)
    w_down = _cast_blocks([ffn1_w_down, ffn2_w_down])
    ffn1 = (_cast_blocks([ffn1_w_gate, ffn1_w_up], FFN_BLOCK, interleave=True), w_down, 0)
    ffn2 = (_cast_blocks([ffn2_w_gate, ffn2_w_up], FFN_BLOCK, interleave=True), w_down, 1)
    w_att = _cast_blocks([w_out, xattn_wo, xattn_wq, xattn_wk, xattn_wv],
                         scales=[None, None, (d // X_HEADS) ** -0.5 * LOG2E, None, None])

    xn = None
    for l in range(depth):
        lam_init = 0.8 - 0.6 * math.exp(-0.3 * l)
        xf, _ = _ffn(xf, row(ffn1_norm_pre[l]), row(ffn1_norm_post[l]), *ffn1, l, xn=xn)

        z, gates = _normproj(xf, row(mix_norm_pre[l]), w_main, l, w_gate, bm=INPROJ_ROWS)
        gate_bias = jnp.pad(jnp.concatenate([b_igate[l], b_fgate[l]]), (0, LANES - n_gate)).reshape(1, LANES)
        y_m = _mlstm(z, gates, gate_bias.astype(F32), conv_w[l].astype(F32), row(conv_b[l]),
                     row(mlstm_norm[l]), batch=batch, seq=seq)
        y_d = _diffattn(z, tiles, diff_lambda[l].astype(F32), row(diff_subln[l]),
                        batch=batch, seq=seq, lam_init=lam_init, col0=gate0)
        xf, _ = _outproj(xf, row(mix_norm_post[l]), [y_m, y_d], w_att, l, w_index=0)

        q = _normproj(xf, row(xattn_norm_pre[l]), w_att, l, block0=2, nb=1, bm=ATT_PROJ_ROWS)
        kv = _normproj(memf, row(mem_norm[l]), w_att, l, block0=3, nb=2, bm=ATT_PROJ_ROWS)
        c = _xattn(q, kv, batch=batch, seq=seq, mem_len=mem_len)
        xf, xn2 = _outproj(xf, row(xattn_norm_post[l]), [c], w_att, l, w_index=1, next_gain=row(ffn2_norm_pre[l]))

        next_gain = row(ffn1_norm_pre[l + 1]) if l + 1 < depth else None
        xf, xn = _ffn(xf, row(ffn2_norm_pre[l]), row(ffn2_norm_post[l]), *ffn2, l, xn=xn2, next_gain=next_gain)
    return xf.reshape(batch, seq, d)
```

```python
import functools
import math

import numpy as np
import jax
import jax.numpy as jnp
from jax import lax
from jax.experimental import pallas as pl
from jax.experimental.pallas import tpu as pltpu

F32 = jnp.float32
BF16 = jnp.bfloat16
EPS = 1e-6
LOG2E = math.log2(math.e)

M_HEADS = 4
M_DQK = 128
M_DV = 256
CONV_K = 4
DA_HEADS = 8
DA_D = 64
DA_DV = 128
X_HEADS = 4
NUM_BUCKETS = 32
MAX_EXACT = NUM_BUCKETS // 2
MAX_DISTANCE = 128

LANES = 128
MXU_N = 256
V7X_VMEM_BYTES = 64 * 2 ** 20
MLSTM_CHUNK = 128
DA_BLOCK = 256
DA_LOOKAHEAD = 3
FFN_BLOCK = 1408
FFN_ROWS = 512
FFN_ROWS_PRENORMED = 1024
DOWN_ROWS = 512
ATT_PROJ_ROWS = 512
INPROJ_BLOCK = 3072
INPROJ_ROWS = 256


def _params(semantics, vmem_bytes):
    assert vmem_bytes <= V7X_VMEM_BYTES
    return pltpu.CompilerParams(dimension_semantics=semantics, vmem_limit_bytes=int(vmem_bytes))


def _rmsnorm(xf, g):
    return xf * lax.rsqrt(jnp.mean(xf * xf, axis=-1, keepdims=True) + EPS) * g


def _log_sigmoid(x):
    return jnp.minimum(x, 0.0) - jnp.log1p(jnp.exp(-jnp.abs(x)))


def _dot(a, b):
    return jnp.dot(a, b, preferred_element_type=F32)


def _dot_nt(a, b):
    return lax.dot_general(a, b, (((1,), (1,)), ((), ())), preferred_element_type=F32)


def _cast_body(*refs, bn, scales, interleave):
    *in_refs, o_ref = refs
    cast = lambda blk, scale: (blk if scale is None else blk * scale).astype(BF16)
    if interleave:
        for b in range(in_refs[0].shape[2] // bn):
            for t, x_ref in enumerate(in_refs):
                o_ref[0, b, :, t * bn:(t + 1) * bn] = cast(x_ref[0, :, b * bn:(b + 1) * bn], scales[t])
    else:
        c = 0
        for t, x_ref in enumerate(in_refs):
            for b in range(x_ref.shape[2] // bn):
                o_ref[0, c] = cast(x_ref[0, :, b * bn:(b + 1) * bn], scales[t])
                c += 1


def _cast_blocks(ws, bn=None, *, scales=None, interleave=False, kb=256):
    depth, k, _ = ws[0].shape
    bn = bn or ws[0].shape[2]
    n_total = sum(w.shape[2] for w in ws)
    width = bn * len(ws) if interleave else bn
    kb = min(kb, k)
    vmem = 2 * kb * n_total * (4 + 2) + (4 << 20)
    return pl.pallas_call(
        functools.partial(_cast_body, bn=bn, scales=scales or [None] * len(ws), interleave=interleave),
        out_shape=jax.ShapeDtypeStruct((depth, n_total // width, k, width), BF16),
        grid=(depth, k // kb),
        in_specs=[pl.BlockSpec((1, kb, w.shape[2]), lambda l, i: (l, i, 0)) for w in ws],
        out_specs=pl.BlockSpec((1, n_total // width, kb, width), lambda l, i: (l, 0, i, 0)),
        compiler_params=_params(("parallel", "parallel"), vmem),
        name="cast_blocks",
    )(*ws)


def _cast_w_in_body(x_ref, o_ref, gate_ref, *, gate0, n_gate, bn, scaled, scale):
    n, kb = x_ref.shape[1:]
    starts = list(range(0, gate0, bn)) + list(range(gate0 + n_gate, n, bn))
    q0 = gate0 + n_gate
    for c, r0 in enumerate(starts):
        blk = x_ref[0, r0:r0 + bn, :]
        lo, hi = max(q0, r0) - r0, min(q0 + scaled, r0 + bn) - r0
        if lo < hi:
            pieces = [(0, lo, None), (lo, hi, scale), (hi, bn, None)]
            blk = jnp.concatenate([blk[a:b] if s is None else blk[a:b] * s for a, b, s in pieces if a < b], axis=0)
        o_ref[0, c] = blk.T.astype(BF16)
    gates = jnp.concatenate([x_ref[0, gate0:gate0 + n_gate, :], jnp.zeros((LANES - n_gate, kb), F32)], axis=0)
    gate_ref[0] = gates.T.astype(BF16)


def _cast_w_in(w_in, *, gate0, n_gate, scaled, scale, bn=INPROJ_BLOCK, kb=256):
    depth, k, n = w_in.shape
    n_main = n - n_gate
    kb = min(kb, k)
    assert gate0 % bn == 0 and (n - gate0 - n_gate) % bn == 0 and (gate0 + n_gate) % 8 == 0
    vmem = 2 * kb * n * 4 + 2 * kb * n_main * 2 + 3 * kb * n * 4 + (4 << 20)
    return pl.pallas_call(
        functools.partial(_cast_w_in_body, gate0=gate0, n_gate=n_gate, bn=bn, scaled=scaled, scale=scale),
        out_shape=[jax.ShapeDtypeStruct((depth, n_main // bn, k, bn), BF16),
                   jax.ShapeDtypeStruct((depth, k, LANES), BF16)],
        grid=(depth, k // kb),
        in_specs=[pl.BlockSpec((1, n, kb), lambda l, i: (l, 0, i))],
        out_specs=[pl.BlockSpec((1, n_main // bn, kb, bn), lambda l, i: (l, 0, i, 0)),
                   pl.BlockSpec((1, kb, LANES), lambda l, i: (l, i, 0))],
        compiler_params=_params(("parallel", "parallel"), vmem),
        name="cast_w_in",
    )(jnp.swapaxes(w_in, 1, 2))


def _ffn_up_body(*refs, prenormed):
    if prenormed:
        xn_ref, w_ref, h_ref = refs
    else:
        x_ref, g_ref, w_ref, h_ref, xn_ref = refs

        @pl.when(pl.program_id(1) == 0)
        def _():
            xn_ref[...] = _rmsnorm(x_ref[...], g_ref[...]).astype(BF16)

    bf = h_ref.shape[1]
    gu = _dot(xn_ref[...], w_ref[0, 0])
    g, u = gu[:, :bf], gu[:, bf:]
    h_ref[...] = (g * jax.nn.sigmoid(g) * u).astype(h_ref.dtype)


def _ffn(x, g_pre, g_post, w_gate_up, w_down, down_index, layer, *, xn=None, next_gain=None):
    m, d = x.shape
    nf, bf = w_gate_up.shape[1], w_gate_up.shape[3] // 2
    f = nf * bf
    prenormed = xn is not None
    bm = min(FFN_ROWS_PRENORMED if prenormed else FFN_ROWS, m)
    assert m % bm == 0
    w_spec = pl.BlockSpec((1, 1, d, 2 * bf), lambda i, j: (layer, j, 0, 0))
    row_spec = pl.BlockSpec((bm, d), lambda i, j: (i, 0))
    if prenormed:
        in_specs, args, scratch = [row_spec, w_spec], (xn, w_gate_up), []
        vmem_up = 2 * bm * d * 2 + 4 * d * bf * 2 + 2 * bm * bf * 2 + 3 * bm * bf * 4 + (4 << 20)
    else:
        in_specs = [row_spec, pl.BlockSpec((1, d), lambda i, j: (0, 0)), w_spec]
        args, scratch = (x, g_pre, w_gate_up), [pltpu.VMEM((bm, d), BF16)]
        vmem_up = 2 * bm * d * 4 + bm * d * 2 + 4 * d * bf * 2 + 2 * bm * bf * 2 + 3 * bm * bf * 4 + (4 << 20)
    hidden = pl.pallas_call(
        functools.partial(_ffn_up_body, prenormed=prenormed),
        out_shape=jax.ShapeDtypeStruct((m, f), BF16),
        grid=(m // bm, nf),
        in_specs=in_specs,
        out_specs=pl.BlockSpec((bm, bf), lambda i, j: (i, j)),
        scratch_shapes=scratch,
        compiler_params=_params(("parallel", "arbitrary"), vmem_up),
        name="ffn_up",
    )(*args)
    return _outproj(x, g_post, [hidden], w_down, layer, w_index=down_index, res_scale=0.5, next_gain=next_gain,
                    bm=DOWN_ROWS)


def _normproj_body(x_ref, g_ref, w_ref, *rest, with_gate):
    if with_gate:
        wgate_ref, o_ref, gate_ref, xn_ref = rest
    else:
        o_ref, xn_ref = rest
    j = pl.program_id(1)

    @pl.when(j == 0)
    def _():
        xn = _rmsnorm(x_ref[...], g_ref[...]).astype(BF16)
        xn_ref[...] = xn
        if with_gate:
            gate_ref[...] = _dot(xn, wgate_ref[0])

    bn = w_ref.shape[3]
    for b in range(w_ref.shape[1]):
        o_ref[:, b * bn:(b + 1) * bn] = _dot(xn_ref[...], w_ref[0, b]).astype(o_ref.dtype)


def _normproj(x, g, w, layer, w_gate=None, *, block0=0, nb=None, per_step=1, bm=1024):
    m, d = x.shape
    nb, bn = nb or w.shape[1], w.shape[3]
    bm = min(bm, m)
    assert m % bm == 0 and nb % per_step == 0 and block0 % per_step == 0
    steps = nb // per_step
    with_gate = w_gate is not None
    in_specs = [
        pl.BlockSpec((bm, d), lambda i, j: (i, 0)),
        pl.BlockSpec((1, d), lambda i, j: (0, 0)),
        pl.BlockSpec((1, per_step, d, bn), lambda i, j: (layer, block0 // per_step + j, 0, 0),
                     **({"pipeline_mode": pl.Buffered(1)} if steps == 1 else {})),
    ]
    out_shape = [jax.ShapeDtypeStruct((m, nb * bn), BF16)]
    out_specs = [pl.BlockSpec((bm, per_step * bn), lambda i, j: (i, j))]
    args = [x, g, w]
    if with_gate:
        in_specs.append(pl.BlockSpec((1, d, LANES), lambda i, j: (layer, 0, 0)))
        out_shape.append(jax.ShapeDtypeStruct((m, LANES), F32))
        out_specs.append(pl.BlockSpec((bm, LANES), lambda i, j: (i, 0)))
        args.append(w_gate)
    wn = per_step * bn
    w_bufs = 1 if steps == 1 else 2
    vmem = 2 * bm * d * 4 + bm * d * 2 + w_bufs * d * wn * 2 + 2 * bm * wn * 2 + bm * wn * 4 + (6 << 20)
    out = pl.pallas_call(
        functools.partial(_normproj_body, with_gate=with_gate),
        out_shape=out_shape,
        grid=(m // bm, steps),
        in_specs=in_specs,
        out_specs=out_specs,
        scratch_shapes=[pltpu.VMEM((bm, d), BF16)],
        compiler_params=_params(("parallel", "arbitrary"), vmem),
        name="normproj_gate" if with_gate else "normproj",
    )(*args)
    return out if with_gate else out[0]


def _outproj_body(*refs, n_parts, res_scale, handoff):
    a_refs = refs[:n_parts]
    w_refs = refs[n_parts:2 * n_parts]
    if handoff:
        x_ref, g_ref, gnext_ref, o_ref, xn_ref = refs[2 * n_parts:]
    else:
        x_ref, g_ref, o_ref = refs[2 * n_parts:]
    y = _dot(a_refs[0][...], w_refs[0][0, 0])
    for a_ref, w_ref in zip(a_refs[1:], w_refs[1:]):
        y += _dot(a_ref[...], w_ref[0, 0])
    out = x_ref[...] + res_scale * _rmsnorm(y, g_ref[...])
    o_ref[...] = out
    if handoff:
        xn_ref[...] = _rmsnorm(out, gnext_ref[...]).astype(xn_ref.dtype)


def _outproj(x, g, acts, w, layer, *, w_index=0, res_scale=1.0, next_gain=None, bm=512):
    m, d = x.shape
    bm = min(bm, m)
    assert m % bm == 0
    handoff = next_gain is not None
    k_total = w.shape[2]
    in_specs = [pl.BlockSpec((bm, a.shape[1]), lambda i: (i, 0)) for a in acts]
    row0 = 0
    for a in acts:
        k = a.shape[1]
        assert row0 % k == 0
        in_specs.append(pl.BlockSpec((1, 1, k, d), functools.partial(lambda i, rb: (layer, w_index, rb, 0), rb=row0 // k),
                                     pipeline_mode=pl.Buffered(1)))
        row0 += k
    assert row0 == k_total
    gain_spec = pl.BlockSpec((1, d), lambda i: (0, 0))
    row_spec = pl.BlockSpec((bm, d), lambda i: (i, 0))
    in_specs += [row_spec, gain_spec] + ([gain_spec] if handoff else [])
    out_shape = [jax.ShapeDtypeStruct((m, d), F32)] + ([jax.ShapeDtypeStruct((m, d), BF16)] if handoff else [])
    vmem = k_total * d * 2 + 2 * bm * k_total * 2 + 5 * bm * d * 4 + (2 * bm * d * 2 if handoff else 0) + (4 << 20)
    out = pl.pallas_call(
        functools.partial(_outproj_body, n_parts=len(acts), res_scale=res_scale, handoff=handoff),
        out_shape=out_shape,
        grid=(m // bm,),
        in_specs=in_specs,
        out_specs=[row_spec] * len(out_shape),
        compiler_params=_params(("parallel",), vmem),
        name="outproj",
    )(*acts, *([w] * len(acts)), x, g, *([next_gain] if handoff else []))
    return (out[0], out[1]) if handoff else (out[0], None)


CONV_PAD = 8


def _conv_silu(pad_ref, raw, w, b):
    s = raw.shape[0]
    pad_ref[0:CONV_PAD, :] = jnp.zeros((CONV_PAD, raw.shape[1]), F32)
    pad_ref[CONV_PAD:CONV_PAD + s, :] = raw
    acc = raw * w[CONV_K - 1:CONV_K, :] + b
    for shift in range(1, CONV_K):
        acc += pad_ref[CONV_PAD - shift:CONV_PAD - shift + s, :] * w[CONV_K - 1 - shift:CONV_K - shift, :]
    return acc * jax.nn.sigmoid(acc)


GATE_ROWS = 16


def _split3(x):
    hi = x.astype(BF16)
    r1 = x - hi.astype(F32)
    mid = r1.astype(BF16)
    return hi, mid, (r1 - mid.astype(F32)).astype(BF16)


def _mlstm_body(q_ref, k_ref, v_ref, og_ref, gates_ref, gbias_ref, cwq_ref, cbq_ref, cwk_ref, cbk_ref,
                norm_ref, o_ref, qs_ref, ks_ref, vx_ref, gt_ref, bt_ref, row_ref, pad_ref, *, chunk):
    s = q_ref.shape[0]
    nc = s // chunk
    head = pl.program_id(1)
    ext = vx_ref.shape[1]
    qs_ref[...] = _conv_silu(pad_ref, q_ref[...].astype(F32), cwq_ref[...], cbq_ref[...]).astype(BF16)
    ks_ref[...] = _conv_silu(pad_ref, k_ref[...].astype(F32), cwk_ref[...], cbk_ref[...]) * (M_DQK ** -0.5)
    ones_col = (lax.broadcasted_iota(jnp.int32, (s, ext - M_DV), 1) == 0).astype(BF16)
    vx_ref[...] = jnp.concatenate([v_ref[...], ones_col], axis=1)

    upper = (lax.broadcasted_iota(jnp.int32, (chunk, chunk), 0)
             <= lax.broadcasted_iota(jnp.int32, (chunk, chunk), 1)).astype(BF16)
    for c in range(nc):
        cols = slice(c * chunk, (c + 1) * chunk)
        gt = (gates_ref[cols, :] + gbias_ref[...]).T[:GATE_ROWS]
        gt_ref[:, cols] = gt
        bt_ref[:, cols] = sum(_dot(piece, upper) for piece in _split3(_log_sigmoid(gt)))

    i_row = gt_ref[pl.ds(head, 1), :]
    b_row = bt_ref[pl.ds(M_HEADS + head, 1), :]
    r_row = i_row - b_row
    pos = lax.broadcasted_iota(jnp.int32, (1, s), 1) % chunk
    cmax = r_row
    shift = 1
    while shift < chunk:
        cmax = jnp.maximum(cmax, jnp.where(pos >= shift, pltpu.roll(cmax, shift, 1), -jnp.inf))
        shift *= 2

    row_ref[...] = jnp.zeros(row_ref.shape, F32)
    m = jnp.zeros((1, 1), F32)
    ws_rows, decays = [], []
    for c in range(nc):
        cols = slice(c * chunk, (c + 1) * chunk)
        a = jnp.maximum(m, cmax[:, cols])
        a_last = a[:, chunk - 1:]
        b_c = b_row[:, cols]
        row_ref[0:1, cols] = a
        row_ref[1:2, cols] = jnp.exp(m - a)
        row_ref[2:3, cols] = jnp.exp(-(b_c + a))
        ws_rows.append(jnp.exp(r_row[:, cols] - a_last))
        decays.append(jnp.exp(m - a_last))
        m = b_c[:, chunk - 1:] + a_last

    causal = (lax.broadcasted_iota(jnp.int32, (chunk, chunk), 0)
              >= lax.broadcasted_iota(jnp.int32, (chunk, chunk), 1))
    gain = norm_ref[...]

    def independent(c):
        rows = slice(c * chunk, (c + 1) * chunk)
        kf = ks_ref[rows, :]
        s_qk = _dot_nt(qs_ref[rows, :], kf.astype(BF16))
        kv = _dot((kf.T * ws_rows[c]).astype(BF16), vx_ref[rows, :])
        return s_qk, kv

    cmat = None
    nxt = independent(0)
    for c in range(nc):
        rows = slice(c * chunk, (c + 1) * chunk)
        s_qk, kv = nxt
        if c + 1 < nc:
            nxt = independent(c + 1)
        col = row_ref[:, rows].T
        a_col, wi_col, en_col = col[:, 0:1], col[:, 1:2], col[:, 2:3]
        p = jnp.exp(jnp.where(causal, r_row[:, rows] - a_col, -jnp.inf)) * s_qk
        hx = _dot(p.astype(BF16), vx_ref[rows, :])
        if cmat is not None:
            hx += wi_col * _dot(qs_ref[rows, :], cmat.astype(BF16))
        hc = hx[:, :M_DV] / jnp.maximum(jnp.abs(hx[:, M_DV:M_DV + 1]), en_col)
        cmat = kv if cmat is None else decays[c] * cmat + kv
        y = _rmsnorm(hc, gain) * jax.nn.sigmoid(og_ref[rows, :].astype(F32))
        o_ref[rows, :] = y.astype(o_ref.dtype)


def _mlstm(z, gates, gate_bias, conv_w, conv_b, norm, *, batch, seq):
    chunk = min(MLSTM_CHUNK, seq)
    qk_blocks = M_HEADS
    v_block0 = 2 * M_HEADS * M_DQK // M_DV
    og_block0 = v_block0 + M_HEADS
    vmem = seq * (2 * 2 * M_DQK * 2 + 3 * 2 * M_DV * 2 + 3 * LANES * 4 + M_DQK * 6) + (24 << 20)
    return pl.pallas_call(
        functools.partial(_mlstm_body, chunk=chunk),
        out_shape=jax.ShapeDtypeStruct((batch * seq, M_HEADS * M_DV), BF16),
        grid=(batch, M_HEADS),
        in_specs=[
            pl.BlockSpec((seq, M_DQK), lambda b, h: (b, h)),
            pl.BlockSpec((seq, M_DQK), lambda b, h: (b, qk_blocks + h)),
            pl.BlockSpec((seq, M_DV), lambda b, h: (b, v_block0 + h)),
            pl.BlockSpec((seq, M_DV), lambda b, h: (b, og_block0 + h)),
            pl.BlockSpec((seq, LANES), lambda b, h: (b, 0)),
            pl.BlockSpec((1, LANES), lambda b, h: (0, 0)),
            pl.BlockSpec((CONV_K, M_DQK), lambda b, h: (0, h)),
            pl.BlockSpec((1, M_DQK), lambda b, h: (0, h)),
            pl.BlockSpec((CONV_K, M_DQK), lambda b, h: (0, qk_blocks + h)),
            pl.BlockSpec((1, M_DQK), lambda b, h: (0, qk_blocks + h)),
            pl.BlockSpec((1, M_DV), lambda b, h: (0, h)),
        ],
        out_specs=pl.BlockSpec((seq, M_DV), lambda b, h: (b, h)),
        scratch_shapes=[pltpu.VMEM((seq, M_DQK), BF16), pltpu.VMEM((seq, M_DQK), F32),
                        pltpu.VMEM((seq, M_DV + LANES), BF16), pltpu.VMEM((GATE_ROWS, seq), F32),
                        pltpu.VMEM((GATE_ROWS, seq), F32), pltpu.VMEM((LANES, seq), F32),
                        pltpu.VMEM((CONV_PAD + seq, M_DQK), F32)],
        compiler_params=_params(("parallel", "parallel"), vmem),
        name="mlstm",
    )(z, z, z, z, gates, gate_bias, conv_w, conv_b, conv_w, conv_b, norm)


def _t5_bucket_np(n):
    nf = np.maximum(n, 1).astype(np.float64)
    val = np.log(nf / MAX_EXACT) / math.log(MAX_DISTANCE / MAX_EXACT) * (NUM_BUCKETS - MAX_EXACT)
    frac = val[(n > MAX_EXACT) & (n < MAX_DISTANCE)]
    assert np.all(np.abs(frac - np.round(frac)) > 1e-3)
    large = np.minimum(MAX_EXACT + val.astype(np.int64), NUM_BUCKETS - 1)
    return np.where(n < MAX_EXACT, n, large)


def _bucket_tile(blk):
    rel = np.arange(blk)[None, :] + blk - np.arange(2 * blk)[:, None]
    return np.where(rel >= 0, _t5_bucket_np(np.maximum(rel, 0)), -1).astype(np.int32)


def _bias_body(rb_ref, bucket_ref, t_ref):
    h = pl.program_id(0)
    bucket = bucket_ref[...]
    acc = jnp.zeros(bucket.shape, F32)
    for b in range(NUM_BUCKETS):
        acc = jnp.where(bucket == b, rb_ref[h * NUM_BUCKETS + b], acc)
    far = rb_ref[h * NUM_BUCKETS + NUM_BUCKETS - 1]
    t_ref[0] = jnp.where(bucket < 0, -jnp.inf, (acc - far) * LOG2E)


def _bias_tiles(rel_bias, blk):
    assert blk >= MAX_DISTANCE
    bucket = jnp.asarray(_bucket_tile(blk))
    return pl.pallas_call(
        _bias_body,
        out_shape=jax.ShapeDtypeStruct((DA_HEADS, 2 * blk, blk), F32),
        grid=(DA_HEADS,),
        in_specs=[pl.BlockSpec(memory_space=pltpu.SMEM),
                  pl.BlockSpec((2 * blk, blk), lambda h: (0, 0))],
        out_specs=pl.BlockSpec((1, 2 * blk, blk), lambda h: (h, 0, 0)),
        compiler_params=_params(("arbitrary",), 16 << 20),
        name="t5_bias_tiles",
    )(rel_bias.T.reshape(-1), bucket)


VT_ROWS = DA_DV + 16


def _diffattn_body(lam_ref, q_ref, k_ref, v_ref, t_ref, g_ref, o_ref, vt_ref, *, blk, lam_init):
    nb = q_ref.shape[0] // blk
    ones_rows = (lax.broadcasted_iota(jnp.int32, (VT_ROWS - DA_DV, blk), 0) == 0).astype(F32)
    for c in range(nb):
        vt = v_ref[c * blk:(c + 1) * blk, :].astype(F32).T
        vt_ref[c] = jnp.concatenate([vt, ones_rows], axis=0).astype(BF16)

    lp = lam_ref[...]
    lam = (jnp.exp(jnp.sum(lp[0:1] * lp[1:2], axis=-1, keepdims=True))
           - jnp.exp(jnp.sum(lp[2:3] * lp[3:4], axis=-1, keepdims=True)) + lam_init)
    sub = lax.broadcasted_iota(jnp.int32, (LANES, blk), 0)
    gain = g_ref[...]

    def query_maps(qi):
        qt = q_ref[qi * blk:(qi + 1) * blk, :].astype(F32).T
        return (jnp.where(sub < DA_D, qt, 0.0).astype(BF16), jnp.where(sub >= DA_D, qt, 0.0).astype(BF16))

    def scores(qts, qi, kc):
        kb = k_ref[kc * blk:(kc + 1) * blk, :]
        s = [_dot(kb, qt_m) for qt_m in qts]
        if kc >= qi - 1:
            bias = t_ref[0, blk:, :] if kc == qi else t_ref[0, :blk, :]
            s = [s_m + bias for s_m in s]
        return s

    units = [(qi, kc) for qi in range(nb) for kc in range(qi, -1, -1)]
    qts_of = {}

    def unit_scores(u):
        qi, kc = units[u]
        if qi not in qts_of:
            qts_of.clear()
            qts_of[qi] = query_maps(qi)
        return scores(qts_of[qi], qi, kc)

    ahead = [unit_scores(u) for u in range(min(DA_LOOKAHEAD, len(units)))]
    state = [None, None]
    for u, (qi, kc) in enumerate(units):
        s_cur = ahead.pop(0)
        if u + DA_LOOKAHEAD < len(units):
            ahead.append(unit_scores(u + DA_LOOKAHEAD))
        for a in range(2):
            s = s_cur[a]
            blk_max = jnp.max(s, axis=0, keepdims=True)
            if state[a] is None:
                m_new = blk_max
                acc = _dot(vt_ref[kc], jnp.exp2(s - m_new).astype(BF16))
            else:
                m, acc = state[a]
                m_new = jnp.maximum(m, blk_max)
                acc = acc * jnp.exp2(m - m_new) + _dot(vt_ref[kc], jnp.exp2(s - m_new).astype(BF16))
            state[a] = (m_new, acc)
        if kc == 0:
            outs = [acc[:DA_DV] / acc[DA_DV:DA_DV + 1] for _, acc in state]
            o = (outs[0] - lam * outs[1]).T
            o_ref[qi * blk:(qi + 1) * blk, :] = (_rmsnorm(o, gain) * (1.0 - lam_init)).astype(o_ref.dtype)
            state = [None, None]


def _diffattn(z, tiles, lam_p, subln, *, batch, seq, lam_init, col0):
    blk = tiles.shape[2]
    q0 = col0 // LANES
    k0 = q0 + DA_HEADS
    v0 = k0 + DA_HEADS
    vmem = 8 * seq * LANES * 2 + seq * VT_ROWS * 2 + 4 * blk * blk * 4 + 24 * blk * blk * 4 + (8 << 20)
    return pl.pallas_call(
        functools.partial(_diffattn_body, blk=blk, lam_init=lam_init),
        out_shape=jax.ShapeDtypeStruct((batch * seq, DA_HEADS * DA_DV), BF16),
        grid=(batch, DA_HEADS),
        in_specs=[
            pl.BlockSpec((4, DA_D), lambda b, h: (0, 0)),
            pl.BlockSpec((seq, LANES), lambda b, h: (b, q0 + h)),
            pl.BlockSpec((seq, LANES), lambda b, h: (b, k0 + h)),
            pl.BlockSpec((seq, LANES), lambda b, h: (b, v0 + h)),
            pl.BlockSpec((1, 2 * blk, blk), lambda b, h: (h, 0, 0)),
            pl.BlockSpec((1, DA_DV), lambda b, h: (0, 0)),
        ],
        out_specs=pl.BlockSpec((seq, DA_DV), lambda b, h: (b, h)),
        scratch_shapes=[pltpu.VMEM((seq // blk, VT_ROWS, blk), BF16)],
        compiler_params=_params(("parallel", "parallel"), vmem),
        name="diffattn",
    )(lam_p, z, z, z, tiles, subln)


def _xattn_body(q_ref, k_ref, v_ref, o_ref, *, blk):
    nb = q_ref.shape[0] // blk
    kb = k_ref[...]
    vb = v_ref[...]
    s_next = _dot_nt(q_ref[0:blk, :], kb)
    for j in range(nb):
        s = s_next
        if j + 1 < nb:
            s_next = _dot_nt(q_ref[(j + 1) * blk:(j + 2) * blk, :], kb)
        p = jnp.exp2(s - jnp.max(s, axis=-1, keepdims=True))
        o = _dot(p.astype(BF16), vb) / jnp.sum(p, axis=-1, keepdims=True)
        o_ref[j * blk:(j + 1) * blk, :] = o.astype(o_ref.dtype)


def _xattn(q, kv, *, batch, seq, mem_len, blk=256):
    d = q.shape[1]
    dh = d // X_HEADS
    blk = min(blk, seq)
    vmem = 4 * seq * dh * 2 + 4 * mem_len * dh * 2 + 8 * blk * (mem_len + dh) * 4 + (8 << 20)
    return pl.pallas_call(
        functools.partial(_xattn_body, blk=blk),
        out_shape=jax.ShapeDtypeStruct((batch * seq, d), BF16),
        grid=(batch, X_HEADS),
        in_specs=[
            pl.BlockSpec((seq, dh), lambda b, h: (b, h)),
            pl.BlockSpec((mem_len, dh), lambda b, h: (b, h)),
            pl.BlockSpec((mem_len, dh), lambda b, h: (b, X_HEADS + h)),
        ],
        out_specs=pl.BlockSpec((seq, dh), lambda b, h: (b, h)),
        compiler_params=_params(("parallel", "parallel"), vmem),
        name="xattn",
    )(q, kv, kv)


def kernel(x, mem, rel_bias, ffn1_norm_pre, ffn1_norm_post, ffn1_w_gate, ffn1_w_up, ffn1_w_down, mix_norm_pre, mix_norm_post, w_in, conv_w, conv_b, b_igate, b_fgate, mlstm_norm, diff_lambda, diff_subln, w_out, xattn_norm_pre, xattn_norm_post, mem_norm, xattn_wq, xattn_wk, xattn_wv, xattn_wo, ffn2_norm_pre, ffn2_norm_post, ffn2_w_gate, ffn2_w_up, ffn2_w_down):
    batch, seq, d = x.shape
    mem_len = mem.shape[1]
    depth = w_in.shape[0]
    n_gate = 2 * M_HEADS
    da_qw = DA_HEADS * 2 * DA_D
    gate0 = 2 * M_HEADS * M_DQK + 2 * M_HEADS * M_DV
    xf = x.reshape(batch * seq, d)
    memf = mem.reshape(batch * mem_len, d)
    row = lambda v: v.reshape(1, -1).astype(F32)

    tiles = _bias_tiles(rel_bias, min(DA_BLOCK, seq))
    w_main, w_gate = _cast_w_in(w_in, gate0=gate0, n_gate=n_gate, scaled=da_qw, scale=DA_D ** -0.5 * LOG2E)
    w_down = _cast_blocks([ffn1_w_down, ffn2_w_down])
    ffn1 = (_cast_blocks([ffn1_w_gate, ffn1_w_up], FFN_BLOCK, interleave=True), w_down, 0)
    ffn2 = (_cast_blocks([ffn2_w_gate, ffn2_w_up], FFN_BLOCK, interleave=True), w_down, 1)
    w_att = _cast_blocks([w_out, xattn_wo, xattn_wq, xattn_wk, xattn_wv],
                         scales=[None, None, (d // X_HEADS) ** -0.5 * LOG2E, None, None])

    xn = None
    for l in range(depth):
        lam_init = 0.8 - 0.6 * math.exp(-0.3 * l)
        xf, _ = _ffn(xf, row(ffn1_norm_pre[l]), row(ffn1_norm_post[l]), *ffn1, l, xn=xn)

        z, gates = _normproj(xf, row(mix_norm_pre[l]), w_main, l, w_gate, per_step=2, bm=INPROJ_ROWS)
        gate_bias = jnp.pad(jnp.concatenate([b_igate[l], b_fgate[l]]), (0, LANES - n_gate)).reshape(1, LANES)
        y_m = _mlstm(z, gates, gate_bias.astype(F32), conv_w[l].astype(F32), row(conv_b[l]),
                     row(mlstm_norm[l]), batch=batch, seq=seq)
        y_d = _diffattn(z, tiles, diff_lambda[l].astype(F32), row(diff_subln[l]),
                        batch=batch, seq=seq, lam_init=lam_init, col0=gate0)
        xf, _ = _outproj(xf, row(mix_norm_post[l]), [y_m, y_d], w_att, l, w_index=0)

        q = _normproj(xf, row(xattn_norm_pre[l]), w_att, l, block0=2, nb=1, bm=ATT_PROJ_ROWS)
        kv = _normproj(memf, row(mem_norm[l]), w_att, l, block0=3, nb=2, bm=ATT_PROJ_ROWS)
        c = _xattn(q, kv, batch=batch, seq=seq, mem_len=mem_len)
        xf, xn2 = _outproj(xf, row(xattn_norm_post[l]), [c], w_att, l, w_index=1, next_gain=row(ffn2_norm_pre[l]))

        next_gain = row(ffn1_norm_pre[l + 1]) if l + 1 < depth else None
        xf, xn = _ffn(xf, row(ffn2_norm_pre[l]), row(ffn2_norm_post[l]), *ffn2, l, xn=xn2, next_gain=next_gain)
    return xf.reshape(batch, seq, d)
```
